```python
import math
import numpy as np
import jax
import jax.numpy as jnp
from jax import lax

D_MODEL = 4096
BATCH = 8
SEQ = 2048
DEPTH = 2

CTX_LEN = 256
GRID_W = 64
HEAD_DIM = 128
MIX_WIDTH = D_MODEL
GDN_WIDTH = MIX_WIDTH // 2
NA_WIDTH = MIX_WIDTH - GDN_WIDTH
GDN_HEADS = GDN_WIDTH // HEAD_DIM
NA_HEADS = NA_WIDTH // HEAD_DIM
GDN_CHUNK = 64
SHORT_CONV = 5
NA_WIN_R = 8
NA_WIN_C = 16
NA_SPAN_C = 2 * NA_WIN_C
NA_COL_BLOCKS = GRID_W // NA_WIN_C
N_EXPERTS = 16
N_EXPERT_GROUPS = 4
EXPERTS_PER_GROUP = N_EXPERTS // N_EXPERT_GROUPS
TOP_K = 2
EXPERT_FF = 1024
N_MOD = 6
IN_COLS = 4 * GDN_WIDTH + 4 * GDN_HEADS + 3 * NA_WIDTH
NORM_EPS = 1e-6
MASK_VALUE = -1e30

kernel_name = 'hybrid_gdn_natten_moe_dit_block'


def _rmsnorm(t, w):
    tf = t.astype(jnp.float32)
    y = tf * lax.rsqrt(jnp.mean(tf * tf, axis=-1, keepdims=True) + NORM_EPS)
    return (y * w.astype(jnp.float32)).astype(t.dtype)


def _l2norm(t):
    tf = t.astype(jnp.float32)
    return tf * lax.rsqrt(jnp.sum(tf * tf, axis=-1, keepdims=True) + NORM_EPS)


def _modulate(t, w, shift, scale):
    return _rmsnorm(t, w) * (1 + scale) + shift


def _short_conv(t, w):
    ch = t.shape[-1]
    pad = SHORT_CONV // 2
    y = lax.conv_general_dilated(t, w[:, None, :].astype(t.dtype), window_strides=(1,),
                                 padding=[(pad, pad)], dimension_numbers=('NWC', 'WIO', 'NWC'),
                                 feature_group_count=ch)
    return jax.nn.silu(y)


def _gdn_inputs(qkv, ab, conv_w, a_log, dt_bias):
    bn, length, _ = qkv.shape
    q, k, v = jnp.split(_short_conv(qkv, conv_w), 3, axis=-1)
    shp = (bn, length, GDN_HEADS, HEAD_DIM)
    q = _l2norm(q.reshape(shp)) * (HEAD_DIM ** -0.5)
    k = _l2norm(k.reshape(shp))
    v = v.reshape(shp).astype(jnp.float32)
    ab = ab.astype(jnp.float32).reshape(bn, length, 2, 2, GDN_HEADS)
    beta = jax.nn.sigmoid(ab[:, :, 0])
    g = -jnp.exp(a_log.astype(jnp.float32)) * jax.nn.softplus(ab[:, :, 1] + dt_bias.astype(jnp.float32))

    def stack(f, b):
        return jnp.concatenate([f, jnp.flip(b, axis=1)], axis=0)

    return (stack(q, q), stack(k, k), stack(v, v),
            stack(g[:, :, 0], g[:, :, 1]), stack(beta[:, :, 0], beta[:, :, 1]))


def _gated_delta_chunked(q, k, v, g, beta, s0):
    n, length, nh, dk = q.shape
    dv = v.shape[-1]
    cs = GDN_CHUNK
    nc = length // cs

    def chunks(t):
        t = t.astype(jnp.float32).reshape((n, nc, cs, nh) + t.shape[3:])
        return jnp.moveaxis(t, 3, 2)

    q, k, v, g, beta = (chunks(t) for t in (q, k, v, g, beta))
    g = jnp.cumsum(g, axis=-1)
    tril = jnp.tril(jnp.ones((cs, cs), dtype=bool))
    strict = jnp.tril(jnp.ones((cs, cs), dtype=bool), -1)
    decay = jnp.exp(jnp.where(tril, g[..., :, None] - g[..., None, :], -jnp.inf))
    kb = k * beta[..., None]
    kk = jnp.einsum('...cd,...sd->...cs', kb, k) * decay
    a_mat = jnp.eye(cs, dtype=jnp.float32) + jnp.where(strict, kk, 0.0)
    rhs = jnp.concatenate([v * beta[..., None], kb * jnp.exp(g)[..., None]], axis=-1)
    sol = lax.linalg.triangular_solve(a_mat, rhs, left_side=True, lower=True, unit_diagonal=True)
    u, w = sol[..., :dv], sol[..., dv:]
    qk = jnp.einsum('...cd,...sd->...cs', q, k) * decay
    q_dec = q * jnp.exp(g)[..., None]
    k_dec = k * jnp.exp(g[..., -1:] - g)[..., None]
    g_last = jnp.exp(g[..., -1])

    def step(state, inp):
        qd, kd, u_i, w_i, qk_i, gl = inp
        v_new = u_i - jnp.einsum('nhcd,nhde->nhce', w_i, state)
        o = jnp.einsum('nhcd,nhde->nhce', qd, state) + jnp.einsum('nhcs,nhse->nhce', qk_i, v_new)
        state = state * gl[..., None, None] + jnp.einsum('nhcd,nhce->nhde', kd, v_new)
        return state, o

    xs = tuple(jnp.moveaxis(t, 1, 0) for t in (q_dec, k_dec, u, w, qk, g_last))
    s_fin, o = lax.scan(step, s0.astype(jnp.float32), xs)
    o = jnp.moveaxis(jnp.moveaxis(o, 0, 1), 2, 3).reshape(n, length, nh, dv)
    return o, s_fin


def _gdn_out(o_dir, z, norm_w):
    bn = z.shape[0]
    o = o_dir[:bn] + jnp.flip(o_dir[bn:], axis=1)
    zf = z.astype(jnp.float32).reshape(o.shape)
    y = _rmsnorm(o, norm_w) * jax.nn.silu(zf)
    return y.reshape(bn, o.shape[1], GDN_WIDTH).astype(z.dtype)


def _na_inputs(qkv, qk_norm_w):
    bn, length, _ = qkv.shape
    q, k, v = (t.reshape(bn, length, NA_HEADS, HEAD_DIM) for t in jnp.split(qkv, 3, axis=-1))
    q = _rmsnorm(q, qk_norm_w[0]) * (HEAD_DIM ** -0.5)
    k = _rmsnorm(k, qk_norm_w[1])
    return q, k, v


def _na_column_tables():
    qc = np.arange(GRID_W).reshape(NA_COL_BLOCKS, NA_WIN_C)
    span_start = np.clip(qc[:, 0] - NA_WIN_C // 2, 0, GRID_W - NA_SPAN_C)
    span_idx = span_start[:, None] + np.arange(NA_SPAN_C)
    win_start = np.clip(qc - NA_WIN_C // 2, 0, GRID_W - NA_WIN_C)
    kc = span_idx[:, None, :]
    valid = (kc >= win_start[..., None]) & (kc < win_start[..., None] + NA_WIN_C)
    off = np.clip(kc - qc[..., None] + NA_WIN_C - 1, 0, 2 * NA_WIN_C - 2)
    return span_idx, valid, off


def _na_latent(q, k, v, k_ctx, v_ctx, rpb):
    bn, s_len, nh, hd = q.shape
    rows = s_len // GRID_W
    wr = min(NA_WIN_R, rows)
    qg = q.reshape(bn, rows, GRID_W, nh, hd)
    kg = k.reshape(bn, rows, GRID_W, nh, hd)
    vg = v.reshape(bn, rows, GRID_W, nh, hd)
    span_idx, col_valid, col_off = _na_column_tables()
    rpb32 = rpb.astype(jnp.float32)
    n_loc = wr * NA_SPAN_C

    def row_fn(r):
        sr = jnp.clip(r - wr // 2, 0, rows - wr)
        kb = lax.dynamic_slice_in_dim(kg, sr, wr, axis=1)[:, :, span_idx]
        vb = lax.dynamic_slice_in_dim(vg, sr, wr, axis=1)[:, :, span_idx]
        qb = lax.dynamic_index_in_dim(qg, r, axis=1, keepdims=False).reshape(bn, NA_COL_BLOCKS, NA_WIN_C, nh, hd)
        s_loc = jnp.einsum('bjqhd,bwjkhd->bhjqwk', qb, kb, preferred_element_type=jnp.float32)
        row_off = sr - r + jnp.arange(wr) + (NA_WIN_R - 1)
        bias = jnp.transpose(rpb32[:, row_off][:, :, col_off], (0, 2, 3, 1, 4))
        s_loc = jnp.where(col_valid[:, :, None, :], s_loc + bias, MASK_VALUE)
        s_ctx = jnp.einsum('bjqhd,bkhd->bhjqk', qb, k_ctx, preferred_element_type=jnp.float32)
        s = jnp.concatenate([s_loc.reshape(bn, nh, NA_COL_BLOCKS, NA_WIN_C, n_loc), s_ctx], axis=-1)
        p = jax.nn.softmax(s, axis=-1).astype(v.dtype)
        p_loc = p[..., :n_loc].reshape(bn, nh, NA_COL_BLOCKS, NA_WIN_C, wr, NA_SPAN_C)
        o = (jnp.einsum('bhjqwk,bwjkhd->bjqhd', p_loc, vb)
             + jnp.einsum('bhjqk,bkhd->bjqhd', p[..., n_loc:], v_ctx))
        return o.reshape(bn, GRID_W, nh, hd)

    o = lax.map(row_fn, jnp.arange(rows))
    return jnp.moveaxis(o, 0, 1).reshape(bn, s_len, nh * hd)


def _ctx_attention(q, k, v):
    bn, length, nh, hd = q.shape
    s = jnp.einsum('bqhd,bkhd->bhqk', q, k, preferred_element_type=jnp.float32)
    p = jax.nn.softmax(s, axis=-1).astype(v.dtype)
    return jnp.einsum('bhqk,bkhd->bqhd', p, v).reshape(bn, length, nh * hd)


def _moe(h, w_router, b_router, w_gate, w_up, w_down):
    n_tok = h.shape[0]
    scores = jax.nn.sigmoid(jnp.dot(h, w_router, preferred_element_type=jnp.float32))
    sel = (scores + b_router.astype(jnp.float32)).reshape(n_tok, N_EXPERT_GROUPS, EXPERTS_PER_GROUP)
    grp_score = jnp.sum(lax.top_k(sel, TOP_K)[0], axis=-1)
    gsel = jnp.argmax(grp_score, axis=-1)
    in_grp = jnp.take_along_axis(sel, gsel[:, None, None], axis=1)[:, 0]
    eidx = gsel[:, None] * EXPERTS_PER_GROUP + lax.top_k(in_grp, TOP_K)[1]
    wsel = jnp.take_along_axis(scores, eidx, axis=-1)
    wsel = wsel / jnp.sum(wsel, axis=-1, keepdims=True)
    combine = jnp.einsum('tk,tke->te', wsel, jax.nn.one_hot(eidx, N_EXPERTS, dtype=jnp.float32)).astype(h.dtype)
    y = jnp.zeros_like(h)
    for e in range(N_EXPERTS):
        hid = jax.nn.silu(jnp.dot(h, w_gate[e])) * jnp.dot(h, w_up[e])
        y = y + combine[:, e:e + 1] * jnp.dot(hid, w_down[e])
    return y


def setup_inputs(seed: int = 0) -> dict:
    key = jax.random.key(seed)
    ks = jax.random.split(key, 24)
    f32 = jnp.float32
    nl = DEPTH

    def nrm(k, shape, s):
        return jax.random.normal(k, shape, f32) * s

    dt = jnp.exp(jax.random.uniform(ks[9], (nl, 2, GDN_HEADS), f32, math.log(1e-3), math.log(1e-1)))
    return {
        'x': nrm(ks[0], (BATCH, SEQ, D_MODEL), 1.0),
        'c': nrm(ks[1], (BATCH, D_MODEL), 1.0),
        'ctx': nrm(ks[2], (BATCH, CTX_LEN, D_MODEL), 1.0),
        'c_ctx': nrm(ks[3], (D_MODEL,), 1.0),
        'norm1_w': 1.0 + nrm(ks[4], (nl, D_MODEL), 0.02),
        'norm2_w': 1.0 + nrm(ks[5], (nl, D_MODEL), 0.02),
        'w_ada': nrm(ks[6], (nl, D_MODEL, N_MOD * D_MODEL), 0.5 * D_MODEL ** -0.5),
        'b_ada': nrm(ks[7], (nl, N_MOD * D_MODEL), 0.02),
        'w_in': nrm(ks[8], (nl, D_MODEL, IN_COLS), D_MODEL ** -0.5),
        'conv_w': nrm(ks[10], (nl, SHORT_CONV, 3 * GDN_WIDTH), SHORT_CONV ** -0.5),
        'a_log': jnp.log(jax.random.uniform(ks[11], (nl, 2, GDN_HEADS), f32, 1.0, 16.0)),
        'dt_bias': dt + jnp.log(-jnp.expm1(-dt)),
        'gdn_norm_w': 1.0 + nrm(ks[12], (nl, HEAD_DIM), 0.02),
        'qk_norm_w': 1.0 + nrm(ks[13], (nl, 2, HEAD_DIM), 0.02),
        'na_rpb': nrm(ks[14], (nl, NA_HEADS, 2 * NA_WIN_R - 1, 2 * NA_WIN_C - 1), 0.1),
        'w_out': nrm(ks[15], (nl, MIX_WIDTH, D_MODEL), MIX_WIDTH ** -0.5),
        'w_router': nrm(ks[16], (D_MODEL, N_EXPERTS), D_MODEL ** -0.5),
        'b_router': nrm(ks[17], (N_EXPERTS,), 0.01),
        'w_gate': nrm(ks[18], (nl, N_EXPERTS, D_MODEL, EXPERT_FF), D_MODEL ** -0.5),
        'w_up': nrm(ks[19], (nl, N_EXPERTS, D_MODEL, EXPERT_FF), D_MODEL ** -0.5),
        'w_down': nrm(ks[20], (nl, N_EXPERTS, EXPERT_FF, D_MODEL), EXPERT_FF ** -0.5),
    }


def reference(x, c, ctx, c_ctx, norm1_w, norm2_w, w_ada, b_ada, w_in, conv_w, a_log, dt_bias,
              gdn_norm_w, qk_norm_w, na_rpb, w_out, w_router, b_router, w_gate, w_up, w_down):
    bn, s_len, dm = x.shape
    lc = ctx.shape[1]
    split_at = [3 * GDN_WIDTH, 4 * GDN_WIDTH, 4 * GDN_WIDTH + 4 * GDN_HEADS]
    h = ctx
    for layer in range(DEPTH):
        last = layer == DEPTH - 1
        mod_l = (jnp.dot(jax.nn.silu(c), w_ada[layer]) + b_ada[layer]).reshape(bn, N_MOD, 1, dm)
        mod_c = (jnp.dot(jax.nn.silu(c_ctx), w_ada[layer]) + b_ada[layer]).reshape(N_MOD, 1, dm)

        a_l = _modulate(x, norm1_w[layer], mod_l[:, 0], mod_l[:, 1])
        a_c = _modulate(h, norm1_w[layer], mod_c[0], mod_c[1])
        proj = jnp.dot(jnp.concatenate([a_c, a_l], axis=1), w_in[layer])
        qkv_a, z_a, ab_a, qkv_b = jnp.split(proj, split_at, axis=-1)

        qc, kc, vc, gc, bc = _gdn_inputs(qkv_a[:, :lc], ab_a[:, :lc], conv_w[layer], a_log[layer], dt_bias[layer])
        ql, kl, vl, gl, bl = _gdn_inputs(qkv_a[:, lc:], ab_a[:, lc:], conv_w[layer], a_log[layer], dt_bias[layer])
        s0 = jnp.zeros((2 * bn, GDN_HEADS, HEAD_DIM, HEAD_DIM), jnp.float32)
        oa_c, s_ctx = _gated_delta_chunked(qc, kc, vc, gc, bc, s0)
        oa_l, _ = _gated_delta_chunked(ql, kl, vl, gl, bl, s_ctx)
        ya_l = _gdn_out(oa_l, z_a[:, lc:], gdn_norm_w[layer])

        qb, kb, vb = _na_inputs(qkv_b, qk_norm_w[layer])
        yb_l = _na_latent(qb[:, lc:], kb[:, lc:], vb[:, lc:], kb[:, :lc], vb[:, :lc], na_rpb[layer])

        x = x + mod_l[:, 2] * jnp.dot(jnp.concatenate([ya_l, yb_l], axis=-1), w_out[layer])
        if not last:
            ya_c = _gdn_out(oa_c, z_a[:, :lc], gdn_norm_w[layer])
            yb_c = _ctx_attention(qb[:, :lc], kb[:, :lc], vb[:, :lc])
            h = h + mod_c[2] * jnp.dot(jnp.concatenate([ya_c, yb_c], axis=-1), w_out[layer])

        f_l = _modulate(x, norm2_w[layer], mod_l[:, 3], mod_l[:, 4])
        if last:
            y = _moe(f_l.reshape(-1, dm), w_router, b_router, w_gate[layer], w_up[layer], w_down[layer])
            x = x + mod_l[:, 5] * y.reshape(bn, s_len, dm)
        else:
            f_c = _modulate(h, norm2_w[layer], mod_c[3], mod_c[4])
            y = _moe(jnp.concatenate([f_c, f_l], axis=1).reshape(-1, dm), w_router, b_router,
                     w_gate[layer], w_up[layer], w_down[layer]).reshape(bn, lc + s_len, dm)
            h = h + mod_c[5] * y[:, :lc]
            x = x + mod_l[:, 5] * y[:, lc:]
    return x
```

```python
import functools
import math

import numpy as np
import jax
import jax.numpy as jnp
from jax import lax
from jax.experimental import pallas as pl
from jax.experimental.pallas import tpu as pltpu

HEAD_DIM = 128
GRID_W = 64
GDN_CHUNK = 64
SHORT_CONV = 5
NA_WIN_R = 8
NA_WIN_C = 16
N_EXPERT_GROUPS = 4
TOP_K = 2
N_MOD = 6
NORM_EPS = 1e-6
MASK_VALUE = -1e30

ROW_TILE = 256
MOE_TILE = 256
GATE_LANES = 128
VMEM_LIMIT = 56 * 1024 * 1024

_F32 = jnp.float32
_BF16 = jnp.bfloat16
_HI = lax.Precision.HIGHEST


def _cp(sem, vmem=VMEM_LIMIT):
    return pltpu.CompilerParams(dimension_semantics=sem, vmem_limit_bytes=vmem)


def _pick(n, cands):
    for c in cands:
        if n % c == 0:
            return c
    raise ValueError(f"no tile for {n} in {cands}")


def _dot(a, b):
    return jnp.dot(a, b, preferred_element_type=_F32)


def _dot_nt(a, b, precision=None):
    return lax.dot_general(a, b, (((1,), (1,)), ((), ())), preferred_element_type=_F32, precision=precision)


def _dot_tn(a, b):
    return lax.dot_general(a, b, (((0,), (0,)), ((), ())), preferred_element_type=_F32)


def _sigmoid(x):
    return 1.0 / (1.0 + jnp.exp(-x))


def _silu(x):
    return x * _sigmoid(x)


def _ada_kernel(cc_ref, w_ref, b_ref, o_ref):
    a = _silu(cc_ref[...]).astype(_BF16)
    o_ref[0] = _dot(a, w_ref[0].astype(_BF16)) + b_ref[0]


def _ada(cc, w_ada, b_ada):
    nl, d, n = w_ada.shape
    mp = cc.shape[0]
    tn = _pick(n, (512, 256, 128))
    return pl.pallas_call(
        _ada_kernel,
        grid=(nl, n // tn),
        in_specs=[
            pl.BlockSpec((mp, d), lambda l, j: (0, 0)),
            pl.BlockSpec((1, d, tn), lambda l, j: (l, 0, j)),
            pl.BlockSpec((1, 1, tn), lambda l, j: (l, 0, j)),
        ],
        out_specs=pl.BlockSpec((1, mp, tn), lambda l, j: (l, 0, j)),
        out_shape=jax.ShapeDtypeStruct((nl, mp, n), _F32),
        compiler_params=_cp(("arbitrary", "arbitrary")),
        name="ada_mod",
    )(cc, w_ada, b_ada.reshape(nl, 1, n))


def _modnorm(x, nw, mod, k_shift, k_scale):
    ms = jnp.mean(x * x, axis=-1, keepdims=True)
    y = x * lax.rsqrt(ms + NORM_EPS) * nw
    return y * (1.0 + mod[k_scale:k_scale + 1, :]) + mod[k_shift:k_shift + 1, :]


def _norm_kernel(x_ref, nw_ref, mod_ref, o_ref, *, k_shift, k_scale):
    y = _modnorm(x_ref[0], nw_ref[...], mod_ref[0], k_shift, k_scale)
    o_ref[0] = y.astype(o_ref.dtype)


def _norm_router_kernel(x_ref, nw_ref, mod_ref, wr_ref, o_ref, lg_ref, *, k_shift, k_scale):
    y = _modnorm(x_ref[0], nw_ref[...], mod_ref[0], k_shift, k_scale)
    o_ref[0] = y.astype(o_ref.dtype)
    lg_ref[...] = _dot_nt(wr_ref[...], y, precision=_HI)


def _mod_row(bsz, n_ctx_tiles):
    return lambda b, i: (jnp.where(i < n_ctx_tiles, bsz, b), 0, 0)


def _norm(xx, nw, mod, lc, k_shift, k_scale, w_router_t=None):
    bsz, lt, d = xx.shape
    tr = ROW_TILE
    nt = lt // tr
    in_specs = [
        pl.BlockSpec((1, tr, d), lambda b, i: (b, i, 0)),
        pl.BlockSpec((1, d), lambda b, i: (0, 0)),
        pl.BlockSpec((1, N_MOD, d), _mod_row(bsz, lc // tr)),
    ]
    a_spec = pl.BlockSpec((1, tr, d), lambda b, i: (b, i, 0))
    a_shape = jax.ShapeDtypeStruct((bsz, lt, d), _BF16)
    if w_router_t is None:
        return pl.pallas_call(
            functools.partial(_norm_kernel, k_shift=k_shift, k_scale=k_scale),
            grid=(bsz, nt), in_specs=in_specs, out_specs=a_spec, out_shape=a_shape,
            compiler_params=_cp(("arbitrary", "arbitrary")), name="modnorm",
        )(xx, nw.reshape(1, d), mod)
    ne = w_router_t.shape[0]
    return pl.pallas_call(
        functools.partial(_norm_router_kernel, k_shift=k_shift, k_scale=k_scale),
        grid=(bsz, nt),
        in_specs=in_specs + [pl.BlockSpec((ne, d), lambda b, i: (0, 0))],
        out_specs=[a_spec, pl.BlockSpec((ne, tr), lambda b, i: (0, b * nt + i))],
        out_shape=[a_shape, jax.ShapeDtypeStruct((ne, bsz * lt), _F32)],
        compiler_params=_cp(("arbitrary", "arbitrary")), name="modnorm_router",
    )(xx, nw.reshape(1, d), mod, w_router_t)


def _mm_kernel(a_ref, w_ref, o_ref):
    o_ref[...] = _dot(a_ref[...], w_ref[...]).astype(o_ref.dtype)


def _matmul(a, w, out_dtype):
    m, k = a.shape
    n = w.shape[1]
    tm = _pick(m, (1024, 512, 256, 128))
    tn = _pick(n, (1024, 512, 256, 128))
    return pl.pallas_call(
        _mm_kernel,
        grid=(n // tn, m // tm),
        in_specs=[pl.BlockSpec((tm, k), lambda j, i: (i, 0)),
                  pl.BlockSpec((k, tn), lambda j, i: (0, j))],
        out_specs=pl.BlockSpec((tm, tn), lambda j, i: (i, j)),
        out_shape=jax.ShapeDtypeStruct((m, n), out_dtype),
        compiler_params=_cp(("arbitrary", "arbitrary")), name="matmul",
    )(a, w)


def _conv_silu(x, w, lc):
    lt = x.shape[0]
    row = lax.broadcasted_iota(jnp.int32, x.shape, 0)
    pad = SHORT_CONV // 2
    acc = x * w[pad:pad + 1, :]
    for j in range(SHORT_CONV):
        o = j - pad
        if o == 0:
            continue
        xs = pltpu.roll(x, (-o) % lt, 0)
        if o < 0:
            bad = (row < -o) | ((row >= lc) & (row < lc - o))
        else:
            bad = ((row >= lc - o) & (row < lc)) | (row >= lt - o)
        acc = acc + jnp.where(bad, 0.0, xs) * w[j:j + 1, :]
    return _silu(acc)


def _l2n(t):
    return t * lax.rsqrt(jnp.sum(t * t, axis=-1, keepdims=True) + NORM_EPS)


def _gprep_kernel(q_ref, k_ref, v_ref, wq_ref, wk_ref, wv_ref, qo_ref, ko_ref, vo_ref, *, lc):
    q = _conv_silu(q_ref[0].astype(_F32), wq_ref[...], lc)
    qo_ref[0] = (_l2n(q) * (HEAD_DIM ** -0.5)).astype(qo_ref.dtype)
    k = _conv_silu(k_ref[0].astype(_F32), wk_ref[...], lc)
    ko_ref[0] = _l2n(k).astype(ko_ref.dtype)
    v = _conv_silu(v_ref[0].astype(_F32), wv_ref[...], lc)
    vo_ref[0] = v.astype(vo_ref.dtype)


def _gprep(proj, conv_w, lc, nh):
    bsz, lt, _ = proj.shape
    hd = HEAD_DIM

    def xs(off):
        return pl.BlockSpec((1, lt, hd), lambda b, h: (b, 0, off + h))

    def ws(off):
        return pl.BlockSpec((SHORT_CONV, hd), lambda b, h: (0, off + h))

    o_spec = pl.BlockSpec((1, lt, hd), lambda b, h: (b, 0, h))
    o_shape = jax.ShapeDtypeStruct((bsz, lt, nh * hd), _BF16)
    return pl.pallas_call(
        functools.partial(_gprep_kernel, lc=lc),
        grid=(bsz, nh),
        in_specs=[xs(0), xs(nh), xs(2 * nh), ws(0), ws(nh), ws(2 * nh)],
        out_specs=[o_spec, o_spec, o_spec],
        out_shape=[o_shape, o_shape, o_shape],
        compiler_params=_cp(("arbitrary", "arbitrary")), name="gdn_prep",
    )(proj, proj, proj, conv_w, conv_w, conv_w)


def _gdn_masks():
    c = GDN_CHUNK
    i = np.arange(c)[:, None]
    j = np.arange(c)[None, :]
    tri = np.stack([(i >= j), (i <= j)]).astype(np.float32)
    eye = np.eye(c, dtype=np.float32)
    lv = [(i // 2 == j // 2)]
    s = 2
    while s < c:
        lv.append((i // (2 * s) == j // (2 * s)) & (i // s != j // s))
        s *= 2
    return tri, eye, np.stack(lv).astype(np.float32)


def _gdn_kernel(q_ref, k_ref, v_ref, ab_ref, al_ref, dtb_ref, tri_ref, eye_ref, lv_ref, o_ref, s_ref, *, hg):
    c = GDN_CHUNK
    hd = HEAD_DIM
    t = pl.program_id(3)

    @pl.when(t == 0)
    def _():
        s_ref[...] = jnp.zeros_like(s_ref)

    tri = tri_ref[0]
    eye = eye_ref[...]
    strict = tri - eye
    ab = ab_ref[0, 0, 0]
    beta = _sigmoid(ab)
    z = ab + dtb_ref[0, 0]
    softplus = jnp.maximum(z, 0.0) + jnp.log(1.0 + jnp.exp(-jnp.abs(z)))
    g = -jnp.exp(al_ref[0, 0]) * softplus
    gc = jnp.dot(tri, g, preferred_element_type=_F32, precision=_HI)
    gtot = jnp.sum(g, axis=0, keepdims=True)
    n_lv = lv_ref.shape[0]

    for j in range(hg):
        sl = slice(j * hd, (j + 1) * hd)
        b = beta[:, j:j + 1]
        gj = gc[:, hg + j:hg + j + 1]
        gt = gtot[:, hg + j:hg + j + 1]
        eg = jnp.exp(gj)
        kf = k_ref[0, :, sl].astype(_F32)
        vf = v_ref[0, :, sl].astype(_F32)
        qf = q_ref[0, :, sl].astype(_F32)
        kb16 = k_ref[0, :, sl]
        q16 = q_ref[0, :, sl]

        gcol = jnp.broadcast_to(gj, (c, c))
        grow = jnp.sum(gcol * eye, axis=0, keepdims=True)
        decay = jnp.exp(jnp.where(tri > 0.0, gcol - grow, -jnp.inf))

        kbeta = kf * b
        kk = _dot_nt(kbeta.astype(_BF16), kb16) * decay
        lmat = kk * strict
        dinv = eye - lmat * lv_ref[0]
        for lv in range(1, n_lv):
            wmat = _dot((lmat * lv_ref[lv]).astype(_BF16), dinv.astype(_BF16))
            dinv = dinv - _dot(dinv.astype(_BF16), wmat.astype(_BF16))

        rhs = jnp.concatenate([vf * b, kbeta * eg], axis=-1).astype(_BF16)
        uw = _dot(dinv.astype(_BF16), rhs)
        u = uw[:, :hd]
        w = uw[:, hd:]
        qk = _dot_nt(q16, kb16) * decay
        q_dec = (qf * eg).astype(_BF16)
        k_dec = (kf * jnp.exp(gt - gj)).astype(_BF16)

        s = s_ref[j]
        s16 = s.astype(_BF16)
        v_new = u - _dot(w.astype(_BF16), s16)
        v16 = v_new.astype(_BF16)
        o = _dot(q_dec, s16) + _dot(qk.astype(_BF16), v16)
        s_ref[j] = s * jnp.exp(gt) + _dot_tn(k_dec, v16)
        o_ref[0, 0, :, sl] = o.astype(o_ref.dtype)


def _gdn(q, k, v, ab_r, alog_r, dtb_r, lc, hg):
    bsz, lt, gw = q.shape
    nh = gw // HEAD_DIM
    nhg = nh // hg
    c = GDN_CHUNK
    nc_c = lc // c
    nc = lt // c
    tri, eye, lvm = _gdn_masks()

    def chunk(d, t):
        bwd = jnp.where(t < nc_c, nc_c - 1 - t, nc - 1 + nc_c - t)
        return jnp.where(d == 0, t, bwd)

    x_spec = pl.BlockSpec((1, c, hg * HEAD_DIM), lambda b, d, g, t: (b, chunk(d, t), g))
    return pl.pallas_call(
        functools.partial(_gdn_kernel, hg=hg),
        grid=(bsz, 2, nhg, nc),
        in_specs=[
            x_spec, x_spec, x_spec,
            pl.BlockSpec((1, 1, 1, c, GATE_LANES), lambda b, d, g, t: (d, g, b, chunk(d, t), 0)),
            pl.BlockSpec((1, 1, 1, GATE_LANES), lambda b, d, g, t: (d, g, 0, 0)),
            pl.BlockSpec((1, 1, 1, GATE_LANES), lambda b, d, g, t: (d, g, 0, 0)),
            pl.BlockSpec((1, c, c), lambda b, d, g, t: (d, 0, 0)),
            pl.BlockSpec((c, c), lambda b, d, g, t: (0, 0)),
            pl.BlockSpec(lvm.shape, lambda b, d, g, t: (0, 0, 0)),
        ],
        out_specs=pl.BlockSpec((1, 1, c, hg * HEAD_DIM), lambda b, d, g, t: (d, b, chunk(d, t), g)),
        out_shape=jax.ShapeDtypeStruct((2, bsz, lt, gw), _F32),
        scratch_shapes=[pltpu.VMEM((hg, HEAD_DIM, HEAD_DIM), _F32)],
        compiler_params=_cp(("arbitrary", "arbitrary", "arbitrary", "arbitrary")), name="gdn_scan",
    )(q, k, v, ab_r, alog_r, dtb_r, jnp.asarray(tri), jnp.asarray(eye), jnp.asarray(lvm))


def _gdnout_kernel(of_ref, ob_ref, z_ref, nw_ref, y_ref, *, nh):
    hd = HEAD_DIM
    for h in range(nh):
        sl = slice(h * hd, (h + 1) * hd)
        o = of_ref[0, 0, :, sl] + ob_ref[0, 0, :, sl]
        ms = jnp.mean(o * o, axis=-1, keepdims=True)
        y = o * lax.rsqrt(ms + NORM_EPS) * nw_ref[...]
        y_ref[0, :, sl] = (y * _silu(z_ref[0, :, sl].astype(_F32))).astype(y_ref.dtype)


def _gdnout(o_dir, proj, z_off_blocks, norm_w):
    _, bsz, lt, gw = o_dir.shape
    nh = gw // HEAD_DIM
    tr = ROW_TILE
    return pl.pallas_call(
        functools.partial(_gdnout_kernel, nh=nh),
        grid=(bsz, lt // tr),
        in_specs=[
            pl.BlockSpec((1, 1, tr, gw), lambda b, i: (0, b, i, 0)),
            pl.BlockSpec((1, 1, tr, gw), lambda b, i: (1, b, i, 0)),
            pl.BlockSpec((1, tr, gw), lambda b, i: (b, i, z_off_blocks)),
            pl.BlockSpec((1, HEAD_DIM), lambda b, i: (0, 0)),
        ],
        out_specs=pl.BlockSpec((1, tr, gw), lambda b, i: (b, i, 0)),
        out_shape=jax.ShapeDtypeStruct((bsz, lt, gw), _BF16),
        compiler_params=_cp(("arbitrary", "arbitrary")), name="gdn_out",
    )(o_dir, o_dir, proj, norm_w.reshape(1, HEAD_DIM))


def _na_bias_table(rpb, rows):
    wc = NA_WIN_C
    qc = np.arange(GRID_W)[:, None]
    kc = np.arange(GRID_W)[None, :]
    win_start = np.clip(qc - wc // 2, 0, GRID_W - wc)
    valid = (kc >= win_start) & (kc < win_start + wc)
    off = np.clip(kc - qc + wc - 1, 0, 2 * wc - 2)
    ro = np.arange(NA_WIN_R)[:, None] + np.arange(NA_WIN_R)[None, :]
    tab = rpb.astype(_F32)[:, ro][:, :, :, off]
    tab = jnp.where(valid[None, None, None], tab, MASK_VALUE)
    tab = jnp.transpose(tab, (0, 1, 3, 2, 4))
    return tab.reshape(rpb.shape[0], NA_WIN_R, GRID_W, NA_WIN_R * GRID_W)


def _rms_head(t, w):
    return t * lax.rsqrt(jnp.mean(t * t, axis=-1, keepdims=True) + NORM_EPS) * w


def _na_kernel(q_ref, k_ref, v_ref, tab_ref, nw_ref, o_ref, kn_ref, *, hg, lc, rows):
    hd = HEAD_DIM
    w = GRID_W
    nloc = NA_WIN_R * w
    ncc = lc // w
    r = pl.program_id(2)

    @pl.when(r == 0)
    def _():
        for j in range(hg):
            sl = slice(j * hd, (j + 1) * hd)
            kn_ref[:, sl] = _rms_head(k_ref[0, :, sl].astype(_F32), nw_ref[1:2, :]).astype(kn_ref.dtype)

    rr = r - ncc
    sr = jnp.clip(rr - NA_WIN_R // 2, 0, rows - NA_WIN_R)
    start = pl.multiple_of(lc + sr * w, w)
    ro = jnp.clip(sr - rr + NA_WIN_R - 1, 0, NA_WIN_R - 1)

    for j in range(hg):
        sl = slice(j * hd, (j + 1) * hd)
        qn = (_rms_head(q_ref[0, :, sl].astype(_F32), nw_ref[0:1, :]) * (hd ** -0.5)).astype(_BF16)
        s_ctx = _dot_nt(qn, kn_ref[0:lc, sl])
        v_ctx = v_ref[0, 0:lc, sl]

        @pl.when(r < ncc)
        def _():
            m = jnp.max(s_ctx, axis=-1, keepdims=True)
            p = jnp.exp(s_ctx - m)
            den = jnp.sum(p, axis=-1, keepdims=True)
            o_ref[0, :, sl] = (_dot(p.astype(_BF16), v_ctx) / den).astype(o_ref.dtype)

        @pl.when(r >= ncc)
        def _():
            s_loc = _dot_nt(qn, kn_ref[pl.ds(start, nloc), sl]) + tab_ref[j, ro]
            m = jnp.maximum(jnp.max(s_loc, axis=-1, keepdims=True), jnp.max(s_ctx, axis=-1, keepdims=True))
            p_loc = jnp.exp(s_loc - m)
            p_ctx = jnp.exp(s_ctx - m)
            den = jnp.sum(p_loc, axis=-1, keepdims=True) + jnp.sum(p_ctx, axis=-1, keepdims=True)
            o = _dot(p_loc.astype(_BF16), v_ref[0, pl.ds(start, nloc), sl]) + _dot(p_ctx.astype(_BF16), v_ctx)
            o_ref[0, :, sl] = (o / den).astype(o_ref.dtype)


def _na(proj, q_off, k_off, v_off, tab, qk_norm_w, lc, nh, hg):
    bsz, lt, _ = proj.shape
    rows = (lt - lc) // GRID_W
    nhg = nh // hg
    bw = hg * HEAD_DIM
    return pl.pallas_call(
        functools.partial(_na_kernel, hg=hg, lc=lc, rows=rows),
        grid=(bsz, nhg, lt // GRID_W),
        in_specs=[
            pl.BlockSpec((1, GRID_W, bw), lambda b, g, r: (b, r, q_off + g)),
            pl.BlockSpec((1, lt, bw), lambda b, g, r: (b, 0, k_off + g)),
            pl.BlockSpec((1, lt, bw), lambda b, g, r: (b, 0, v_off + g)),
            pl.BlockSpec((hg,) + tab.shape[1:], lambda b, g, r: (g, 0, 0, 0)),
            pl.BlockSpec((2, HEAD_DIM), lambda b, g, r: (0, 0)),
        ],
        out_specs=pl.BlockSpec((1, GRID_W, bw), lambda b, g, r: (b, r, g)),
        out_shape=jax.ShapeDtypeStruct((bsz, lt, nh * HEAD_DIM), _BF16),
        scratch_shapes=[pltpu.VMEM((lt, bw), _BF16)],
        compiler_params=_cp(("arbitrary", "arbitrary", "arbitrary")), name="natten",
    )(proj, proj, proj, tab, qk_norm_w)


def _outproj_kernel(ya_ref, yb_ref, wa_ref, wb_ref, x_ref, ml_ref, mc_ref, o_ref, *, lc, tiles_per_sample, k_gate):
    tm = x_ref.shape[0]
    acc = _dot(ya_ref[...], wa_ref[...]) + _dot(yb_ref[...], wb_ref[...])
    row = (pl.program_id(0) % tiles_per_sample) * tm + lax.broadcasted_iota(jnp.int32, acc.shape, 0)
    gate = jnp.where(row < lc, mc_ref[0, k_gate:k_gate + 1, :], ml_ref[0, k_gate:k_gate + 1, :])
    o_ref[...] = x_ref[...] + gate * acc


def _outproj(ya, yb, w_out, xx, mod, lc, k_gate):
    bsz, lt, d = xx.shape
    t = bsz * lt
    ga = ya.shape[-1]
    gb = yb.shape[-1]
    tm = _pick(lt, (1152, 768, 576, 384, 256, 128))
    tn = _pick(d, (512, 256, 128))
    tps = lt // tm
    out = pl.pallas_call(
        functools.partial(_outproj_kernel, lc=lc, tiles_per_sample=tps, k_gate=k_gate),
        grid=(t // tm, d // tn),
        in_specs=[
            pl.BlockSpec((tm, ga), lambda i, j: (i, 0)),
            pl.BlockSpec((tm, gb), lambda i, j: (i, 0)),
            pl.BlockSpec((ga, tn), lambda i, j: (0, j)),
            pl.BlockSpec((gb, tn), lambda i, j: (ga // gb, j)),
            pl.BlockSpec((tm, tn), lambda i, j: (i, j)),
            pl.BlockSpec((1, N_MOD, tn), lambda i, j: (i // tps, 0, j)),
            pl.BlockSpec((1, N_MOD, tn), lambda i, j: (bsz, 0, j)),
        ],
        out_specs=pl.BlockSpec((tm, tn), lambda i, j: (i, j)),
        out_shape=jax.ShapeDtypeStruct((t, d), _F32),
        compiler_params=_cp(("arbitrary", "arbitrary")), name="out_proj",
    )(ya.reshape(t, ga), yb.reshape(t, gb), w_out, w_out, xx.reshape(t, d), mod, mod)
    return out.reshape(bsz, lt, d)


def _route_kernel(lg_ref, br_ref, su_ref, oi_ref, ow_ref, cnt_ref, carry_ref, *, ne):
    epg = ne // N_EXPERT_GROUPS
    tq = lg_ref.shape[1]

    @pl.when(pl.program_id(0) == 0)
    def _():
        carry_ref[...] = jnp.zeros_like(carry_ref)

    sc = _sigmoid(lg_ref[...])
    sel = sc + br_ref[...]
    s = [sel[e:e + 1, :] for e in range(ne)]

    def top2(vals):
        a, b, c_, d_ = vals
        hi1, lo1 = jnp.maximum(a, b), jnp.minimum(a, b)
        hi2, lo2 = jnp.maximum(c_, d_), jnp.minimum(c_, d_)
        return jnp.maximum(hi1, hi2) + jnp.maximum(jnp.maximum(lo1, lo2), jnp.minimum(hi1, hi2))

    gs = [top2(s[g * epg:(g + 1) * epg]) for g in range(N_EXPERT_GROUPS)]
    best = gs[0]
    gsel = jnp.zeros_like(best, dtype=jnp.int32)
    for g in range(1, N_EXPERT_GROUPS):
        better = gs[g] > best
        gsel = jnp.where(better, g, gsel)
        best = jnp.maximum(best, gs[g])
    ing = []
    for j in range(epg):
        v = s[j]
        for g in range(1, N_EXPERT_GROUPS):
            v = jnp.where(gsel == g, s[g * epg + j], v)
        ing.append(v)
    b1 = ing[0]
    i1 = jnp.zeros_like(gsel)
    for j in range(1, epg):
        better = ing[j] > b1
        i1 = jnp.where(better, j, i1)
        b1 = jnp.maximum(b1, ing[j])
    b2 = jnp.full_like(b1, -jnp.inf)
    i2 = jnp.zeros_like(gsel)
    for j in range(epg):
        better = (ing[j] > b2) & (i1 != j)
        i2 = jnp.where(better, j, i2)
        b2 = jnp.where(better, ing[j], b2)
    e1 = gsel * epg + i1
    e2 = gsel * epg + i2

    eio = lax.broadcasted_iota(jnp.int32, (ne, tq), 0)
    oh1 = eio == e1
    oh2 = eio == e2
    w1 = jnp.sum(jnp.where(oh1, sc, 0.0), axis=0, keepdims=True)
    w2 = jnp.sum(jnp.where(oh2, sc, 0.0), axis=0, keepdims=True)
    wsum = w1 + w2
    oh = jnp.where(oh1, 1.0, 0.0) + jnp.where(oh2, 1.0, 0.0)
    rank = _dot(oh.astype(_BF16), su_ref[...]) + carry_ref[:, 0:1]
    r1 = jnp.sum(jnp.where(oh1, rank, 0.0), axis=0, keepdims=True)
    r2 = jnp.sum(jnp.where(oh2, rank, 0.0), axis=0, keepdims=True)
    carry = carry_ref[...] + jnp.sum(oh, axis=1, keepdims=True)
    carry_ref[...] = carry
    cnt_ref[...] = carry

    sub = lax.broadcasted_iota(jnp.int32, (8, tq), 0)
    oi = jnp.where(sub == 0, e1, jnp.where(sub == 1, e2,
         jnp.where(sub == 2, r1.astype(jnp.int32), jnp.where(sub == 3, r2.astype(jnp.int32), 0))))
    oi_ref[...] = oi
    ow_ref[...] = jnp.where(sub == 0, w1 / wsum, jnp.where(sub == 1, w2 / wsum, 0.0))


def _route(logits_t, b_router):
    ne, t = logits_t.shape
    tq = ROW_TILE
    su = np.triu(np.ones((tq, tq), np.float32), 1)
    return pl.pallas_call(
        functools.partial(_route_kernel, ne=ne),
        grid=(t // tq,),
        in_specs=[pl.BlockSpec((ne, tq), lambda i: (0, i)),
                  pl.BlockSpec((ne, 1), lambda i: (0, 0)),
                  pl.BlockSpec((tq, tq), lambda i: (0, 0))],
        out_specs=[pl.BlockSpec((8, tq), lambda i: (0, i)),
                   pl.BlockSpec((8, tq), lambda i: (0, i)),
                   pl.BlockSpec((ne, 128), lambda i: (0, 0))],
        out_shape=[jax.ShapeDtypeStruct((8, t), jnp.int32),
                   jax.ShapeDtypeStruct((8, t), _F32),
                   jax.ShapeDtypeStruct((ne, 128), _F32)],
        scratch_shapes=[pltpu.VMEM((ne, 128), _F32)],
        compiler_params=_cp(("arbitrary",)), name="router",
    )(logits_t, b_router.reshape(ne, 1).astype(_F32), jnp.asarray(su, _BF16))


def _moe_up_kernel(te_ref, x_ref, wg_ref, wu_ref, h_ref):
    n_valid = te_ref[te_ref.shape[0] - 1]

    @pl.when(pl.program_id(0) < n_valid)
    def _():
        x = x_ref[...]
        gt = _dot(x, wg_ref[0])
        up = _dot(x, wu_ref[0])
        h_ref[...] = (_silu(gt) * up).astype(h_ref.dtype)

    @pl.when(pl.program_id(0) >= n_valid)
    def _():
        h_ref[...] = jnp.zeros_like(h_ref)


def _moe_down_kernel(te_ref, h_ref, wd_ref, o_ref):
    n_valid = te_ref[te_ref.shape[0] - 1]

    @pl.when(pl.program_id(0) < n_valid)
    def _():
        o_ref[...] = _dot(h_ref[...], wd_ref[0]).astype(o_ref.dtype)

    @pl.when(pl.program_id(0) >= n_valid)
    def _():
        o_ref[...] = jnp.zeros_like(o_ref)


def _moe_experts(xs, tile_e, w_gate, w_up, w_down):
    p, d = xs.shape
    ne, _, f = w_gate.shape
    tm = MOE_TILE
    nt = p // tm
    hid = pl.pallas_call(
        _moe_up_kernel,
        grid_spec=pltpu.PrefetchScalarGridSpec(
            num_scalar_prefetch=1, grid=(nt,),
            in_specs=[pl.BlockSpec((tm, d), lambda i, te: (i, 0)),
                      pl.BlockSpec((1, d, f), lambda i, te: (te[i], 0, 0)),
                      pl.BlockSpec((1, d, f), lambda i, te: (te[i], 0, 0))],
            out_specs=pl.BlockSpec((tm, f), lambda i, te: (i, 0))),
        out_shape=jax.ShapeDtypeStruct((p, f), _BF16),
        compiler_params=_cp(("arbitrary",)), name="moe_up",
    )(tile_e, xs, w_gate, w_up)
    return pl.pallas_call(
        _moe_down_kernel,
        grid_spec=pltpu.PrefetchScalarGridSpec(
            num_scalar_prefetch=1, grid=(nt,),
            in_specs=[pl.BlockSpec((tm, f), lambda i, te: (i, 0)),
                      pl.BlockSpec((1, f, d), lambda i, te: (te[i], 0, 0))],
            out_specs=pl.BlockSpec((tm, d), lambda i, te: (i, 0))),
        out_shape=jax.ShapeDtypeStruct((p, d), _BF16),
        compiler_params=_cp(("arbitrary",)), name="moe_down",
    )(tile_e, hid, w_down)


def _comb_kernel(x_ref, y0_ref, y1_ref, w_ref, mod_ref, o_ref, *, k_gate):
    y = y0_ref[0].astype(_F32) * w_ref[0, :, 0:1] + y1_ref[0].astype(_F32) * w_ref[0, :, 1:2]
    o_ref[0] = x_ref[0] + mod_ref[0, k_gate:k_gate + 1, :] * y


def _combine(xx, y0, y1, w01, mod, lc, k_gate, latent_only):
    bsz, lt, d = xx.shape
    tr = ROW_TILE
    skip = lc // tr if latent_only else 0
    nt = lt // tr - skip
    row = lambda b, i: (b, i + skip, 0)
    mrow = _mod_row(bsz, lc // tr)
    return pl.pallas_call(
        functools.partial(_comb_kernel, k_gate=k_gate),
        grid=(bsz, nt),
        in_specs=[pl.BlockSpec((1, tr, d), row), pl.BlockSpec((1, tr, d), row), pl.BlockSpec((1, tr, d), row),
                  pl.BlockSpec((1, tr, 2), row),
                  pl.BlockSpec((1, N_MOD, d), lambda b, i: mrow(b, i + skip))],
        out_specs=pl.BlockSpec((1, tr, d), lambda b, i: (b, i, 0)),
        out_shape=jax.ShapeDtypeStruct((bsz, nt * tr, d), _F32),
        compiler_params=_cp(("arbitrary", "arbitrary")), name="moe_combine",
    )(xx, y0, y1, w01, mod)


def _moe(f, logits_t, b_router, w_gate, w_up, w_down):
    t, d = f.shape
    ne = w_gate.shape[0]
    tm = MOE_TILE
    oi, ow, cnt = _route(logits_t, b_router)
    e01 = oi[0:2]
    r01 = oi[2:4]
    counts = cnt[:, 0].astype(jnp.int32)
    tiles = (counts + tm - 1) // tm
    tile_end = jnp.cumsum(tiles)
    off = (tile_end - tiles) * tm
    pos = off[e01] + r01
    nt = (t * TOP_K) // tm + ne
    p = nt * tm
    tok = jnp.broadcast_to(jnp.arange(t, dtype=jnp.int32)[None, :], (TOP_K, t))
    src = jnp.zeros((p,), jnp.int32).at[pos.reshape(-1)].set(tok.reshape(-1))
    n_valid = tile_end[-1]
    tile_e = jnp.searchsorted(tile_end, jnp.arange(nt, dtype=jnp.int32), side="right").astype(jnp.int32)
    tile_e = jnp.minimum(tile_e, tile_e[jnp.maximum(n_valid - 1, 0)])
    tile_e = jnp.concatenate([tile_e, n_valid[None].astype(jnp.int32)])
    xs = jnp.take(f, src, axis=0)
    out = _moe_experts(xs, tile_e, w_gate, w_up, w_down)
    y0 = jnp.take(out, pos[0], axis=0)
    y1 = jnp.take(out, pos[1], axis=0)
    return y0, y1, jnp.transpose(ow[0:2])


def kernel(x, c, ctx, c_ctx, norm1_w, norm2_w, w_ada, b_ada, w_in, conv_w, a_log, dt_bias, gdn_norm_w,
           qk_norm_w, na_rpb, w_out, w_router, b_router, w_gate, w_up, w_down):
    bsz, s_len, d = x.shape
    lc = ctx.shape[1]
    lt = lc + s_len
    t = bsz * lt
    depth = w_ada.shape[0]
    gw = d // 2
    naw = d - gw
    nh_a = gw // HEAD_DIM
    nh_b = naw // HEAD_DIM
    rows = s_len // GRID_W
    assert s_len % GRID_W == 0 and rows >= NA_WIN_R
    assert lc % ROW_TILE == 0 and s_len % ROW_TILE == 0
    hg_a = min(4, nh_a)
    hg_b = min(4, nh_b)
    nhg_a = nh_a // hg_a

    xx = jnp.concatenate([ctx, x], axis=1)
    mp = -(-(bsz + 1) // 8) * 8
    cc = jnp.zeros((mp, d), _F32).at[:bsz].set(c).at[bsz].set(c_ctx)
    mod_all = _ada(cc, w_ada, b_ada).reshape(depth, mp, N_MOD, d)
    w_router_t = jnp.transpose(w_router).astype(_F32)

    n_main = 4 * gw + 3 * naw
    for layer in range(depth):
        last = layer == depth - 1
        mod = mod_all[layer]
        wi = w_in[layer]
        w_main = jnp.concatenate([wi[:, :4 * gw], wi[:, 4 * gw + 4 * nh_a:]], axis=1).astype(_BF16)
        w_ab = jnp.zeros((d, 128), _BF16).at[:, :4 * nh_a].set(wi[:, 4 * gw:4 * gw + 4 * nh_a].astype(_BF16))

        a = _norm(xx, norm1_w[layer], mod, lc, 0, 1)
        a2 = a.reshape(t, d)
        proj = _matmul(a2, w_main, _BF16).reshape(bsz, lt, n_main)
        ab = _matmul(a2, w_ab, _F32)[:, :4 * nh_a].reshape(bsz, lt, 2, 2, nhg_a, hg_a)

        q, k, v = _gprep(proj, conv_w[layer], lc, nh_a)
        lane_pad = GATE_LANES - 2 * hg_a
        ab_r = jnp.transpose(ab, (3, 4, 0, 1, 2, 5)).reshape(2, nhg_a, bsz, lt, 2 * hg_a)
        ab_r = jnp.pad(ab_r, ((0, 0),) * 4 + ((0, lane_pad),))
        zpad = jnp.zeros((2, nhg_a, 1, hg_a), _F32)
        zlane = jnp.zeros((2, nhg_a, 1, lane_pad), _F32)
        alog_r = jnp.concatenate([zpad, a_log[layer].astype(_F32).reshape(2, nhg_a, 1, hg_a), zlane], axis=-1)
        dtb_r = jnp.concatenate([zpad, dt_bias[layer].astype(_F32).reshape(2, nhg_a, 1, hg_a), zlane], axis=-1)
        o_dir = _gdn(q, k, v, ab_r, alog_r, dtb_r, lc, hg_a)
        ya = _gdnout(o_dir, proj, 3, gdn_norm_w[layer])

        tab = _na_bias_table(na_rpb[layer], rows)
        nb = naw // (hg_b * HEAD_DIM)
        b0 = (4 * gw) // (hg_b * HEAD_DIM)
        yb = _na(proj, b0, b0 + nb, b0 + 2 * nb, tab, qk_norm_w[layer].astype(_F32), lc, nh_b, hg_b)

        xx = _outproj(ya, yb, w_out[layer].astype(_BF16), xx, mod, lc, 2)

        f, logits_t = _norm(xx, norm2_w[layer], mod, lc, 3, 4, w_router_t)
        y0, y1, w01 = _moe(f.reshape(t, d), logits_t, b_router,
                           w_gate[layer].astype(_BF16), w_up[layer].astype(_BF16), w_down[layer].astype(_BF16))
        xx = _combine(xx, y0.reshape(bsz, lt, d), y1.reshape(bsz, lt, d), w01.reshape(bsz, lt, 2),
                      mod, lc, 5, last)
    return xx
```

```python
import functools
import math

import numpy as np
import jax
import jax.numpy as jnp
from jax import lax
from jax.experimental import pallas as pl
from jax.experimental.pallas import tpu as pltpu

HEAD_DIM = 128
GRID_W = 64
GDN_CHUNK = 64
SHORT_CONV = 5
NA_WIN_R = 8
NA_WIN_C = 16
N_EXPERT_GROUPS = 4
TOP_K = 2
N_MOD = 6
NORM_EPS = 1e-6
MASK_VALUE = -1e30

ROW_TILE = 256
MOE_TILE = 256
GATE_LANES = 128
GDN_HEADS_PER_STEP = 16
NA_HEADS_PER_STEP = 8
VMEM_LIMIT = 56 * 1024 * 1024

_F32 = jnp.float32
_BF16 = jnp.bfloat16
_HI = lax.Precision.HIGHEST


def _cp(sem, vmem=VMEM_LIMIT, flags=None):
    return pltpu.CompilerParams(dimension_semantics=sem, vmem_limit_bytes=vmem, flags=flags)


def _pick(n, cands):
    for c in cands:
        if n % c == 0:
            return c
    raise ValueError(f"no tile for {n} in {cands}")


def _dot(a, b):
    return jnp.dot(a, b, preferred_element_type=_F32)


def _dot_nt(a, b, precision=None):
    return lax.dot_general(a, b, (((1,), (1,)), ((), ())), preferred_element_type=_F32, precision=precision)


def _dot_tn(a, b):
    return lax.dot_general(a, b, (((0,), (0,)), ((), ())), preferred_element_type=_F32)


def _sigmoid(x):
    return 1.0 / (1.0 + jnp.exp(-x))


def _silu(x):
    return x * _sigmoid(x)


def _ada_kernel(cc_ref, w_ref, b_ref, o_ref):
    a = _silu(cc_ref[...]).astype(_BF16)
    o_ref[0] = _dot(a, w_ref[0].astype(_BF16)) + b_ref[0]


def _ada(cc, w_ada, b_ada):
    nl, d, n = w_ada.shape
    mp = cc.shape[0]
    tn = _pick(n, (512, 256, 128))
    return pl.pallas_call(
        _ada_kernel,
        grid=(nl, n // tn),
        in_specs=[
            pl.BlockSpec((mp, d), lambda l, j: (0, 0)),
            pl.BlockSpec((1, d, tn), lambda l, j: (l, 0, j)),
            pl.BlockSpec((1, 1, tn), lambda l, j: (l, 0, j)),
        ],
        out_specs=pl.BlockSpec((1, mp, tn), lambda l, j: (l, 0, j)),
        out_shape=jax.ShapeDtypeStruct((nl, mp, n), _F32),
        compiler_params=_cp(("arbitrary", "arbitrary")),
        name="ada_mod",
    )(cc, w_ada, b_ada.reshape(nl, 1, n))


def _modnorm(x, nw, mod, k_shift, k_scale):
    ms = jnp.mean(x * x, axis=-1, keepdims=True)
    y = x * lax.rsqrt(ms + NORM_EPS) * nw
    return y * (1.0 + mod[k_scale:k_scale + 1, :]) + mod[k_shift:k_shift + 1, :]


def _norm_kernel(x_ref, nw_ref, mod_ref, o_ref, *, k_shift, k_scale):
    y = _modnorm(x_ref[0], nw_ref[...], mod_ref[0], k_shift, k_scale)
    o_ref[0] = y.astype(o_ref.dtype)


def _norm_router_kernel(x_ref, nw_ref, mod_ref, wr_ref, o_ref, lg_ref, *, k_shift, k_scale):
    y = _modnorm(x_ref[0], nw_ref[...], mod_ref[0], k_shift, k_scale)
    o_ref[0] = y.astype(o_ref.dtype)
    lg_ref[...] = _dot_nt(wr_ref[...], y, precision=_HI)


def _mod_row(bsz, n_ctx_tiles):
    return lambda b, i: (jnp.where(i < n_ctx_tiles, bsz, b), 0, 0)


def _norm(xx, nw, mod, lc, k_shift, k_scale, w_router_t=None):
    bsz, lt, d = xx.shape
    tr = ROW_TILE
    nt = lt // tr
    in_specs = [
        pl.BlockSpec((1, tr, d), lambda b, i: (b, i, 0)),
        pl.BlockSpec((1, d), lambda b, i: (0, 0)),
        pl.BlockSpec((1, N_MOD, d), _mod_row(bsz, lc // tr)),
    ]
    a_spec = pl.BlockSpec((1, tr, d), lambda b, i: (b, i, 0))
    a_shape = jax.ShapeDtypeStruct((bsz, lt, d), _BF16)
    if w_router_t is None:
        return pl.pallas_call(
            functools.partial(_norm_kernel, k_shift=k_shift, k_scale=k_scale),
            grid=(bsz, nt), in_specs=in_specs, out_specs=a_spec, out_shape=a_shape,
            compiler_params=_cp(("arbitrary", "arbitrary")), name="modnorm",
        )(xx, nw.reshape(1, d), mod)
    ne = w_router_t.shape[0]
    return pl.pallas_call(
        functools.partial(_norm_router_kernel, k_shift=k_shift, k_scale=k_scale),
        grid=(bsz, nt),
        in_specs=in_specs + [pl.BlockSpec((ne, d), lambda b, i: (0, 0))],
        out_specs=[a_spec, pl.BlockSpec((ne, tr), lambda b, i: (0, b * nt + i))],
        out_shape=[a_shape, jax.ShapeDtypeStruct((ne, bsz * lt), _F32)],
        compiler_params=_cp(("arbitrary", "arbitrary")), name="modnorm_router",
    )(xx, nw.reshape(1, d), mod, w_router_t)


def _mm_kernel(a_ref, w_ref, o_ref):
    o_ref[...] = _dot(a_ref[...], w_ref[...]).astype(o_ref.dtype)


def _matmul(a, w, out_dtype):
    m, k = a.shape
    n = w.shape[1]
    tm = _pick(m, (1024, 512, 256, 128))
    tn = _pick(n, (1024, 512, 256, 128))
    return pl.pallas_call(
        _mm_kernel,
        grid=(n // tn, m // tm),
        in_specs=[pl.BlockSpec((tm, k), lambda j, i: (i, 0)),
                  pl.BlockSpec((k, tn), lambda j, i: (0, j))],
        out_specs=pl.BlockSpec((tm, tn), lambda j, i: (i, j)),
        out_shape=jax.ShapeDtypeStruct((m, n), out_dtype),
        compiler_params=_cp(("arbitrary", "arbitrary")), name="matmul",
    )(a, w)


def _conv_silu(x, w, lc):
    lt = x.shape[0]
    row = lax.broadcasted_iota(jnp.int32, x.shape, 0)
    pad = SHORT_CONV // 2
    acc = x * w[pad:pad + 1, :]
    for j in range(SHORT_CONV):
        o = j - pad
        if o == 0:
            continue
        xs = pltpu.roll(x, (-o) % lt, 0)
        if o < 0:
            bad = (row < -o) | ((row >= lc) & (row < lc - o))
        else:
            bad = ((row >= lc - o) & (row < lc)) | (row >= lt - o)
        acc = acc + jnp.where(bad, 0.0, xs) * w[j:j + 1, :]
    return _silu(acc)


def _l2n(t):
    return t * lax.rsqrt(jnp.sum(t * t, axis=-1, keepdims=True) + NORM_EPS)


def _gprep_kernel(q_ref, k_ref, v_ref, wq_ref, wk_ref, wv_ref, qo_ref, ko_ref, vo_ref, *, lc):
    q = _conv_silu(q_ref[0].astype(_F32), wq_ref[...], lc)
    qo_ref[0] = (_l2n(q) * (HEAD_DIM ** -0.5)).astype(qo_ref.dtype)
    k = _conv_silu(k_ref[0].astype(_F32), wk_ref[...], lc)
    ko_ref[0] = _l2n(k).astype(ko_ref.dtype)
    v = _conv_silu(v_ref[0].astype(_F32), wv_ref[...], lc)
    vo_ref[0] = v.astype(vo_ref.dtype)


def _gprep(proj, conv_w, lc, nh):
    bsz, lt, _ = proj.shape
    hd = HEAD_DIM

    def xs(off):
        return pl.BlockSpec((1, lt, hd), lambda b, h: (b, 0, off + h))

    def ws(off):
        return pl.BlockSpec((SHORT_CONV, hd), lambda b, h: (0, off + h))

    o_spec = pl.BlockSpec((1, lt, hd), lambda b, h: (b, 0, h))
    o_shape = jax.ShapeDtypeStruct((bsz, lt, nh * hd), _BF16)
    return pl.pallas_call(
        functools.partial(_gprep_kernel, lc=lc),
        grid=(bsz, nh),
        in_specs=[xs(0), xs(nh), xs(2 * nh), ws(0), ws(nh), ws(2 * nh)],
        out_specs=[o_spec, o_spec, o_spec],
        out_shape=[o_shape, o_shape, o_shape],
        compiler_params=_cp(("arbitrary", "arbitrary")), name="gdn_prep",
    )(proj, proj, proj, conv_w, conv_w, conv_w)


def _gdn_masks():
    c = GDN_CHUNK
    i = np.arange(c)[:, None]
    j = np.arange(c)[None, :]
    tri = np.stack([(i >= j), (i <= j)]).astype(np.float32)
    eye = np.eye(c, dtype=np.float32)
    lv = [(i // 2 == j // 2)]
    s = 2
    while s < c:
        lv.append((i // (2 * s) == j // (2 * s)) & (i // s != j // s))
        s *= 2
    return tri, eye, np.stack(lv).astype(np.float32)


def _gdn_kernel(q_ref, k_ref, v_ref, ab_ref, al_ref, dtb_ref, tri_ref, eye_ref, lv_ref, o_ref, s_ref, *, hg):
    c = GDN_CHUNK
    hd = HEAD_DIM
    t = pl.program_id(3)

    @pl.when(t == 0)
    def _():
        s_ref[...] = jnp.zeros_like(s_ref)

    tri = tri_ref[0]
    eye = eye_ref[...]
    strict = tri - eye
    ab = ab_ref[0, 0, 0]
    beta = _sigmoid(ab)
    z = ab + dtb_ref[0, 0]
    softplus = jnp.maximum(z, 0.0) + jnp.log(1.0 + jnp.exp(-jnp.abs(z)))
    g = -jnp.exp(al_ref[0, 0]) * softplus
    gc = jnp.dot(tri, g, preferred_element_type=_F32, precision=_HI)
    gtot = jnp.sum(g, axis=0, keepdims=True)
    n_lv = lv_ref.shape[0]
    heads = range(hg)
    sls = [slice(j * hd, (j + 1) * hd) for j in heads]

    b = [beta[:, j:j + 1] for j in heads]
    gj = [gc[:, hg + j:hg + j + 1] for j in heads]
    gt = [gtot[:, hg + j:hg + j + 1] for j in heads]
    eg = [jnp.exp(gj[j]) for j in heads]
    k16 = [k_ref[0, :, sls[j]] for j in heads]
    q16 = [q_ref[0, :, sls[j]] for j in heads]
    kf = [k16[j].astype(_F32) for j in heads]
    decay = []
    for j in heads:
        gcol = jnp.broadcast_to(gj[j], (c, c))
        grow = jnp.sum(gcol * eye, axis=0, keepdims=True)
        decay.append(jnp.exp(jnp.where(tri > 0.0, gcol - grow, -jnp.inf)))
    kbeta = [kf[j] * b[j] for j in heads]
    lmat = [_dot_nt(kbeta[j].astype(_BF16), k16[j]) * decay[j] * strict for j in heads]
    qk = [(_dot_nt(q16[j], k16[j]) * decay[j]).astype(_BF16) for j in heads]
    dinv = [eye - lmat[j] * lv_ref[0] for j in heads]
    for lv in range(1, n_lv):
        wmat = [_dot((lmat[j] * lv_ref[lv]).astype(_BF16), dinv[j].astype(_BF16)) for j in heads]
        dinv = [dinv[j] - _dot(dinv[j].astype(_BF16), wmat[j].astype(_BF16)) for j in heads]
    uw = []
    for j in heads:
        vf = v_ref[0, :, sls[j]].astype(_F32)
        rhs = jnp.concatenate([vf * b[j], kbeta[j] * eg[j]], axis=-1).astype(_BF16)
        uw.append(_dot(dinv[j].astype(_BF16), rhs))
    s = [s_ref[j] for j in heads]
    s16 = [s[j].astype(_BF16) for j in heads]
    v16 = [(uw[j][:, :hd] - _dot(uw[j][:, hd:].astype(_BF16), s16[j])).astype(_BF16) for j in heads]
    q_dec = [(q16[j].astype(_F32) * eg[j]).astype(_BF16) for j in heads]
    k_dec = [(kf[j] * jnp.exp(gt[j] - gj[j])).astype(_BF16) for j in heads]
    outs = [_dot(q_dec[j], s16[j]) + _dot(qk[j], v16[j]) for j in heads]
    states = [s[j] * jnp.exp(gt[j]) + _dot_tn(k_dec[j], v16[j]) for j in heads]

    for j in heads:
        o_ref[0, 0, :, sls[j]] = outs[j].astype(o_ref.dtype)
    for j in heads:
        s_ref[j] = states[j]


def _gdn(q, k, v, ab_r, alog_r, dtb_r, lc, hg):
    bsz, lt, gw = q.shape
    nh = gw // HEAD_DIM
    nhg = nh // hg
    c = GDN_CHUNK
    nc_c = lc // c
    nc = lt // c
    tri, eye, lvm = _gdn_masks()

    def chunk(d, t):
        bwd = jnp.where(t < nc_c, nc_c - 1 - t, nc - 1 + nc_c - t)
        return jnp.where(d == 0, t, bwd)

    x_spec = pl.BlockSpec((1, c, hg * HEAD_DIM), lambda b, d, g, t: (b, chunk(d, t), g))
    return pl.pallas_call(
        functools.partial(_gdn_kernel, hg=hg),
        grid=(bsz, 2, nhg, nc),
        in_specs=[
            x_spec, x_spec, x_spec,
            pl.BlockSpec((1, 1, 1, c, GATE_LANES), lambda b, d, g, t: (d, g, b, chunk(d, t), 0)),
            pl.BlockSpec((1, 1, 1, GATE_LANES), lambda b, d, g, t: (d, g, 0, 0)),
            pl.BlockSpec((1, 1, 1, GATE_LANES), lambda b, d, g, t: (d, g, 0, 0)),
            pl.BlockSpec((1, c, c), lambda b, d, g, t: (d, 0, 0)),
            pl.BlockSpec((c, c), lambda b, d, g, t: (0, 0)),
            pl.BlockSpec(lvm.shape, lambda b, d, g, t: (0, 0, 0)),
        ],
        out_specs=pl.BlockSpec((1, 1, c, hg * HEAD_DIM), lambda b, d, g, t: (d, b, chunk(d, t), g)),
        out_shape=jax.ShapeDtypeStruct((2, bsz, lt, gw), _BF16),
        scratch_shapes=[pltpu.VMEM((hg, HEAD_DIM, HEAD_DIM), _F32)],
        compiler_params=_cp(("arbitrary", "arbitrary", "arbitrary", "arbitrary")), name="gdn_scan",
    )(q, k, v, ab_r, alog_r, dtb_r, jnp.asarray(tri), jnp.asarray(eye), jnp.asarray(lvm))


def _gdnout_kernel(of_ref, ob_ref, z_ref, nw_ref, y_ref, *, nh):
    hd = HEAD_DIM
    for h in range(nh):
        sl = slice(h * hd, (h + 1) * hd)
        o = of_ref[0, 0, :, sl].astype(_F32) + ob_ref[0, 0, :, sl].astype(_F32)
        ms = jnp.mean(o * o, axis=-1, keepdims=True)
        y = o * lax.rsqrt(ms + NORM_EPS) * nw_ref[...]
        y_ref[0, :, sl] = (y * _silu(z_ref[0, :, sl].astype(_F32))).astype(y_ref.dtype)


def _gdnout(o_dir, proj, z_off_blocks, norm_w):
    _, bsz, lt, gw = o_dir.shape
    nh = gw // HEAD_DIM
    tr = ROW_TILE
    return pl.pallas_call(
        functools.partial(_gdnout_kernel, nh=nh),
        grid=(bsz, lt // tr),
        in_specs=[
            pl.BlockSpec((1, 1, tr, gw), lambda b, i: (0, b, i, 0)),
            pl.BlockSpec((1, 1, tr, gw), lambda b, i: (1, b, i, 0)),
            pl.BlockSpec((1, tr, gw), lambda b, i: (b, i, z_off_blocks)),
            pl.BlockSpec((1, HEAD_DIM), lambda b, i: (0, 0)),
        ],
        out_specs=pl.BlockSpec((1, tr, gw), lambda b, i: (b, i, 0)),
        out_shape=jax.ShapeDtypeStruct((bsz, lt, gw), _BF16),
        compiler_params=_cp(("arbitrary", "arbitrary")), name="gdn_out",
    )(o_dir, o_dir, proj, norm_w.reshape(1, HEAD_DIM))


def _na_bias_table(rpb, rows):
    wc = NA_WIN_C
    qc = np.arange(GRID_W)[:, None]
    kc = np.arange(GRID_W)[None, :]
    win_start = np.clip(qc - wc // 2, 0, GRID_W - wc)
    valid = (kc >= win_start) & (kc < win_start + wc)
    off = np.clip(kc - qc + wc - 1, 0, 2 * wc - 2)
    ro = np.arange(NA_WIN_R)[:, None] + np.arange(NA_WIN_R)[None, :]
    tab = rpb.astype(_F32)[:, ro][:, :, :, off]
    tab = jnp.where(valid[None, None, None], tab, MASK_VALUE)
    tab = jnp.transpose(tab, (0, 1, 3, 2, 4))
    return tab.reshape(rpb.shape[0], NA_WIN_R, GRID_W, NA_WIN_R * GRID_W)


def _rms_head(t, w):
    return t * lax.rsqrt(jnp.mean(t * t, axis=-1, keepdims=True) + NORM_EPS) * w


def _na_kernel(q_ref, k_ref, v_ref, tab_ref, nw_ref, o_ref, kn_ref, *, hg, lc, rows):
    hd = HEAD_DIM
    w = GRID_W
    nloc = NA_WIN_R * w
    ncc = lc // w
    r = pl.program_id(2)

    @pl.when(r == 0)
    def _():
        for j in range(hg):
            sl = slice(j * hd, (j + 1) * hd)
            kn_ref[:, sl] = _rms_head(k_ref[0, :, sl].astype(_F32), nw_ref[1:2, :]).astype(kn_ref.dtype)

    rr = r - ncc
    sr = jnp.clip(rr - NA_WIN_R // 2, 0, rows - NA_WIN_R)
    start = pl.multiple_of(lc + sr * w, w)
    ro = jnp.clip(sr - rr + NA_WIN_R - 1, 0, NA_WIN_R - 1)

    heads = range(hg)
    sls = [slice(j * hd, (j + 1) * hd) for j in heads]

    def normed_q():
        return [(_rms_head(q_ref[0, :, sls[j]].astype(_F32), nw_ref[0:1, :]) * (hd ** -0.5)).astype(_BF16)
                for j in heads]

    def store(outs):
        for j in heads:
            o_ref[0, :, sls[j]] = outs[j].astype(o_ref.dtype)

    @pl.when(r < ncc)
    def _():
        qn = normed_q()
        s_ctx = [_dot_nt(qn[j], kn_ref[0:lc, sls[j]]) for j in heads]
        p = [jnp.exp(s_ctx[j] - jnp.max(s_ctx[j], axis=-1, keepdims=True)) for j in heads]
        den = [jnp.sum(p[j], axis=-1, keepdims=True) for j in heads]
        store([_dot(p[j].astype(_BF16), v_ref[0, 0:lc, sls[j]]) / den[j] for j in heads])

    @pl.when(r >= ncc)
    def _():
        qn = normed_q()
        s_ctx = [_dot_nt(qn[j], kn_ref[0:lc, sls[j]]) for j in heads]
        s_loc = [_dot_nt(qn[j], kn_ref[pl.ds(start, nloc), sls[j]]) + tab_ref[j, ro] for j in heads]
        m = [jnp.maximum(jnp.max(s_loc[j], axis=-1, keepdims=True), jnp.max(s_ctx[j], axis=-1, keepdims=True))
             for j in heads]
        p_loc = [jnp.exp(s_loc[j] - m[j]) for j in heads]
        p_ctx = [jnp.exp(s_ctx[j] - m[j]) for j in heads]
        den = [jnp.sum(p_loc[j], axis=-1, keepdims=True) + jnp.sum(p_ctx[j], axis=-1, keepdims=True)
               for j in heads]
        o = [_dot(p_loc[j].astype(_BF16), v_ref[0, pl.ds(start, nloc), sls[j]])
             + _dot(p_ctx[j].astype(_BF16), v_ref[0, 0:lc, sls[j]]) for j in heads]
        store([o[j] / den[j] for j in heads])


def _na(proj, q_off, k_off, v_off, tab, qk_norm_w, lc, nh, hg):
    bsz, lt, _ = proj.shape
    rows = (lt - lc) // GRID_W
    nhg = nh // hg
    bw = hg * HEAD_DIM
    return pl.pallas_call(
        functools.partial(_na_kernel, hg=hg, lc=lc, rows=rows),
        grid=(bsz, nhg, lt // GRID_W),
        in_specs=[
            pl.BlockSpec((1, GRID_W, bw), lambda b, g, r: (b, r, q_off + g)),
            pl.BlockSpec((1, lt, bw), lambda b, g, r: (b, 0, k_off + g)),
            pl.BlockSpec((1, lt, bw), lambda b, g, r: (b, 0, v_off + g)),
            pl.BlockSpec((hg,) + tab.shape[1:], lambda b, g, r: (g, 0, 0, 0)),
            pl.BlockSpec((2, HEAD_DIM), lambda b, g, r: (0, 0)),
        ],
        out_specs=pl.BlockSpec((1, GRID_W, bw), lambda b, g, r: (b, r, g)),
        out_shape=jax.ShapeDtypeStruct((bsz, lt, nh * HEAD_DIM), _BF16),
        scratch_shapes=[pltpu.VMEM((lt, bw), _BF16)],
        compiler_params=_cp(("arbitrary", "arbitrary", "arbitrary")), name="natten",
    )(proj, proj, proj, tab, qk_norm_w)


def _outproj_kernel(ya_ref, yb_ref, wa_ref, wb_ref, x_ref, ml_ref, mc_ref, o_ref, *, lc, tiles_per_sample, k_gate):
    tm = x_ref.shape[0]
    acc = _dot(ya_ref[...], wa_ref[...]) + _dot(yb_ref[...], wb_ref[...])
    row = (pl.program_id(0) % tiles_per_sample) * tm + lax.broadcasted_iota(jnp.int32, acc.shape, 0)
    gate = jnp.where(row < lc, mc_ref[0, k_gate:k_gate + 1, :], ml_ref[0, k_gate:k_gate + 1, :])
    o_ref[...] = x_ref[...] + gate * acc


def _outproj(ya, yb, w_out, xx, mod, lc, k_gate):
    bsz, lt, d = xx.shape
    t = bsz * lt
    ga = ya.shape[-1]
    gb = yb.shape[-1]
    tm = _pick(lt, (1152, 768, 576, 384, 256, 128))
    tn = _pick(d, (512, 256, 128))
    tps = lt // tm
    out = pl.pallas_call(
        functools.partial(_outproj_kernel, lc=lc, tiles_per_sample=tps, k_gate=k_gate),
        grid=(t // tm, d // tn),
        in_specs=[
            pl.BlockSpec((tm, ga), lambda i, j: (i, 0)),
            pl.BlockSpec((tm, gb), lambda i, j: (i, 0)),
            pl.BlockSpec((ga, tn), lambda i, j: (0, j)),
            pl.BlockSpec((gb, tn), lambda i, j: (ga // gb, j)),
            pl.BlockSpec((tm, tn), lambda i, j: (i, j)),
            pl.BlockSpec((1, N_MOD, tn), lambda i, j: (i // tps, 0, j)),
            pl.BlockSpec((1, N_MOD, tn), lambda i, j: (bsz, 0, j)),
        ],
        out_specs=pl.BlockSpec((tm, tn), lambda i, j: (i, j)),
        out_shape=jax.ShapeDtypeStruct((t, d), _F32),
        compiler_params=_cp(("arbitrary", "arbitrary")), name="out_proj",
    )(ya.reshape(t, ga), yb.reshape(t, gb), w_out, w_out, xx.reshape(t, d), mod, mod)
    return out.reshape(bsz, lt, d)


def _route_kernel(lg_ref, br_ref, su_ref, oi_ref, ow_ref, cnt_ref, carry_ref, *, ne):
    epg = ne // N_EXPERT_GROUPS
    tq = lg_ref.shape[1]

    @pl.when(pl.program_id(0) == 0)
    def _():
        carry_ref[...] = jnp.zeros_like(carry_ref)

    sc = _sigmoid(lg_ref[...])
    sel = sc + br_ref[...]
    s = [sel[e:e + 1, :] for e in range(ne)]

    def top2(vals):
        a, b, c_, d_ = vals
        hi1, lo1 = jnp.maximum(a, b), jnp.minimum(a, b)
        hi2, lo2 = jnp.maximum(c_, d_), jnp.minimum(c_, d_)
        return jnp.maximum(hi1, hi2) + jnp.maximum(jnp.maximum(lo1, lo2), jnp.minimum(hi1, hi2))

    gs = [top2(s[g * epg:(g + 1) * epg]) for g in range(N_EXPERT_GROUPS)]
    best = gs[0]
    gsel = jnp.zeros_like(best, dtype=jnp.int32)
    for g in range(1, N_EXPERT_GROUPS):
        better = gs[g] > best
        gsel = jnp.where(better, g, gsel)
        best = jnp.maximum(best, gs[g])
    ing = []
    for j in range(epg):
        v = s[j]
        for g in range(1, N_EXPERT_GROUPS):
            v = jnp.where(gsel == g, s[g * epg + j], v)
        ing.append(v)
    b1 = ing[0]
    i1 = jnp.zeros_like(gsel)
    for j in range(1, epg):
        better = ing[j] > b1
        i1 = jnp.where(better, j, i1)
        b1 = jnp.maximum(b1, ing[j])
    b2 = jnp.full_like(b1, -jnp.inf)
    i2 = jnp.zeros_like(gsel)
    for j in range(epg):
        better = (ing[j] > b2) & (i1 != j)
        i2 = jnp.where(better, j, i2)
        b2 = jnp.where(better, ing[j], b2)
    e1 = gsel * epg + i1
    e2 = gsel * epg + i2

    eio = lax.broadcasted_iota(jnp.int32, (ne, tq), 0)
    oh1 = eio == e1
    oh2 = eio == e2
    w1 = jnp.sum(jnp.where(oh1, sc, 0.0), axis=0, keepdims=True)
    w2 = jnp.sum(jnp.where(oh2, sc, 0.0), axis=0, keepdims=True)
    wsum = w1 + w2
    oh = jnp.where(oh1, 1.0, 0.0) + jnp.where(oh2, 1.0, 0.0)
    rank = _dot(oh.astype(_BF16), su_ref[...]) + carry_ref[:, 0:1]
    r1 = jnp.sum(jnp.where(oh1, rank, 0.0), axis=0, keepdims=True)
    r2 = jnp.sum(jnp.where(oh2, rank, 0.0), axis=0, keepdims=True)
    carry = carry_ref[...] + jnp.sum(oh, axis=1, keepdims=True)
    carry_ref[...] = carry
    cnt_ref[...] = carry

    sub = lax.broadcasted_iota(jnp.int32, (8, tq), 0)
    oi = jnp.where(sub == 0, e1, jnp.where(sub == 1, e2,
         jnp.where(sub == 2, r1.astype(jnp.int32), jnp.where(sub == 3, r2.astype(jnp.int32), 0))))
    oi_ref[...] = oi
    ow_ref[...] = jnp.where(sub == 0, w1 / wsum, jnp.where(sub == 1, w2 / wsum, 0.0))


def _route(logits_t, b_router):
    ne, t = logits_t.shape
    tq = ROW_TILE
    su = np.triu(np.ones((tq, tq), np.float32), 1)
    return pl.pallas_call(
        functools.partial(_route_kernel, ne=ne),
        grid=(t // tq,),
        in_specs=[pl.BlockSpec((ne, tq), lambda i: (0, i)),
                  pl.BlockSpec((ne, 1), lambda i: (0, 0)),
                  pl.BlockSpec((tq, tq), lambda i: (0, 0))],
        out_specs=[pl.BlockSpec((8, tq), lambda i: (0, i)),
                   pl.BlockSpec((8, tq), lambda i: (0, i)),
                   pl.BlockSpec((ne, 128), lambda i: (0, 0))],
        out_shape=[jax.ShapeDtypeStruct((8, t), jnp.int32),
                   jax.ShapeDtypeStruct((8, t), _F32),
                   jax.ShapeDtypeStruct((ne, 128), _F32)],
        scratch_shapes=[pltpu.VMEM((ne, 128), _F32)],
        compiler_params=_cp(("arbitrary",)), name="router",
    )(logits_t, b_router.reshape(ne, 1).astype(_F32), jnp.asarray(su, _BF16))


def _moe_up_kernel(te_ref, x_ref, wg_ref, wu_ref, h_ref):
    n_valid = te_ref[te_ref.shape[0] - 1]

    @pl.when(pl.program_id(0) < n_valid)
    def _():
        x = x_ref[...]
        gt = _dot(x, wg_ref[0])
        up = _dot(x, wu_ref[0])
        h_ref[...] = (_silu(gt) * up).astype(h_ref.dtype)

    @pl.when(pl.program_id(0) >= n_valid)
    def _():
        h_ref[...] = jnp.zeros_like(h_ref)


def _moe_down_kernel(te_ref, h_ref, wd_ref, o_ref):
    n_valid = te_ref[te_ref.shape[0] - 1]

    @pl.when(pl.program_id(0) < n_valid)
    def _():
        o_ref[...] = _dot(h_ref[...], wd_ref[0]).astype(o_ref.dtype)

    @pl.when(pl.program_id(0) >= n_valid)
    def _():
        o_ref[...] = jnp.zeros_like(o_ref)


def _moe_experts(xs, tile_e, w_gate, w_up, w_down):
    p, d = xs.shape
    ne, _, f = w_gate.shape
    tm = MOE_TILE
    nt = p // tm
    hid = pl.pallas_call(
        _moe_up_kernel,
        grid_spec=pltpu.PrefetchScalarGridSpec(
            num_scalar_prefetch=1, grid=(nt,),
            in_specs=[pl.BlockSpec((tm, d), lambda i, te: (i, 0)),
                      pl.BlockSpec((1, d, f), lambda i, te: (te[i], 0, 0)),
                      pl.BlockSpec((1, d, f), lambda i, te: (te[i], 0, 0))],
            out_specs=pl.BlockSpec((tm, f), lambda i, te: (i, 0))),
        out_shape=jax.ShapeDtypeStruct((p, f), _BF16),
        compiler_params=_cp(("arbitrary",)), name="moe_up",
    )(tile_e, xs, w_gate, w_up)
    return pl.pallas_call(
        _moe_down_kernel,
        grid_spec=pltpu.PrefetchScalarGridSpec(
            num_scalar_prefetch=1, grid=(nt,),
            in_specs=[pl.BlockSpec((tm, f), lambda i, te: (i, 0)),
                      pl.BlockSpec((1, f, d), lambda i, te: (te[i], 0, 0))],
            out_specs=pl.BlockSpec((tm, d), lambda i, te: (i, 0))),
        out_shape=jax.ShapeDtypeStruct((p, d), _BF16),
        compiler_params=_cp(("arbitrary",)), name="moe_down",
    )(tile_e, hid, w_down)


def _comb_kernel(x_ref, y0_ref, y1_ref, w_ref, mod_ref, o_ref, *, k_gate):
    y = y0_ref[0].astype(_F32) * w_ref[0, :, 0:1] + y1_ref[0].astype(_F32) * w_ref[0, :, 1:2]
    o_ref[0] = x_ref[0] + mod_ref[0, k_gate:k_gate + 1, :] * y


def _combine(xx, y0, y1, w01, mod, lc, k_gate, latent_only):
    bsz, lt, d = xx.shape
    tr = ROW_TILE
    skip = lc // tr if latent_only else 0
    nt = lt // tr - skip
    row = lambda b, i: (b, i + skip, 0)
    mrow = _mod_row(bsz, lc // tr)
    return pl.pallas_call(
        functools.partial(_comb_kernel, k_gate=k_gate),
        grid=(bsz, nt),
        in_specs=[pl.BlockSpec((1, tr, d), row), pl.BlockSpec((1, tr, d), row), pl.BlockSpec((1, tr, d), row),
                  pl.BlockSpec((1, tr, 2), row),
                  pl.BlockSpec((1, N_MOD, d), lambda b, i: mrow(b, i + skip))],
        out_specs=pl.BlockSpec((1, tr, d), lambda b, i: (b, i, 0)),
        out_shape=jax.ShapeDtypeStruct((bsz, nt * tr, d), _F32),
        compiler_params=_cp(("arbitrary", "arbitrary")), name="moe_combine",
    )(xx, y0, y1, w01, mod)


def _moe(f, logits_t, b_router, w_gate, w_up, w_down):
    t, d = f.shape
    ne = w_gate.shape[0]
    tm = MOE_TILE
    oi, ow, cnt = _route(logits_t, b_router)
    e01 = oi[0:2]
    r01 = oi[2:4]
    counts = cnt[:, 0].astype(jnp.int32)
    tiles = (counts + tm - 1) // tm
    tile_end = jnp.cumsum(tiles)
    off = (tile_end - tiles) * tm
    pos = off[e01] + r01
    nt = (t * TOP_K) // tm + ne
    p = nt * tm
    tok = jnp.broadcast_to(jnp.arange(t, dtype=jnp.int32)[None, :], (TOP_K, t))
    src = jnp.zeros((p,), jnp.int32).at[pos.reshape(-1)].set(tok.reshape(-1))
    n_valid = tile_end[-1]
    tile_e = jnp.searchsorted(tile_end, jnp.arange(nt, dtype=jnp.int32), side="right").astype(jnp.int32)
    tile_e = jnp.minimum(tile_e, tile_e[jnp.maximum(n_valid - 1, 0)])
    tile_e = jnp.concatenate([tile_e, n_valid[None].astype(jnp.int32)])
    xs = f.at[src].get(mode="promise_in_bounds")
    out = _moe_experts(xs, tile_e, w_gate, w_up, w_down)
    y0 = out.at[pos[0]].get(mode="promise_in_bounds")
    y1 = out.at[pos[1]].get(mode="promise_in_bounds")
    return y0, y1, jnp.transpose(ow[0:2])


def kernel(x, c, ctx, c_ctx, norm1_w, norm2_w, w_ada, b_ada, w_in, conv_w, a_log, dt_bias, gdn_norm_w,
           qk_norm_w, na_rpb, w_out, w_router, b_router, w_gate, w_up, w_down):
    bsz, s_len, d = x.shape
    lc = ctx.shape[1]
    lt = lc + s_len
    t = bsz * lt
    depth = w_ada.shape[0]
    gw = d // 2
    naw = d - gw
    nh_a = gw // HEAD_DIM
    nh_b = naw // HEAD_DIM
    rows = s_len // GRID_W
    assert s_len % GRID_W == 0 and rows >= NA_WIN_R
    assert lc % ROW_TILE == 0 and s_len % ROW_TILE == 0
    hg_a = min(GDN_HEADS_PER_STEP, nh_a)
    hg_b = min(NA_HEADS_PER_STEP, nh_b)
    nhg_a = nh_a // hg_a

    xx = jnp.concatenate([ctx, x], axis=1)
    mp = -(-(bsz + 1) // 8) * 8
    cc = jnp.zeros((mp, d), _F32).at[:bsz].set(c).at[bsz].set(c_ctx)
    mod_all = _ada(cc, w_ada, b_ada).reshape(depth, mp, N_MOD, d)
    w_router_t = jnp.transpose(w_router).astype(_F32)

    n_main = 4 * gw + 3 * naw
    for layer in range(depth):
        last = layer == depth - 1
        mod = mod_all[layer]
        wi = w_in[layer]
        w_main = jnp.concatenate([wi[:, :4 * gw], wi[:, 4 * gw + 4 * nh_a:]], axis=1).astype(_BF16)
        w_ab = jnp.zeros((d, 128), _BF16).at[:, :4 * nh_a].set(wi[:, 4 * gw:4 * gw + 4 * nh_a].astype(_BF16))

        a = _norm(xx, norm1_w[layer], mod, lc, 0, 1)
        a2 = a.reshape(t, d)
        proj = _matmul(a2, w_main, _BF16).reshape(bsz, lt, n_main)
        ab = _matmul(a2, w_ab, _F32)[:, :4 * nh_a].reshape(bsz, lt, 2, 2, nhg_a, hg_a)

        q, k, v = _gprep(proj, conv_w[layer], lc, nh_a)
        lane_pad = GATE_LANES - 2 * hg_a
        ab_r = jnp.transpose(ab, (3, 4, 0, 1, 2, 5)).reshape(2, nhg_a, bsz, lt, 2 * hg_a)
        ab_r = jnp.pad(ab_r, ((0, 0),) * 4 + ((0, lane_pad),))
        zpad = jnp.zeros((2, nhg_a, 1, hg_a), _F32)
        zlane = jnp.zeros((2, nhg_a, 1, lane_pad), _F32)
        alog_r = jnp.concatenate([zpad, a_log[layer].astype(_F32).reshape(2, nhg_a, 1, hg_a), zlane], axis=-1)
        dtb_r = jnp.concatenate([zpad, dt_bias[layer].astype(_F32).reshape(2, nhg_a, 1, hg_a), zlane], axis=-1)
        o_dir = _gdn(q, k, v, ab_r, alog_r, dtb_r, lc, hg_a)
        ya = _gdnout(o_dir, proj, 3, gdn_norm_w[layer])

        tab = _na_bias_table(na_rpb[layer], rows)
        nb = naw // (hg_b * HEAD_DIM)
        b0 = (4 * gw) // (hg_b * HEAD_DIM)
        yb = _na(proj, b0, b0 + nb, b0 + 2 * nb, tab, qk_norm_w[layer].astype(_F32), lc, nh_b, hg_b)

        xx = _outproj(ya, yb, w_out[layer].astype(_BF16), xx, mod, lc, 2)

        f, logits_t = _norm(xx, norm2_w[layer], mod, lc, 3, 4, w_router_t)
        y0, y1, w01 = _moe(f.reshape(t, d), logits_t, b_router,
                           w_gate[layer].astype(_BF16), w_up[layer].astype(_BF16), w_down[layer].astype(_BF16))
        xx = _combine(xx, y0.reshape(bsz, lt, d), y1.reshape(bsz, lt, d), w01.reshape(bsz, lt, 2),
                      mod, lc, 5, last)
    return xx
```

```python
import functools
import math

import numpy as np
import jax
import jax.numpy as jnp
from jax import lax
from jax.experimental import pallas as pl
from jax.experimental.pallas import tpu as pltpu

HEAD_DIM = 128
GRID_W = 64
GDN_CHUNK = 64
SHORT_CONV = 5
NA_WIN_R = 8
NA_WIN_C = 16
N_EXPERT_GROUPS = 4
TOP_K = 2
N_MOD = 6
NORM_EPS = 1e-6
MASK_VALUE = -1e30

ROW_TILE = 256
MOE_TILE = 256
MOE_F_TILE = 512
MOE_D_TILE = 2048
LANES = 128
GATE_LANES = 128
CONV_GAP = 8
GDN_HEADS_PER_STEP = 16
NA_HEADS_PER_STEP = 8
VMEM_LIMIT = 56 * 1024 * 1024

_F32 = jnp.float32
_BF16 = jnp.bfloat16
_HI = lax.Precision.HIGHEST


def _cp(sem, vmem=VMEM_LIMIT, flags=None):
    return pltpu.CompilerParams(dimension_semantics=sem, vmem_limit_bytes=vmem, flags=flags)


def _pick(n, cands):
    for c in cands:
        if n % c == 0:
            return c
    raise ValueError(f"no tile for {n} in {cands}")


def _dot(a, b):
    return jnp.dot(a, b, preferred_element_type=_F32)


def _dot_nt(a, b, precision=None):
    return lax.dot_general(a, b, (((1,), (1,)), ((), ())), preferred_element_type=_F32, precision=precision)


def _dot_tn(a, b):
    return lax.dot_general(a, b, (((0,), (0,)), ((), ())), preferred_element_type=_F32)


def _sigmoid(x):
    return 1.0 / (1.0 + jnp.exp(-x))


def _silu(x):
    return x * _sigmoid(x)


def _ada_kernel(cc_ref, w_ref, b_ref, o_ref):
    a = _silu(cc_ref[...]).astype(_BF16)
    o_ref[0] = _dot(a, w_ref[0].astype(_BF16)) + b_ref[0]


def _ada(cc, w_ada, b_ada):
    nl, d, n = w_ada.shape
    mp = cc.shape[0]
    tn = _pick(n, (512, 256, 128))
    return pl.pallas_call(
        _ada_kernel,
        grid=(nl, n // tn),
        in_specs=[
            pl.BlockSpec((mp, d), lambda l, j: (0, 0)),
            pl.BlockSpec((1, d, tn), lambda l, j: (l, 0, j)),
            pl.BlockSpec((1, 1, tn), lambda l, j: (l, 0, j)),
        ],
        out_specs=pl.BlockSpec((1, mp, tn), lambda l, j: (l, 0, j)),
        out_shape=jax.ShapeDtypeStruct((nl, mp, n), _F32),
        compiler_params=_cp(("arbitrary", "arbitrary")),
        name="ada_mod",
    )(cc, w_ada, b_ada.reshape(nl, 1, n))


def _modnorm(x, nw, mod, k_shift, k_scale):
    ms = jnp.mean(x * x, axis=-1, keepdims=True)
    y = x * lax.rsqrt(ms + NORM_EPS) * nw
    return y * (1.0 + mod[k_scale:k_scale + 1, :]) + mod[k_shift:k_shift + 1, :]


def _norm_kernel(x_ref, nw_ref, mod_ref, o_ref, *, k_shift, k_scale):
    y = _modnorm(x_ref[0], nw_ref[...], mod_ref[0], k_shift, k_scale)
    o_ref[0] = y.astype(o_ref.dtype)


def _norm_router_kernel(x_ref, nw_ref, mod_ref, wr_ref, o_ref, lg_ref, *, k_shift, k_scale):
    y = _modnorm(x_ref[0], nw_ref[...], mod_ref[0], k_shift, k_scale)
    o_ref[0] = y.astype(o_ref.dtype)
    lg_ref[...] = _dot_nt(wr_ref[...], y, precision=_HI)


def _mod_row(bsz, n_ctx_tiles):
    return lambda b, i: (jnp.where(i < n_ctx_tiles, bsz, b), 0, 0)


def _norm(xx, nw, mod, lc, k_shift, k_scale, w_router_t=None):
    bsz, lt, d = xx.shape
    tr = ROW_TILE
    nt = lt // tr
    in_specs = [
        pl.BlockSpec((1, tr, d), lambda b, i: (b, i, 0)),
        pl.BlockSpec((1, d), lambda b, i: (0, 0)),
        pl.BlockSpec((1, N_MOD, d), _mod_row(bsz, lc // tr)),
    ]
    a_spec = pl.BlockSpec((1, tr, d), lambda b, i: (b, i, 0))
    a_shape = jax.ShapeDtypeStruct((bsz, lt, d), _BF16)
    if w_router_t is None:
        return pl.pallas_call(
            functools.partial(_norm_kernel, k_shift=k_shift, k_scale=k_scale),
            grid=(bsz, nt), in_specs=in_specs, out_specs=a_spec, out_shape=a_shape,
            compiler_params=_cp(("arbitrary", "arbitrary")), name="modnorm",
        )(xx, nw.reshape(1, d), mod)
    ne = w_router_t.shape[0]
    return pl.pallas_call(
        functools.partial(_norm_router_kernel, k_shift=k_shift, k_scale=k_scale),
        grid=(bsz, nt),
        in_specs=in_specs + [pl.BlockSpec((ne, d), lambda b, i: (0, 0))],
        out_specs=[a_spec, pl.BlockSpec((ne, tr), lambda b, i: (0, b * nt + i))],
        out_shape=[a_shape, jax.ShapeDtypeStruct((ne, bsz * lt), _F32)],
        compiler_params=_cp(("arbitrary", "arbitrary")), name="modnorm_router",
    )(xx, nw.reshape(1, d), mod, w_router_t)


def _mm_kernel(a_ref, w_ref, o_ref):
    o_ref[...] = _dot(a_ref[...], w_ref[0]).astype(o_ref.dtype)


def _matmul(a, w, layer, out_dtype):
    m, k = a.shape
    n = w.shape[2]
    tm = _pick(m, (1024, 512, 256, 128))
    tn = _pick(n, (1024, 512, 256, 128))
    return pl.pallas_call(
        _mm_kernel,
        grid=(n // tn, m // tm),
        in_specs=[pl.BlockSpec((tm, k), lambda j, i: (i, 0)),
                  pl.BlockSpec((1, k, tn), lambda j, i: (layer, 0, j))],
        out_specs=pl.BlockSpec((tm, tn), lambda j, i: (i, j)),
        out_shape=jax.ShapeDtypeStruct((m, n), out_dtype),
        compiler_params=_cp(("arbitrary", "arbitrary")), name="matmul",
    )(a, w)


def _conv_silu(pad_ref, x, w, lc):
    lt = x.shape[0]
    g = CONV_GAP
    pad = SHORT_CONV // 2
    zeros = jnp.zeros((g, x.shape[1]), _F32)
    pad_ref[0:g] = zeros
    pad_ref[g:g + lc] = x[0:lc]
    pad_ref[g + lc:2 * g + lc] = zeros
    pad_ref[2 * g + lc:2 * g + lt] = x[lc:lt]
    pad_ref[2 * g + lt:3 * g + lt] = zeros
    acc = None
    for j in range(SHORT_CONV):
        term = pad_ref[g + j - pad:2 * g + lt + j - pad] * w[j:j + 1, :]
        acc = term if acc is None else acc + term
    y = _silu(acc)
    return jnp.concatenate([y[0:lc], y[lc + g:lt + g]], axis=0)


def _l2n(t):
    return t * lax.rsqrt(jnp.sum(t * t, axis=-1, keepdims=True) + NORM_EPS)


def _gprep_kernel(q_ref, k_ref, v_ref, wq_ref, wk_ref, wv_ref, qo_ref, ko_ref, vo_ref, pad_ref, *, lc):
    q = _conv_silu(pad_ref, q_ref[0].astype(_F32), wq_ref[...], lc)
    qo_ref[0] = (_l2n(q) * (HEAD_DIM ** -0.5)).astype(qo_ref.dtype)
    k = _conv_silu(pad_ref, k_ref[0].astype(_F32), wk_ref[...], lc)
    ko_ref[0] = _l2n(k).astype(ko_ref.dtype)
    v = _conv_silu(pad_ref, v_ref[0].astype(_F32), wv_ref[...], lc)
    vo_ref[0] = v.astype(vo_ref.dtype)


def _gprep(proj, conv_w, lc, nh):
    bsz, lt, _ = proj.shape
    hd = HEAD_DIM

    def xs(off):
        return pl.BlockSpec((1, lt, hd), lambda b, h: (b, 0, off + h))

    def ws(off):
        return pl.BlockSpec((SHORT_CONV, hd), lambda b, h: (0, off + h))

    o_spec = pl.BlockSpec((1, lt, hd), lambda b, h: (b, 0, h))
    o_shape = jax.ShapeDtypeStruct((bsz, lt, nh * hd), _BF16)
    return pl.pallas_call(
        functools.partial(_gprep_kernel, lc=lc),
        grid=(bsz, nh),
        in_specs=[xs(0), xs(nh), xs(2 * nh), ws(0), ws(nh), ws(2 * nh)],
        out_specs=[o_spec, o_spec, o_spec],
        out_shape=[o_shape, o_shape, o_shape],
        scratch_shapes=[pltpu.VMEM((lt + 3 * CONV_GAP, hd), _F32)],
        compiler_params=_cp(("arbitrary", "arbitrary")), name="gdn_prep",
    )(proj, proj, proj, conv_w, conv_w, conv_w)


def _gdn_masks():
    c = GDN_CHUNK
    i = np.arange(c)[:, None]
    j = np.arange(c)[None, :]
    tri = np.stack([(i >= j), (i <= j)]).astype(np.float32)
    eye = np.eye(c, dtype=np.float32)
    lv = [(i // 2 == j // 2)]
    s = 2
    while s < c:
        lv.append((i // (2 * s) == j // (2 * s)) & (i // s != j // s))
        s *= 2
    return tri, eye, np.stack(lv).astype(np.float32)


def _gdn_kernel(q_ref, k_ref, v_ref, ab_ref, al_ref, dtb_ref, tri_ref, eye_ref, lv_ref, o_ref, s_ref, *, hg):
    c = GDN_CHUNK
    hd = HEAD_DIM
    t = pl.program_id(3)

    @pl.when(t == 0)
    def _():
        s_ref[...] = jnp.zeros_like(s_ref)

    tri = tri_ref[0]
    eye = eye_ref[...]
    strict = tri - eye
    ab = ab_ref[0, 0, 0]
    beta = _sigmoid(ab)
    z = ab + dtb_ref[0, 0]
    softplus = jnp.maximum(z, 0.0) + jnp.log(1.0 + jnp.exp(-jnp.abs(z)))
    g = -jnp.exp(al_ref[0, 0]) * softplus
    gc = jnp.dot(tri, g, preferred_element_type=_F32, precision=_HI)
    gtot = jnp.sum(g, axis=0, keepdims=True)
    n_lv = lv_ref.shape[0]
    heads = range(hg)
    sls = [slice(j * hd, (j + 1) * hd) for j in heads]

    b = [beta[:, j:j + 1] for j in heads]
    gj = [gc[:, hg + j:hg + j + 1] for j in heads]
    gt = [gtot[:, hg + j:hg + j + 1] for j in heads]
    eg = [jnp.exp(gj[j]) for j in heads]
    k16 = [k_ref[0, :, sls[j]] for j in heads]
    q16 = [q_ref[0, :, sls[j]] for j in heads]
    kf = [k16[j].astype(_F32) for j in heads]
    decay = []
    for j in heads:
        gcol = jnp.broadcast_to(gj[j], (c, c))
        grow = jnp.sum(gcol * eye, axis=0, keepdims=True)
        decay.append(jnp.exp(jnp.where(tri > 0.0, gcol - grow, -jnp.inf)))
    kbeta = [kf[j] * b[j] for j in heads]
    kq = [_dot_nt(jnp.concatenate([kbeta[j].astype(_BF16), q16[j]], axis=0), k16[j]) for j in heads]
    lmat = [kq[j][:c] * decay[j] * strict for j in heads]
    qk = [(kq[j][c:] * decay[j]).astype(_BF16) for j in heads]
    dinv = [eye - lmat[j] * lv_ref[0] for j in heads]
    for lv in range(1, n_lv):
        wmat = [_dot((lmat[j] * lv_ref[lv]).astype(_BF16), dinv[j].astype(_BF16)) for j in heads]
        dinv = [dinv[j] - _dot(dinv[j].astype(_BF16), wmat[j].astype(_BF16)) for j in heads]
    uw = []
    for j in heads:
        vf = v_ref[0, :, sls[j]].astype(_F32)
        rhs = jnp.concatenate([vf * b[j], kbeta[j] * eg[j]], axis=-1).astype(_BF16)
        uw.append(_dot(dinv[j].astype(_BF16), rhs))
    s = [s_ref[j] for j in heads]
    s16 = [s[j].astype(_BF16) for j in heads]
    q_dec = [(q16[j].astype(_F32) * eg[j]).astype(_BF16) for j in heads]
    k_dec = [(kf[j] * jnp.exp(gt[j] - gj[j])).astype(_BF16) for j in heads]
    ws = [_dot(jnp.concatenate([uw[j][:, hd:].astype(_BF16), q_dec[j]], axis=0), s16[j]) for j in heads]
    v16 = [(uw[j][:, :hd] - ws[j][:c]).astype(_BF16) for j in heads]
    outs = [ws[j][c:] + _dot(qk[j], v16[j]) for j in heads]
    states = [s[j] * jnp.exp(gt[j]) + _dot_tn(k_dec[j], v16[j]) for j in heads]

    for j in heads:
        o_ref[0, 0, :, sls[j]] = outs[j].astype(o_ref.dtype)
    for j in heads:
        s_ref[j] = states[j]


def _gdn(q, k, v, ab_r, alog_r, dtb_r, lc, hg):
    bsz, lt, gw = q.shape
    nh = gw // HEAD_DIM
    nhg = nh // hg
    c = GDN_CHUNK
    nc_c = lc // c
    nc = lt // c
    tri, eye, lvm = _gdn_masks()

    def chunk(d, t):
        bwd = jnp.where(t < nc_c, nc_c - 1 - t, nc - 1 + nc_c - t)
        return jnp.where(d == 0, t, bwd)

    x_spec = pl.BlockSpec((1, c, hg * HEAD_DIM), lambda b, d, g, t: (b, chunk(d, t), g))
    return pl.pallas_call(
        functools.partial(_gdn_kernel, hg=hg),
        grid=(bsz, 2, nhg, nc),
        in_specs=[
            x_spec, x_spec, x_spec,
            pl.BlockSpec((1, 1, 1, c, GATE_LANES), lambda b, d, g, t: (d, g, b, chunk(d, t), 0)),
            pl.BlockSpec((1, 1, 1, GATE_LANES), lambda b, d, g, t: (d, g, 0, 0)),
            pl.BlockSpec((1, 1, 1, GATE_LANES), lambda b, d, g, t: (d, g, 0, 0)),
            pl.BlockSpec((1, c, c), lambda b, d, g, t: (d, 0, 0)),
            pl.BlockSpec((c, c), lambda b, d, g, t: (0, 0)),
            pl.BlockSpec(lvm.shape, lambda b, d, g, t: (0, 0, 0)),
        ],
        out_specs=pl.BlockSpec((1, 1, c, hg * HEAD_DIM), lambda b, d, g, t: (d, b, chunk(d, t), g)),
        out_shape=jax.ShapeDtypeStruct((2, bsz, lt, gw), _BF16),
        scratch_shapes=[pltpu.VMEM((hg, HEAD_DIM, HEAD_DIM), _F32)],
        compiler_params=_cp(("arbitrary", "arbitrary", "arbitrary", "arbitrary")), name="gdn_scan",
    )(q, k, v, ab_r, alog_r, dtb_r, jnp.asarray(tri), jnp.asarray(eye), jnp.asarray(lvm))


def _gdnout_kernel(of_ref, ob_ref, z_ref, nw_ref, y_ref, *, nh):
    hd = HEAD_DIM
    for h in range(nh):
        sl = slice(h * hd, (h + 1) * hd)
        o = of_ref[0, 0, :, sl].astype(_F32) + ob_ref[0, 0, :, sl].astype(_F32)
        ms = jnp.mean(o * o, axis=-1, keepdims=True)
        y = o * lax.rsqrt(ms + NORM_EPS) * nw_ref[...]
        y_ref[0, :, sl] = (y * _silu(z_ref[0, :, sl].astype(_F32))).astype(y_ref.dtype)


def _gdnout(o_dir, proj, z_off_blocks, norm_w):
    _, bsz, lt, gw = o_dir.shape
    nh = gw // HEAD_DIM
    tr = ROW_TILE
    return pl.pallas_call(
        functools.partial(_gdnout_kernel, nh=nh),
        grid=(bsz, lt // tr),
        in_specs=[
            pl.BlockSpec((1, 1, tr, gw), lambda b, i: (0, b, i, 0)),
            pl.BlockSpec((1, 1, tr, gw), lambda b, i: (1, b, i, 0)),
            pl.BlockSpec((1, tr, gw), lambda b, i: (b, i, z_off_blocks)),
            pl.BlockSpec((1, HEAD_DIM), lambda b, i: (0, 0)),
        ],
        out_specs=pl.BlockSpec((1, tr, gw), lambda b, i: (b, i, 0)),
        out_shape=jax.ShapeDtypeStruct((bsz, lt, gw), _BF16),
        compiler_params=_cp(("arbitrary", "arbitrary")), name="gdn_out",
    )(o_dir, o_dir, proj, norm_w.reshape(1, HEAD_DIM))


def _na_bias_table(rpb, rows):
    wc = NA_WIN_C
    qc = np.arange(GRID_W)[:, None]
    kc = np.arange(GRID_W)[None, :]
    win_start = np.clip(qc - wc // 2, 0, GRID_W - wc)
    valid = (kc >= win_start) & (kc < win_start + wc)
    off = np.clip(kc - qc + wc - 1, 0, 2 * wc - 2)
    ro = np.arange(NA_WIN_R)[:, None] + np.arange(NA_WIN_R)[None, :]
    tab = rpb.astype(_F32)[:, ro][:, :, :, off]
    tab = jnp.where(valid[None, None, None], tab, MASK_VALUE)
    tab = jnp.transpose(tab, (0, 1, 3, 2, 4))
    return tab.reshape(rpb.shape[0], NA_WIN_R, GRID_W, NA_WIN_R * GRID_W)


def _rms_head(t, w):
    return t * lax.rsqrt(jnp.mean(t * t, axis=-1, keepdims=True) + NORM_EPS) * w


def _na_kernel(q_ref, k_ref, v_ref, tab_ref, nw_ref, o_ref, kn_ref, *, hg, lc, rows):
    hd = HEAD_DIM
    w = GRID_W
    nloc = NA_WIN_R * w
    ncc = lc // w
    r = pl.program_id(2)

    @pl.when(r == 0)
    def _():
        for j in range(hg):
            sl = slice(j * hd, (j + 1) * hd)
            kn_ref[:, sl] = _rms_head(k_ref[0, :, sl].astype(_F32), nw_ref[1:2, :]).astype(kn_ref.dtype)

    rr = r - ncc
    sr = jnp.clip(rr - NA_WIN_R // 2, 0, rows - NA_WIN_R)
    start = pl.multiple_of(lc + sr * w, w)
    ro = jnp.clip(sr - rr + NA_WIN_R - 1, 0, NA_WIN_R - 1)

    heads = range(hg)
    sls = [slice(j * hd, (j + 1) * hd) for j in heads]

    def normed_q():
        return [(_rms_head(q_ref[0, :, sls[j]].astype(_F32), nw_ref[0:1, :]) * (hd ** -0.5)).astype(_BF16)
                for j in heads]

    def store(outs):
        for j in heads:
            o_ref[0, :, sls[j]] = outs[j].astype(o_ref.dtype)

    def lane_tile_reduce(op, *arrays):
        tiles = [a[:, i:i + LANES] for a in arrays for i in range(0, a.shape[1], LANES)]
        acc = tiles[0]
        for tl in tiles[1:]:
            acc = op(acc, tl)
        return acc

    @pl.when(r < ncc)
    def _():
        qn = normed_q()
        s_ctx = [_dot_nt(qn[j], kn_ref[0:lc, sls[j]]) for j in heads]
        p = [jnp.exp(s_ctx[j] - jnp.max(s_ctx[j], axis=-1, keepdims=True)) for j in heads]
        den = [jnp.sum(p[j], axis=-1, keepdims=True) for j in heads]
        store([_dot(p[j].astype(_BF16), v_ref[0, 0:lc, sls[j]]) / den[j] for j in heads])

    @pl.when(r >= ncc)
    def _():
        qn = normed_q()
        s_ctx = [_dot_nt(qn[j], kn_ref[0:lc, sls[j]]) for j in heads]
        s_loc = [_dot_nt(qn[j], kn_ref[pl.ds(start, nloc), sls[j]]) + tab_ref[j, ro] for j in heads]
        m = [jnp.max(lane_tile_reduce(jnp.maximum, s_loc[j], s_ctx[j]), axis=-1, keepdims=True) for j in heads]
        p_loc = [jnp.exp(s_loc[j] - m[j]) for j in heads]
        p_ctx = [jnp.exp(s_ctx[j] - m[j]) for j in heads]
        den = [jnp.sum(lane_tile_reduce(jnp.add, p_loc[j], p_ctx[j]), axis=-1, keepdims=True) for j in heads]
        o = [_dot(p_loc[j].astype(_BF16), v_ref[0, pl.ds(start, nloc), sls[j]])
             + _dot(p_ctx[j].astype(_BF16), v_ref[0, 0:lc, sls[j]]) for j in heads]
        store([o[j] / den[j] for j in heads])


def _na(proj, q_off, k_off, v_off, tab, qk_norm_w, lc, nh, hg):
    bsz, lt, _ = proj.shape
    rows = (lt - lc) // GRID_W
    nhg = nh // hg
    bw = hg * HEAD_DIM
    return pl.pallas_call(
        functools.partial(_na_kernel, hg=hg, lc=lc, rows=rows),
        grid=(bsz, nhg, lt // GRID_W),
        in_specs=[
            pl.BlockSpec((1, GRID_W, bw), lambda b, g, r: (b, r, q_off + g)),
            pl.BlockSpec((1, lt, bw), lambda b, g, r: (b, 0, k_off + g)),
            pl.BlockSpec((1, lt, bw), lambda b, g, r: (b, 0, v_off + g)),
            pl.BlockSpec((hg,) + tab.shape[1:], lambda b, g, r: (g, 0, 0, 0)),
            pl.BlockSpec((2, HEAD_DIM), lambda b, g, r: (0, 0)),
        ],
        out_specs=pl.BlockSpec((1, GRID_W, bw), lambda b, g, r: (b, r, g)),
        out_shape=jax.ShapeDtypeStruct((bsz, lt, nh * HEAD_DIM), _BF16),
        scratch_shapes=[pltpu.VMEM((lt, bw), _BF16)],
        compiler_params=_cp(("arbitrary", "arbitrary", "arbitrary")), name="natten",
    )(proj, proj, proj, tab, qk_norm_w)


def _outproj_kernel(ya_ref, yb_ref, wa_ref, wb_ref, x_ref, ml_ref, mc_ref, o_ref, *, lc, tiles_per_sample, k_gate):
    tm = x_ref.shape[0]
    acc = _dot(ya_ref[...], wa_ref[0]) + _dot(yb_ref[...], wb_ref[0])
    row = (pl.program_id(0) % tiles_per_sample) * tm + lax.broadcasted_iota(jnp.int32, acc.shape, 0)
    gate = jnp.where(row < lc, mc_ref[0, k_gate:k_gate + 1, :], ml_ref[0, k_gate:k_gate + 1, :])
    o_ref[...] = x_ref[...] + gate * acc


def _outproj(ya, yb, w_out, layer, xx, mod, lc, k_gate):
    bsz, lt, d = xx.shape
    t = bsz * lt
    ga = ya.shape[-1]
    gb = yb.shape[-1]
    tm = _pick(lt, (1152, 768, 576, 384, 256, 128))
    tn = _pick(d, (512, 256, 128))
    tps = lt // tm
    out = pl.pallas_call(
        functools.partial(_outproj_kernel, lc=lc, tiles_per_sample=tps, k_gate=k_gate),
        grid=(t // tm, d // tn),
        in_specs=[
            pl.BlockSpec((tm, ga), lambda i, j: (i, 0)),
            pl.BlockSpec((tm, gb), lambda i, j: (i, 0)),
            pl.BlockSpec((1, ga, tn), lambda i, j: (layer, 0, j)),
            pl.BlockSpec((1, gb, tn), lambda i, j: (layer, ga // gb, j)),
            pl.BlockSpec((tm, tn), lambda i, j: (i, j)),
            pl.BlockSpec((1, N_MOD, tn), lambda i, j: (i // tps, 0, j)),
            pl.BlockSpec((1, N_MOD, tn), lambda i, j: (bsz, 0, j)),
        ],
        out_specs=pl.BlockSpec((tm, tn), lambda i, j: (i, j)),
        out_shape=jax.ShapeDtypeStruct((t, d), _F32),
        compiler_params=_cp(("arbitrary", "arbitrary")), name="out_proj",
    )(ya.reshape(t, ga), yb.reshape(t, gb), w_out, w_out, xx.reshape(t, d), mod, mod)
    return out.reshape(bsz, lt, d)


def _route_kernel(lg_ref, br_ref, su_ref, oi_ref, ow_ref, cnt_ref, carry_ref, *, ne):
    epg = ne // N_EXPERT_GROUPS
    tq = lg_ref.shape[1]

    @pl.when(pl.program_id(0) == 0)
    def _():
        carry_ref[...] = jnp.zeros_like(carry_ref)

    sc = _sigmoid(lg_ref[...])
    sel = sc + br_ref[...]
    s = [sel[e:e + 1, :] for e in range(ne)]

    def top2(vals):
        a, b, c_, d_ = vals
        hi1, lo1 = jnp.maximum(a, b), jnp.minimum(a, b)
        hi2, lo2 = jnp.maximum(c_, d_), jnp.minimum(c_, d_)
        return jnp.maximum(hi1, hi2) + jnp.maximum(jnp.maximum(lo1, lo2), jnp.minimum(hi1, hi2))

    gs = [top2(s[g * epg:(g + 1) * epg]) for g in range(N_EXPERT_GROUPS)]
    best = gs[0]
    gsel = jnp.zeros_like(best, dtype=jnp.int32)
    for g in range(1, N_EXPERT_GROUPS):
        better = gs[g] > best
        gsel = jnp.where(better, g, gsel)
        best = jnp.maximum(best, gs[g])
    ing = []
    for j in range(epg):
        v = s[j]
        for g in range(1, N_EXPERT_GROUPS):
            v = jnp.where(gsel == g, s[g * epg + j], v)
        ing.append(v)
    b1 = ing[0]
    i1 = jnp.zeros_like(gsel)
    for j in range(1, epg):
        better = ing[j] > b1
        i1 = jnp.where(better, j, i1)
        b1 = jnp.maximum(b1, ing[j])
    b2 = jnp.full_like(b1, -jnp.inf)
    i2 = jnp.zeros_like(gsel)
    for j in range(epg):
        better = (ing[j] > b2) & (i1 != j)
        i2 = jnp.where(better, j, i2)
        b2 = jnp.where(better, ing[j], b2)
    e1 = gsel * epg + i1
    e2 = gsel * epg + i2

    eio = lax.broadcasted_iota(jnp.int32, (ne, tq), 0)
    oh1 = eio == e1
    oh2 = eio == e2
    w1 = jnp.sum(jnp.where(oh1, sc, 0.0), axis=0, keepdims=True)
    w2 = jnp.sum(jnp.where(oh2, sc, 0.0), axis=0, keepdims=True)
    wsum = w1 + w2
    oh = jnp.where(oh1, 1.0, 0.0) + jnp.where(oh2, 1.0, 0.0)
    rank = _dot(oh.astype(_BF16), su_ref[...]) + carry_ref[:, 0:1]
    r1 = jnp.sum(jnp.where(oh1, rank, 0.0), axis=0, keepdims=True)
    r2 = jnp.sum(jnp.where(oh2, rank, 0.0), axis=0, keepdims=True)
    carry = carry_ref[...] + jnp.sum(oh, axis=1, keepdims=True)
    carry_ref[...] = carry
    cnt_ref[...] = carry

    sub = lax.broadcasted_iota(jnp.int32, (8, tq), 0)
    oi = jnp.where(sub == 0, e1, jnp.where(sub == 1, e2,
         jnp.where(sub == 2, r1.astype(jnp.int32), jnp.where(sub == 3, r2.astype(jnp.int32), 0))))
    oi_ref[...] = oi
    ow_ref[...] = jnp.where(sub == 0, w1 / wsum, jnp.where(sub == 1, w2 / wsum, 0.0))


def _route(logits_t, b_router):
    ne, t = logits_t.shape
    tq = ROW_TILE
    su = np.triu(np.ones((tq, tq), np.float32), 1)
    return pl.pallas_call(
        functools.partial(_route_kernel, ne=ne),
        grid=(t // tq,),
        in_specs=[pl.BlockSpec((ne, tq), lambda i: (0, i)),
                  pl.BlockSpec((ne, 1), lambda i: (0, 0)),
                  pl.BlockSpec((tq, tq), lambda i: (0, 0))],
        out_specs=[pl.BlockSpec((8, tq), lambda i: (0, i)),
                   pl.BlockSpec((8, tq), lambda i: (0, i)),
                   pl.BlockSpec((ne, 128), lambda i: (0, 0))],
        out_shape=[jax.ShapeDtypeStruct((8, t), jnp.int32),
                   jax.ShapeDtypeStruct((8, t), _F32),
                   jax.ShapeDtypeStruct((ne, 128), _F32)],
        scratch_shapes=[pltpu.VMEM((ne, 128), _F32)],
        compiler_params=_cp(("arbitrary",)), name="router",
    )(logits_t, b_router.reshape(ne, 1).astype(_F32), jnp.asarray(su, _BF16))


def _new_weight_block(te_ref, i):
    return (i == 0) | (te_ref[i] != te_ref[jnp.maximum(i - 1, 0)])


def _moe_up_kernel(te_ref, x_ref, wg_ref, wu_ref, h_ref, wg16_ref, wu16_ref):
    i = pl.program_id(1)
    n_valid = te_ref[te_ref.shape[0] - 1]

    @pl.when(_new_weight_block(te_ref, i))
    def _():
        wg16_ref[...] = wg_ref[0, 0].astype(_BF16)
        wu16_ref[...] = wu_ref[0, 0].astype(_BF16)

    @pl.when(i < n_valid)
    def _():
        x = x_ref[...]
        gt = _dot(x, wg16_ref[...])
        up = _dot(x, wu16_ref[...])
        h_ref[...] = (_silu(gt) * up).astype(h_ref.dtype)

    @pl.when(i >= n_valid)
    def _():
        h_ref[...] = jnp.zeros_like(h_ref)


def _moe_down_kernel(te_ref, h_ref, wd_ref, o_ref, wd16_ref):
    i = pl.program_id(1)
    n_valid = te_ref[te_ref.shape[0] - 1]

    @pl.when(_new_weight_block(te_ref, i))
    def _():
        wd16_ref[...] = wd_ref[0, 0].astype(_BF16)

    @pl.when(i < n_valid)
    def _():
        o_ref[...] = _dot(h_ref[...], wd16_ref[...]).astype(o_ref.dtype)

    @pl.when(i >= n_valid)
    def _():
        o_ref[...] = jnp.zeros_like(o_ref)


def _moe_experts(xs, tile_e, w_gate, w_up, w_down, layer):
    p, d = xs.shape
    f = w_gate.shape[-1]
    tm = MOE_TILE
    nt = p // tm
    tf = _pick(f, (MOE_F_TILE, 256, 128))
    td = _pick(d, (MOE_D_TILE, 1024, 512, 256, 128))
    hid = pl.pallas_call(
        _moe_up_kernel,
        grid_spec=pltpu.PrefetchScalarGridSpec(
            num_scalar_prefetch=1, grid=(f // tf, nt),
            in_specs=[pl.BlockSpec((tm, d), lambda j, i, te: (i, 0)),
                      pl.BlockSpec((1, 1, d, tf), lambda j, i, te: (layer, te[i], 0, j)),
                      pl.BlockSpec((1, 1, d, tf), lambda j, i, te: (layer, te[i], 0, j))],
            out_specs=pl.BlockSpec((tm, tf), lambda j, i, te: (i, j)),
            scratch_shapes=[pltpu.VMEM((d, tf), _BF16), pltpu.VMEM((d, tf), _BF16)]),
        out_shape=jax.ShapeDtypeStruct((p, f), _BF16),
        compiler_params=_cp(("arbitrary", "arbitrary")), name="moe_up",
    )(tile_e, xs, w_gate, w_up)
    return pl.pallas_call(
        _moe_down_kernel,
        grid_spec=pltpu.PrefetchScalarGridSpec(
            num_scalar_prefetch=1, grid=(d // td, nt),
            in_specs=[pl.BlockSpec((tm, f), lambda j, i, te: (i, 0)),
                      pl.BlockSpec((1, 1, f, td), lambda j, i, te: (layer, te[i], 0, j))],
            out_specs=pl.BlockSpec((tm, td), lambda j, i, te: (i, j)),
            scratch_shapes=[pltpu.VMEM((f, td), _BF16)]),
        out_shape=jax.ShapeDtypeStruct((p, d), _BF16),
        compiler_params=_cp(("arbitrary", "arbitrary")), name="moe_down",
    )(tile_e, hid, w_down)


def _comb_kernel(x_ref, y0_ref, y1_ref, w_ref, mod_ref, o_ref, *, k_gate):
    y = y0_ref[0].astype(_F32) * w_ref[0, :, 0:1] + y1_ref[0].astype(_F32) * w_ref[0, :, 1:2]
    o_ref[0] = x_ref[0] + mod_ref[0, k_gate:k_gate + 1, :] * y


def _combine(xx, y0, y1, w01, mod, lc, k_gate, latent_only):
    bsz, lt, d = xx.shape
    tr = ROW_TILE
    skip = lc // tr if latent_only else 0
    nt = lt // tr - skip
    row = lambda b, i: (b, i + skip, 0)
    mrow = _mod_row(bsz, lc // tr)
    return pl.pallas_call(
        functools.partial(_comb_kernel, k_gate=k_gate),
        grid=(bsz, nt),
        in_specs=[pl.BlockSpec((1, tr, d), row), pl.BlockSpec((1, tr, d), row), pl.BlockSpec((1, tr, d), row),
                  pl.BlockSpec((1, tr, 2), row),
                  pl.BlockSpec((1, N_MOD, d), lambda b, i: mrow(b, i + skip))],
        out_specs=pl.BlockSpec((1, tr, d), lambda b, i: (b, i, 0)),
        out_shape=jax.ShapeDtypeStruct((bsz, nt * tr, d), _F32),
        compiler_params=_cp(("arbitrary", "arbitrary")), name="moe_combine",
    )(xx, y0, y1, w01, mod)


def _moe(f, logits_t, b_router, w_gate, w_up, w_down, layer):
    t, d = f.shape
    ne = w_gate.shape[1]
    tm = MOE_TILE
    oi, ow, cnt = _route(logits_t, b_router)
    e01 = oi[0:2]
    r01 = oi[2:4]
    counts = cnt[:, 0].astype(jnp.int32)
    tiles = (counts + tm - 1) // tm
    tile_end = jnp.cumsum(tiles)
    off = (tile_end - tiles) * tm
    first = jnp.sum(jnp.where(e01[None] == jnp.arange(ne, dtype=jnp.int32)[:, None, None],
                              off[:, None, None], 0), axis=0)
    pos = first + r01
    nt = (t * TOP_K) // tm + ne
    p = nt * tm
    tok = jnp.broadcast_to(jnp.arange(t, dtype=jnp.int32)[None, :], (TOP_K, t))
    src = jnp.zeros((p,), jnp.int32).at[pos.reshape(-1)].set(tok.reshape(-1))
    n_valid = tile_end[-1]
    tile_e = jnp.searchsorted(tile_end, jnp.arange(nt, dtype=jnp.int32), side="right").astype(jnp.int32)
    tile_e = jnp.minimum(tile_e, tile_e[jnp.maximum(n_valid - 1, 0)])
    tile_e = jnp.concatenate([tile_e, n_valid[None].astype(jnp.int32)])
    xs = f.at[src].get(mode="promise_in_bounds")
    out = _moe_experts(xs, tile_e, w_gate, w_up, w_down, layer)
    y0 = out.at[pos[0]].get(mode="promise_in_bounds")
    y1 = out.at[pos[1]].get(mode="promise_in_bounds")
    return y0, y1, jnp.transpose(ow[0:2])


def kernel(x, c, ctx, c_ctx, norm1_w, norm2_w, w_ada, b_ada, w_in, conv_w, a_log, dt_bias, gdn_norm_w,
           qk_norm_w, na_rpb, w_out, w_router, b_router, w_gate, w_up, w_down):
    bsz, s_len, d = x.shape
    lc = ctx.shape[1]
    lt = lc + s_len
    t = bsz * lt
    depth = w_ada.shape[0]
    gw = d // 2
    naw = d - gw
    nh_a = gw // HEAD_DIM
    nh_b = naw // HEAD_DIM
    rows = s_len // GRID_W
    assert s_len % GRID_W == 0 and rows >= NA_WIN_R
    assert lc % ROW_TILE == 0 and s_len % ROW_TILE == 0
    hg_a = min(GDN_HEADS_PER_STEP, nh_a)
    hg_b = min(NA_HEADS_PER_STEP, nh_b)
    nhg_a = nh_a // hg_a

    xx = jnp.concatenate([ctx, x], axis=1)
    mp = -(-(bsz + 1) // 8) * 8
    cc = jnp.zeros((mp, d), _F32).at[:bsz].set(c).at[bsz].set(c_ctx)
    mod_all = _ada(cc, w_ada, b_ada).reshape(depth, mp, N_MOD, d)
    w_router_t = jnp.transpose(w_router).astype(_F32)

    col_b = 4 * gw + 4 * nh_a
    w_a = w_in[:, :, :4 * gw].astype(_BF16)
    w_b = w_in[:, :, col_b:].astype(_BF16)
    w_ab = jnp.zeros((depth, d, LANES), _BF16).at[:, :, :4 * nh_a].set(w_in[:, :, 4 * gw:col_b].astype(_BF16))
    w_out16 = w_out.astype(_BF16)

    for layer in range(depth):
        last = layer == depth - 1
        mod = mod_all[layer]

        a = _norm(xx, norm1_w[layer], mod, lc, 0, 1)
        a2 = a.reshape(t, d)
        proj_a = _matmul(a2, w_a, layer, _BF16).reshape(bsz, lt, 4 * gw)
        proj_b = _matmul(a2, w_b, layer, _BF16).reshape(bsz, lt, 3 * naw)
        ab = _matmul(a2, w_ab, layer, _F32)[:, :4 * nh_a].reshape(bsz, lt, 2, 2, nhg_a, hg_a)

        q, k, v = _gprep(proj_a, conv_w[layer], lc, nh_a)
        lane_pad = GATE_LANES - 2 * hg_a
        ab_r = jnp.transpose(ab, (3, 4, 0, 1, 2, 5)).reshape(2, nhg_a, bsz, lt, 2 * hg_a)
        ab_r = jnp.pad(ab_r, ((0, 0),) * 4 + ((0, lane_pad),))
        zpad = jnp.zeros((2, nhg_a, 1, hg_a), _F32)
        zlane = jnp.zeros((2, nhg_a, 1, lane_pad), _F32)
        alog_r = jnp.concatenate([zpad, a_log[layer].astype(_F32).reshape(2, nhg_a, 1, hg_a), zlane], axis=-1)
        dtb_r = jnp.concatenate([zpad, dt_bias[layer].astype(_F32).reshape(2, nhg_a, 1, hg_a), zlane], axis=-1)
        o_dir = _gdn(q, k, v, ab_r, alog_r, dtb_r, lc, hg_a)
        ya = _gdnout(o_dir, proj_a, 3, gdn_norm_w[layer])

        tab = _na_bias_table(na_rpb[layer], rows)
        nb = naw // (hg_b * HEAD_DIM)
        yb = _na(proj_b, 0, nb, 2 * nb, tab, qk_norm_w[layer].astype(_F32), lc, nh_b, hg_b)

        xx = _outproj(ya, yb, w_out16, layer, xx, mod, lc, 2)

        f, logits_t = _norm(xx, norm2_w[layer], mod, lc, 3, 4, w_router_t)
        y0, y1, w01 = _moe(f.reshape(t, d), logits_t, b_router, w_gate, w_up, w_down, layer)
        xx = _combine(xx, y0.reshape(bsz, lt, d), y1.reshape(bsz, lt, d), w01.reshape(bsz, lt, 2),
                      mod, lc, 5, last)
    return xx
```

```python
import functools
import math

import numpy as np
import jax
import jax.numpy as jnp
from jax import lax
from jax.experimental import pallas as pl
from jax.experimental.pallas import tpu as pltpu

HEAD_DIM = 128
GRID_W = 64
GDN_CHUNK = 64
SHORT_CONV = 5
NA_WIN_R = 8
NA_WIN_C = 16
N_EXPERT_GROUPS = 4
TOP_K = 2
N_MOD = 6
NORM_EPS = 1e-6
MASK_VALUE = -1e30

ROW_TILE = 256
MOE_TILE = 256
MOE_F_TILE = 512
MOE_D_TILE = 2048
LANES = 128
MM_COL_TILE = 512
GATE_LANES = 128
CONV_GAP = 8
GDN_HEADS_PER_STEP = 16
NA_HEADS_PER_STEP = 8
VMEM_LIMIT = 56 * 1024 * 1024

_F32 = jnp.float32
_BF16 = jnp.bfloat16
_HI = lax.Precision.HIGHEST


def _cp(sem, vmem=VMEM_LIMIT, flags=None):
    return pltpu.CompilerParams(dimension_semantics=sem, vmem_limit_bytes=vmem, flags=flags)


def _pick(n, cands):
    for c in cands:
        if n % c == 0:
            return c
    raise ValueError(f"no tile for {n} in {cands}")


def _dot(a, b):
    return jnp.dot(a, b, preferred_element_type=_F32)


def _dot_nt(a, b, precision=None):
    return lax.dot_general(a, b, (((1,), (1,)), ((), ())), preferred_element_type=_F32, precision=precision)


def _dot_tn(a, b):
    return lax.dot_general(a, b, (((0,), (0,)), ((), ())), preferred_element_type=_F32)


def _sigmoid(x):
    return 1.0 / (1.0 + jnp.exp(-x))


def _silu(x):
    return x * _sigmoid(x)


def _ada_kernel(cc_ref, w_ref, b_ref, o_ref):
    a = _silu(cc_ref[...]).astype(_BF16)
    o_ref[0] = _dot(a, w_ref[0].astype(_BF16)) + b_ref[0]


def _ada(cc, w_ada, b_ada):
    nl, d, n = w_ada.shape
    mp = cc.shape[0]
    tn = _pick(n, (512, 256, 128))
    return pl.pallas_call(
        _ada_kernel,
        grid=(nl, n // tn),
        in_specs=[
            pl.BlockSpec((mp, d), lambda l, j: (0, 0)),
            pl.BlockSpec((1, d, tn), lambda l, j: (l, 0, j)),
            pl.BlockSpec((1, 1, tn), lambda l, j: (l, 0, j)),
        ],
        out_specs=pl.BlockSpec((1, mp, tn), lambda l, j: (l, 0, j)),
        out_shape=jax.ShapeDtypeStruct((nl, mp, n), _F32),
        compiler_params=_cp(("arbitrary", "arbitrary")),
        name="ada_mod",
    )(cc, w_ada, b_ada.reshape(nl, 1, n))


def _modnorm(x, nw, mod, k_shift, k_scale):
    ms = jnp.mean(x * x, axis=-1, keepdims=True)
    y = x * lax.rsqrt(ms + NORM_EPS) * nw
    return y * (1.0 + mod[k_scale:k_scale + 1, :]) + mod[k_shift:k_shift + 1, :]


def _norm_kernel(x_ref, nw_ref, mod_ref, o_ref, *, k_shift, k_scale):
    y = _modnorm(x_ref[0], nw_ref[...], mod_ref[0], k_shift, k_scale)
    o_ref[0] = y.astype(o_ref.dtype)


def _norm_router_kernel(x_ref, nw_ref, mod_ref, wr_ref, o_ref, lg_ref, *, k_shift, k_scale):
    y = _modnorm(x_ref[0], nw_ref[...], mod_ref[0], k_shift, k_scale)
    o_ref[0] = y.astype(o_ref.dtype)
    lg_ref[...] = _dot_nt(wr_ref[...], y, precision=_HI)


def _mod_row(bsz, n_ctx_tiles):
    return lambda b, i: (jnp.where(i < n_ctx_tiles, bsz, b), 0, 0)


def _norm(xx, nw, mod, lc, k_shift, k_scale, w_router_t=None, latent_only=False):
    bsz, lt, d = xx.shape
    tr = ROW_TILE
    skip = lc // tr if latent_only else 0
    nt = lt // tr - skip
    mrow = _mod_row(bsz, lc // tr)
    in_specs = [
        pl.BlockSpec((1, tr, d), lambda b, i: (b, i + skip, 0)),
        pl.BlockSpec((1, d), lambda b, i: (0, 0)),
        pl.BlockSpec((1, N_MOD, d), lambda b, i: mrow(b, i + skip)),
    ]
    a_spec = pl.BlockSpec((1, tr, d), lambda b, i: (b, i, 0))
    a_shape = jax.ShapeDtypeStruct((bsz, nt * tr, d), _BF16)
    if w_router_t is None:
        return pl.pallas_call(
            functools.partial(_norm_kernel, k_shift=k_shift, k_scale=k_scale),
            grid=(bsz, nt), in_specs=in_specs, out_specs=a_spec, out_shape=a_shape,
            compiler_params=_cp(("arbitrary", "arbitrary")), name="modnorm",
        )(xx, nw.reshape(1, d), mod)
    ne = w_router_t.shape[0]
    return pl.pallas_call(
        functools.partial(_norm_router_kernel, k_shift=k_shift, k_scale=k_scale),
        grid=(bsz, nt),
        in_specs=in_specs + [pl.BlockSpec((ne, d), lambda b, i: (0, 0))],
        out_specs=[a_spec, pl.BlockSpec((ne, tr), lambda b, i: (0, b * nt + i))],
        out_shape=[a_shape, jax.ShapeDtypeStruct((ne, bsz * nt * tr), _F32)],
        compiler_params=_cp(("arbitrary", "arbitrary")), name="modnorm_router",
    )(xx, nw.reshape(1, d), mod, w_router_t)


def _mm_kernel(a_ref, w_ref, o_ref, w16_ref):
    @pl.when(pl.program_id(1) == 0)
    def _():
        w16_ref[...] = w_ref[0].astype(_BF16)

    o_ref[...] = _dot(a_ref[...], w16_ref[...]).astype(o_ref.dtype)


def _matmul(a, w, layer, n, out_dtype):
    m, k = a.shape
    tm = _pick(m, (1024, 512, 256, 128))
    tn = _pick(n, (MM_COL_TILE, 256, 128))
    return pl.pallas_call(
        _mm_kernel,
        grid=(n // tn, m // tm),
        in_specs=[pl.BlockSpec((tm, k), lambda j, i: (i, 0)),
                  pl.BlockSpec((1, k, tn), lambda j, i: (layer, 0, j))],
        out_specs=pl.BlockSpec((tm, tn), lambda j, i: (i, j)),
        out_shape=jax.ShapeDtypeStruct((m, n), out_dtype),
        scratch_shapes=[pltpu.VMEM((k, tn), _BF16)],
        compiler_params=_cp(("arbitrary", "arbitrary")), name="matmul",
    )(a, w)


def _conv_silu(pad_ref, x, w, lc):
    lt = x.shape[0]
    g = CONV_GAP
    pad = SHORT_CONV // 2
    zeros = jnp.zeros((g, x.shape[1]), _F32)
    pad_ref[0:g] = zeros
    pad_ref[g:g + lc] = x[0:lc]
    pad_ref[g + lc:2 * g + lc] = zeros
    pad_ref[2 * g + lc:2 * g + lt] = x[lc:lt]
    pad_ref[2 * g + lt:3 * g + lt] = zeros
    acc = None
    for j in range(SHORT_CONV):
        term = pad_ref[g + j - pad:2 * g + lt + j - pad] * w[j:j + 1, :]
        acc = term if acc is None else acc + term
    y = _silu(acc)
    return jnp.concatenate([y[0:lc], y[lc + g:lt + g]], axis=0)


def _l2n(t):
    return t * lax.rsqrt(jnp.sum(t * t, axis=-1, keepdims=True) + NORM_EPS)


def _gprep_kernel(q_ref, k_ref, v_ref, wq_ref, wk_ref, wv_ref, qo_ref, ko_ref, vo_ref, pad_ref, *, lc):
    q = _conv_silu(pad_ref, q_ref[0].astype(_F32), wq_ref[...], lc)
    qo_ref[0] = (_l2n(q) * (HEAD_DIM ** -0.5)).astype(qo_ref.dtype)
    k = _conv_silu(pad_ref, k_ref[0].astype(_F32), wk_ref[...], lc)
    ko_ref[0] = _l2n(k).astype(ko_ref.dtype)
    v = _conv_silu(pad_ref, v_ref[0].astype(_F32), wv_ref[...], lc)
    vo_ref[0] = v.astype(vo_ref.dtype)


def _gprep(proj, conv_w, lc, nh):
    bsz, lt, _ = proj.shape
    hd = HEAD_DIM

    def xs(off):
        return pl.BlockSpec((1, lt, hd), lambda b, h: (b, 0, off + h))

    def ws(off):
        return pl.BlockSpec((SHORT_CONV, hd), lambda b, h: (0, off + h))

    o_spec = pl.BlockSpec((1, lt, hd), lambda b, h: (b, 0, h))
    o_shape = jax.ShapeDtypeStruct((bsz, lt, nh * hd), _BF16)
    return pl.pallas_call(
        functools.partial(_gprep_kernel, lc=lc),
        grid=(bsz, nh),
        in_specs=[xs(0), xs(nh), xs(2 * nh), ws(0), ws(nh), ws(2 * nh)],
        out_specs=[o_spec, o_spec, o_spec],
        out_shape=[o_shape, o_shape, o_shape],
        scratch_shapes=[pltpu.VMEM((lt + 3 * CONV_GAP, hd), _F32)],
        compiler_params=_cp(("arbitrary", "arbitrary")), name="gdn_prep",
    )(proj, proj, proj, conv_w, conv_w, conv_w)


def _gdn_masks():
    c = GDN_CHUNK
    i = np.arange(c)[:, None]
    j = np.arange(c)[None, :]
    tri = np.stack([(i >= j), (i <= j)]).astype(np.float32)
    eye = np.eye(c, dtype=np.float32)
    lv = [(i // 2 == j // 2)]
    s = 2
    while s < c:
        lv.append((i // (2 * s) == j // (2 * s)) & (i // s != j // s))
        s *= 2
    return tri, eye, np.stack(lv).astype(np.float32)


def _gdn_kernel(q_ref, k_ref, v_ref, ab_ref, al_ref, dtb_ref, tri_ref, eye_ref, lv_ref, o_ref, s_ref, *, hg):
    c = GDN_CHUNK
    hd = HEAD_DIM
    t = pl.program_id(3)

    @pl.when(t == 0)
    def _():
        s_ref[...] = jnp.zeros_like(s_ref)

    tri = tri_ref[0]
    eye = eye_ref[...]
    strict = tri - eye
    ab = ab_ref[0, 0, 0]
    beta = _sigmoid(ab)
    z = ab + dtb_ref[0, 0]
    softplus = jnp.maximum(z, 0.0) + jnp.log(1.0 + jnp.exp(-jnp.abs(z)))
    g = -jnp.exp(al_ref[0, 0]) * softplus
    gc = jnp.dot(tri, g, preferred_element_type=_F32, precision=_HI)
    gtot = jnp.sum(g, axis=0, keepdims=True)
    n_lv = lv_ref.shape[0]
    heads = range(hg)
    sls = [slice(j * hd, (j + 1) * hd) for j in heads]

    b = [beta[:, j:j + 1] for j in heads]
    gj = [gc[:, hg + j:hg + j + 1] for j in heads]
    gt = [gtot[:, hg + j:hg + j + 1] for j in heads]
    eg = [jnp.exp(gj[j]) for j in heads]
    k16 = [k_ref[0, :, sls[j]] for j in heads]
    q16 = [q_ref[0, :, sls[j]] for j in heads]
    kf = [k16[j].astype(_F32) for j in heads]
    decay = []
    for j in heads:
        gcol = jnp.broadcast_to(gj[j], (c, c))
        grow = jnp.sum(gcol * eye, axis=0, keepdims=True)
        decay.append(jnp.exp(jnp.where(tri > 0.0, gcol - grow, -jnp.inf)))
    kbeta = [kf[j] * b[j] for j in heads]
    kq = [_dot_nt(jnp.concatenate([kbeta[j].astype(_BF16), q16[j]], axis=0), k16[j]) for j in heads]
    lmat = [kq[j][:c] * decay[j] * strict for j in heads]
    qk = [(kq[j][c:] * decay[j]).astype(_BF16) for j in heads]
    dinv = [eye - lmat[j] * lv_ref[0] for j in heads]
    for lv in range(1, n_lv):
        wmat = [_dot((lmat[j] * lv_ref[lv]).astype(_BF16), dinv[j].astype(_BF16)) for j in heads]
        dinv = [dinv[j] - _dot(dinv[j].astype(_BF16), wmat[j].astype(_BF16)) for j in heads]
    uw = []
    for j in heads:
        vf = v_ref[0, :, sls[j]].astype(_F32)
        rhs = jnp.concatenate([vf * b[j], kbeta[j] * eg[j]], axis=-1).astype(_BF16)
        uw.append(_dot(dinv[j].astype(_BF16), rhs))
    s = [s_ref[j] for j in heads]
    s16 = [s[j].astype(_BF16) for j in heads]
    q_dec = [(q16[j].astype(_F32) * eg[j]).astype(_BF16) for j in heads]
    k_dec = [(kf[j] * jnp.exp(gt[j] - gj[j])).astype(_BF16) for j in heads]
    ws = [_dot(jnp.concatenate([uw[j][:, hd:].astype(_BF16), q_dec[j]], axis=0), s16[j]) for j in heads]
    v16 = [(uw[j][:, :hd] - ws[j][:c]).astype(_BF16) for j in heads]
    outs = [ws[j][c:] + _dot(qk[j], v16[j]) for j in heads]
    states = [s[j] * jnp.exp(gt[j]) + _dot_tn(k_dec[j], v16[j]) for j in heads]

    for j in heads:
        o_ref[0, 0, :, sls[j]] = outs[j].astype(o_ref.dtype)
    for j in heads:
        s_ref[j] = states[j]


def _gdn(q, k, v, ab_r, alog_r, dtb_r, lc, hg):
    bsz, lt, gw = q.shape
    nh = gw // HEAD_DIM
    nhg = nh // hg
    c = GDN_CHUNK
    nc_c = lc // c
    nc = lt // c
    tri, eye, lvm = _gdn_masks()

    def chunk(d, t):
        bwd = jnp.where(t < nc_c, nc_c - 1 - t, nc - 1 + nc_c - t)
        return jnp.where(d == 0, t, bwd)

    x_spec = pl.BlockSpec((1, c, hg * HEAD_DIM), lambda b, d, g, t: (b, chunk(d, t), g))
    return pl.pallas_call(
        functools.partial(_gdn_kernel, hg=hg),
        grid=(bsz, 2, nhg, nc),
        in_specs=[
            x_spec, x_spec, x_spec,
            pl.BlockSpec((1, 1, 1, c, GATE_LANES), lambda b, d, g, t: (d, g, b, chunk(d, t), 0)),
            pl.BlockSpec((1, 1, 1, GATE_LANES), lambda b, d, g, t: (d, g, 0, 0)),
            pl.BlockSpec((1, 1, 1, GATE_LANES), lambda b, d, g, t: (d, g, 0, 0)),
            pl.BlockSpec((1, c, c), lambda b, d, g, t: (d, 0, 0)),
            pl.BlockSpec((c, c), lambda b, d, g, t: (0, 0)),
            pl.BlockSpec(lvm.shape, lambda b, d, g, t: (0, 0, 0)),
        ],
        out_specs=pl.BlockSpec((1, 1, c, hg * HEAD_DIM), lambda b, d, g, t: (d, b, chunk(d, t), g)),
        out_shape=jax.ShapeDtypeStruct((2, bsz, lt, gw), _BF16),
        scratch_shapes=[pltpu.VMEM((hg, HEAD_DIM, HEAD_DIM), _F32)],
        compiler_params=_cp(("arbitrary", "arbitrary", "arbitrary", "arbitrary")), name="gdn_scan",
    )(q, k, v, ab_r, alog_r, dtb_r, jnp.asarray(tri), jnp.asarray(eye), jnp.asarray(lvm))


def _gdnout_kernel(of_ref, ob_ref, z_ref, nw_ref, y_ref, *, nh):
    hd = HEAD_DIM
    for h in range(nh):
        sl = slice(h * hd, (h + 1) * hd)
        o = of_ref[0, 0, :, sl].astype(_F32) + ob_ref[0, 0, :, sl].astype(_F32)
        ms = jnp.mean(o * o, axis=-1, keepdims=True)
        y = o * lax.rsqrt(ms + NORM_EPS) * nw_ref[...]
        y_ref[0, :, sl] = (y * _silu(z_ref[0, :, sl].astype(_F32))).astype(y_ref.dtype)


def _gdnout(o_dir, proj, z_off_blocks, norm_w):
    _, bsz, lt, gw = o_dir.shape
    nh = gw // HEAD_DIM
    tr = ROW_TILE
    return pl.pallas_call(
        functools.partial(_gdnout_kernel, nh=nh),
        grid=(bsz, lt // tr),
        in_specs=[
            pl.BlockSpec((1, 1, tr, gw), lambda b, i: (0, b, i, 0)),
            pl.BlockSpec((1, 1, tr, gw), lambda b, i: (1, b, i, 0)),
            pl.BlockSpec((1, tr, gw), lambda b, i: (b, i, z_off_blocks)),
            pl.BlockSpec((1, HEAD_DIM), lambda b, i: (0, 0)),
        ],
        out_specs=pl.BlockSpec((1, tr, gw), lambda b, i: (b, i, 0)),
        out_shape=jax.ShapeDtypeStruct((bsz, lt, gw), _BF16),
        compiler_params=_cp(("arbitrary", "arbitrary")), name="gdn_out",
    )(o_dir, o_dir, proj, norm_w.reshape(1, HEAD_DIM))


def _na_bias_table(rpb, rows):
    wc = NA_WIN_C
    qc = np.arange(GRID_W)[:, None]
    kc = np.arange(GRID_W)[None, :]
    win_start = np.clip(qc - wc // 2, 0, GRID_W - wc)
    valid = (kc >= win_start) & (kc < win_start + wc)
    off = np.clip(kc - qc + wc - 1, 0, 2 * wc - 2)
    ro = np.arange(NA_WIN_R)[:, None] + np.arange(NA_WIN_R)[None, :]
    tab = rpb.astype(_F32)[:, ro][:, :, :, off]
    tab = jnp.where(valid[None, None, None], tab, MASK_VALUE)
    tab = jnp.transpose(tab, (0, 1, 3, 2, 4))
    return tab.reshape(rpb.shape[0], NA_WIN_R, GRID_W, NA_WIN_R * GRID_W)


def _rms_head(t, w):
    return t * lax.rsqrt(jnp.mean(t * t, axis=-1, keepdims=True) + NORM_EPS) * w


def _na_kernel(q_ref, k_ref, v_ref, tab_ref, nw_ref, o_ref, kn_ref, *, hg, lc, rows):
    hd = HEAD_DIM
    w = GRID_W
    nloc = NA_WIN_R * w
    ncc = lc // w
    r = pl.program_id(2)

    @pl.when(r == 0)
    def _():
        for j in range(hg):
            sl = slice(j * hd, (j + 1) * hd)
            kn_ref[:, sl] = _rms_head(k_ref[0, :, sl].astype(_F32), nw_ref[1:2, :]).astype(kn_ref.dtype)

    rr = r - ncc
    sr = jnp.clip(rr - NA_WIN_R // 2, 0, rows - NA_WIN_R)
    start = pl.multiple_of(lc + sr * w, w)
    ro = jnp.clip(sr - rr + NA_WIN_R - 1, 0, NA_WIN_R - 1)

    heads = range(hg)
    sls = [slice(j * hd, (j + 1) * hd) for j in heads]

    def normed_q():
        return [(_rms_head(q_ref[0, :, sls[j]].astype(_F32), nw_ref[0:1, :]) * (hd ** -0.5)).astype(_BF16)
                for j in heads]

    def store(outs):
        for j in heads:
            o_ref[0, :, sls[j]] = outs[j].astype(o_ref.dtype)

    def lane_tile_reduce(op, *arrays):
        tiles = [a[:, i:i + LANES] for a in arrays for i in range(0, a.shape[1], LANES)]
        acc = tiles[0]
        for tl in tiles[1:]:
            acc = op(acc, tl)
        return acc

    @pl.when(r < ncc)
    def _():
        qn = normed_q()
        s_ctx = [_dot_nt(qn[j], kn_ref[0:lc, sls[j]]) for j in heads]
        p = [jnp.exp(s_ctx[j] - jnp.max(s_ctx[j], axis=-1, keepdims=True)) for j in heads]
        den = [jnp.sum(p[j], axis=-1, keepdims=True) for j in heads]
        store([_dot(p[j].astype(_BF16), v_ref[0, 0:lc, sls[j]]) / den[j] for j in heads])

    @pl.when(r >= ncc)
    def _():
        qn = normed_q()
        s_ctx = [_dot_nt(qn[j], kn_ref[0:lc, sls[j]]) for j in heads]
        s_loc = [_dot_nt(qn[j], kn_ref[pl.ds(start, nloc), sls[j]]) + tab_ref[j, ro] for j in heads]
        m = [jnp.max(lane_tile_reduce(jnp.maximum, s_loc[j], s_ctx[j]), axis=-1, keepdims=True) for j in heads]
        p_loc = [jnp.exp(s_loc[j] - m[j]) for j in heads]
        p_ctx = [jnp.exp(s_ctx[j] - m[j]) for j in heads]
        den = [jnp.sum(lane_tile_reduce(jnp.add, p_loc[j], p_ctx[j]), axis=-1, keepdims=True) for j in heads]
        o = [_dot(p_loc[j].astype(_BF16), v_ref[0, pl.ds(start, nloc), sls[j]])
             + _dot(p_ctx[j].astype(_BF16), v_ref[0, 0:lc, sls[j]]) for j in heads]
        store([o[j] / den[j] for j in heads])


def _na(proj, q_off, k_off, v_off, tab, qk_norm_w, lc, nh, hg):
    bsz, lt, _ = proj.shape
    rows = (lt - lc) // GRID_W
    nhg = nh // hg
    bw = hg * HEAD_DIM
    return pl.pallas_call(
        functools.partial(_na_kernel, hg=hg, lc=lc, rows=rows),
        grid=(bsz, nhg, lt // GRID_W),
        in_specs=[
            pl.BlockSpec((1, GRID_W, bw), lambda b, g, r: (b, r, q_off + g)),
            pl.BlockSpec((1, lt, bw), lambda b, g, r: (b, 0, k_off + g)),
            pl.BlockSpec((1, lt, bw), lambda b, g, r: (b, 0, v_off + g)),
            pl.BlockSpec((hg,) + tab.shape[1:], lambda b, g, r: (g, 0, 0, 0)),
            pl.BlockSpec((2, HEAD_DIM), lambda b, g, r: (0, 0)),
        ],
        out_specs=pl.BlockSpec((1, GRID_W, bw), lambda b, g, r: (b, r, g)),
        out_shape=jax.ShapeDtypeStruct((bsz, lt, nh * HEAD_DIM), _BF16),
        scratch_shapes=[pltpu.VMEM((lt, bw), _BF16)],
        compiler_params=_cp(("arbitrary", "arbitrary", "arbitrary")), name="natten",
    )(proj, proj, proj, tab, qk_norm_w)


def _outproj_kernel(ya_ref, yb_ref, wa_ref, wb_ref, x_ref, ml_ref, mc_ref, o_ref, *, lc, tiles_per_sample, k_gate):
    tm = x_ref.shape[0]
    acc = _dot(ya_ref[...], wa_ref[0]) + _dot(yb_ref[...], wb_ref[0])
    row = (pl.program_id(0) % tiles_per_sample) * tm + lax.broadcasted_iota(jnp.int32, acc.shape, 0)
    gate = jnp.where(row < lc, mc_ref[0, k_gate:k_gate + 1, :], ml_ref[0, k_gate:k_gate + 1, :])
    o_ref[...] = x_ref[...] + gate * acc


def _outproj(ya, yb, w_out, layer, xx, mod, lc, k_gate):
    bsz, lt, d = xx.shape
    t = bsz * lt
    ga = ya.shape[-1]
    gb = yb.shape[-1]
    tm = _pick(lt, (1152, 768, 576, 384, 256, 128))
    tn = _pick(d, (512, 256, 128))
    tps = lt // tm
    out = pl.pallas_call(
        functools.partial(_outproj_kernel, lc=lc, tiles_per_sample=tps, k_gate=k_gate),
        grid=(t // tm, d // tn),
        in_specs=[
            pl.BlockSpec((tm, ga), lambda i, j: (i, 0)),
            pl.BlockSpec((tm, gb), lambda i, j: (i, 0)),
            pl.BlockSpec((1, ga, tn), lambda i, j: (layer, 0, j)),
            pl.BlockSpec((1, gb, tn), lambda i, j: (layer, ga // gb, j)),
            pl.BlockSpec((tm, tn), lambda i, j: (i, j)),
            pl.BlockSpec((1, N_MOD, tn), lambda i, j: (i // tps, 0, j)),
            pl.BlockSpec((1, N_MOD, tn), lambda i, j: (bsz, 0, j)),
        ],
        out_specs=pl.BlockSpec((tm, tn), lambda i, j: (i, j)),
        out_shape=jax.ShapeDtypeStruct((t, d), _F32),
        compiler_params=_cp(("arbitrary", "arbitrary")), name="out_proj",
    )(ya.reshape(t, ga), yb.reshape(t, gb), w_out, w_out, xx.reshape(t, d), mod, mod)
    return out.reshape(bsz, lt, d)


def _route_kernel(lg_ref, br_ref, su_ref, oi_ref, ow_ref, cnt_ref, carry_ref, *, ne):
    epg = ne // N_EXPERT_GROUPS
    tq = lg_ref.shape[1]

    @pl.when(pl.program_id(0) == 0)
    def _():
        carry_ref[...] = jnp.zeros_like(carry_ref)

    sc = _sigmoid(lg_ref[...])
    sel = sc + br_ref[...]
    s = [sel[e:e + 1, :] for e in range(ne)]

    def top2(vals):
        a, b, c_, d_ = vals
        hi1, lo1 = jnp.maximum(a, b), jnp.minimum(a, b)
        hi2, lo2 = jnp.maximum(c_, d_), jnp.minimum(c_, d_)
        return jnp.maximum(hi1, hi2) + jnp.maximum(jnp.maximum(lo1, lo2), jnp.minimum(hi1, hi2))

    gs = [top2(s[g * epg:(g + 1) * epg]) for g in range(N_EXPERT_GROUPS)]
    best = gs[0]
    gsel = jnp.zeros_like(best, dtype=jnp.int32)
    for g in range(1, N_EXPERT_GROUPS):
        better = gs[g] > best
        gsel = jnp.where(better, g, gsel)
        best = jnp.maximum(best, gs[g])
    ing = []
    for j in range(epg):
        v = s[j]
        for g in range(1, N_EXPERT_GROUPS):
            v = jnp.where(gsel == g, s[g * epg + j], v)
        ing.append(v)
    b1 = ing[0]
    i1 = jnp.zeros_like(gsel)
    for j in range(1, epg):
        better = ing[j] > b1
        i1 = jnp.where(better, j, i1)
        b1 = jnp.maximum(b1, ing[j])
    b2 = jnp.full_like(b1, -jnp.inf)
    i2 = jnp.zeros_like(gsel)
    for j in range(epg):
        better = (ing[j] > b2) & (i1 != j)
        i2 = jnp.where(better, j, i2)
        b2 = jnp.where(better, ing[j], b2)
    e1 = gsel * epg + i1
    e2 = gsel * epg + i2

    eio = lax.broadcasted_iota(jnp.int32, (ne, tq), 0)
    oh1 = eio == e1
    oh2 = eio == e2
    w1 = jnp.sum(jnp.where(oh1, sc, 0.0), axis=0, keepdims=True)
    w2 = jnp.sum(jnp.where(oh2, sc, 0.0), axis=0, keepdims=True)
    wsum = w1 + w2
    oh = jnp.where(oh1, 1.0, 0.0) + jnp.where(oh2, 1.0, 0.0)
    rank = _dot(oh.astype(_BF16), su_ref[...]) + carry_ref[:, 0:1]
    r1 = jnp.sum(jnp.where(oh1, rank, 0.0), axis=0, keepdims=True)
    r2 = jnp.sum(jnp.where(oh2, rank, 0.0), axis=0, keepdims=True)
    carry = carry_ref[...] + jnp.sum(oh, axis=1, keepdims=True)
    carry_ref[...] = carry
    cnt_ref[...] = carry

    sub = lax.broadcasted_iota(jnp.int32, (8, tq), 0)
    oi = jnp.where(sub == 0, e1, jnp.where(sub == 1, e2,
         jnp.where(sub == 2, r1.astype(jnp.int32), jnp.where(sub == 3, r2.astype(jnp.int32), 0))))
    oi_ref[...] = oi
    ow_ref[...] = jnp.where(sub == 0, w1 / wsum, jnp.where(sub == 1, w2 / wsum, 0.0))


def _route(logits_t, b_router):
    ne, t = logits_t.shape
    tq = ROW_TILE
    su = np.triu(np.ones((tq, tq), np.float32), 1)
    return pl.pallas_call(
        functools.partial(_route_kernel, ne=ne),
        grid=(t // tq,),
        in_specs=[pl.BlockSpec((ne, tq), lambda i: (0, i)),
                  pl.BlockSpec((ne, 1), lambda i: (0, 0)),
                  pl.BlockSpec((tq, tq), lambda i: (0, 0))],
        out_specs=[pl.BlockSpec((8, tq), lambda i: (0, i)),
                   pl.BlockSpec((8, tq), lambda i: (0, i)),
                   pl.BlockSpec((ne, 128), lambda i: (0, 0))],
        out_shape=[jax.ShapeDtypeStruct((8, t), jnp.int32),
                   jax.ShapeDtypeStruct((8, t), _F32),
                   jax.ShapeDtypeStruct((ne, 128), _F32)],
        scratch_shapes=[pltpu.VMEM((ne, 128), _F32)],
        compiler_params=_cp(("arbitrary",)), name="router",
    )(logits_t, b_router.reshape(ne, 1).astype(_F32), jnp.asarray(su, _BF16))


def _new_weight_block(te_ref, i):
    return (i == 0) | (te_ref[i] != te_ref[jnp.maximum(i - 1, 0)])


def _moe_up_kernel(te_ref, x_ref, wg_ref, wu_ref, h_ref, wg16_ref, wu16_ref):
    i = pl.program_id(1)
    n_valid = te_ref[te_ref.shape[0] - 1]

    @pl.when(_new_weight_block(te_ref, i))
    def _():
        wg16_ref[...] = wg_ref[0, 0].astype(_BF16)
        wu16_ref[...] = wu_ref[0, 0].astype(_BF16)

    @pl.when(i < n_valid)
    def _():
        x = x_ref[...]
        gt = _dot(x, wg16_ref[...])
        up = _dot(x, wu16_ref[...])
        h_ref[...] = (_silu(gt) * up).astype(h_ref.dtype)

    @pl.when(i >= n_valid)
    def _():
        h_ref[...] = jnp.zeros_like(h_ref)


def _moe_down_kernel(te_ref, h0_ref, h1_ref, wd_ref, o_ref, wd16_ref, *, half):
    i = pl.program_id(1)
    n_valid = te_ref[te_ref.shape[0] - 1]

    @pl.when(_new_weight_block(te_ref, i))
    def _():
        wd16_ref[...] = wd_ref[0, 0].astype(_BF16)

    @pl.when(i < jnp.minimum(n_valid, half))
    def _():
        o_ref[...] = _dot(h0_ref[...], wd16_ref[...]).astype(o_ref.dtype)

    @pl.when((i >= half) & (i < n_valid))
    def _():
        o_ref[...] = _dot(h1_ref[...], wd16_ref[...]).astype(o_ref.dtype)

    @pl.when(i >= n_valid)
    def _():
        o_ref[...] = jnp.zeros_like(o_ref)


def _moe_experts(halves, tile_e, w_gate, w_up, w_down, layer):
    d = halves[0][0].shape[1]
    f = w_gate.shape[-1]
    tm = MOE_TILE
    tf = _pick(f, (MOE_F_TILE, 256, 128))
    td = _pick(d, (MOE_D_TILE, 1024, 512, 256, 128))
    hids = []
    for xs, te_half in halves:
        nth = xs.shape[0] // tm
        hids.append(pl.pallas_call(
            _moe_up_kernel,
            grid_spec=pltpu.PrefetchScalarGridSpec(
                num_scalar_prefetch=1, grid=(f // tf, nth),
                in_specs=[pl.BlockSpec((tm, d), lambda j, i, te: (i, 0)),
                          pl.BlockSpec((1, 1, d, tf), lambda j, i, te: (layer, te[i], 0, j)),
                          pl.BlockSpec((1, 1, d, tf), lambda j, i, te: (layer, te[i], 0, j))],
                out_specs=pl.BlockSpec((tm, tf), lambda j, i, te: (i, j)),
                scratch_shapes=[pltpu.VMEM((d, tf), _BF16), pltpu.VMEM((d, tf), _BF16)]),
            out_shape=jax.ShapeDtypeStruct((nth * tm, f), _BF16),
            compiler_params=_cp(("arbitrary", "arbitrary")), name="moe_up",
        )(te_half, xs, w_gate, w_up))
    half = hids[0].shape[0] // tm
    nt = half + hids[1].shape[0] // tm
    return pl.pallas_call(
        functools.partial(_moe_down_kernel, half=half),
        grid_spec=pltpu.PrefetchScalarGridSpec(
            num_scalar_prefetch=1, grid=(d // td, nt),
            in_specs=[pl.BlockSpec((tm, f), lambda j, i, te: (jnp.minimum(i, half - 1), 0)),
                      pl.BlockSpec((tm, f), lambda j, i, te: (jnp.maximum(i - half, 0), 0)),
                      pl.BlockSpec((1, 1, f, td), lambda j, i, te: (layer, te[i], 0, j))],
            out_specs=pl.BlockSpec((tm, td), lambda j, i, te: (i, j)),
            scratch_shapes=[pltpu.VMEM((f, td), _BF16)]),
        out_shape=jax.ShapeDtypeStruct((nt * tm, d), _BF16),
        compiler_params=_cp(("arbitrary", "arbitrary")), name="moe_down",
    )(tile_e, hids[0], hids[1], w_down)


def _comb_kernel(x_ref, y0_ref, y1_ref, w_ref, mod_ref, o_ref, *, k_gate):
    y = y0_ref[0].astype(_F32) * w_ref[0, :, 0:1] + y1_ref[0].astype(_F32) * w_ref[0, :, 1:2]
    o_ref[0] = x_ref[0] + mod_ref[0, k_gate:k_gate + 1, :] * y


def _combine(xx, y0, y1, w01, mod, lc, k_gate, latent_only):
    bsz, lt, d = xx.shape
    tr = ROW_TILE
    skip = lc // tr if latent_only else 0
    nt = lt // tr - skip
    row = lambda b, i: (b, i + skip, 0)
    yrow = lambda b, i: (b, i, 0)
    mrow = _mod_row(bsz, lc // tr)
    return pl.pallas_call(
        functools.partial(_comb_kernel, k_gate=k_gate),
        grid=(bsz, nt),
        in_specs=[pl.BlockSpec((1, tr, d), row), pl.BlockSpec((1, tr, d), yrow), pl.BlockSpec((1, tr, d), yrow),
                  pl.BlockSpec((1, tr, 2), yrow),
                  pl.BlockSpec((1, N_MOD, d), lambda b, i: mrow(b, i + skip))],
        out_specs=pl.BlockSpec((1, tr, d), lambda b, i: (b, i, 0)),
        out_shape=jax.ShapeDtypeStruct((bsz, nt * tr, d), _F32),
        compiler_params=_cp(("arbitrary", "arbitrary")), name="moe_combine",
    )(xx, y0, y1, w01, mod)


def _moe(f, logits_t, b_router, w_gate, w_up, w_down, layer):
    t, d = f.shape
    ne = w_gate.shape[1]
    tm = MOE_TILE
    oi, ow, cnt = _route(logits_t, b_router)
    e01 = oi[0:2]
    r01 = oi[2:4]
    counts = cnt[:, 0].astype(jnp.int32)
    tiles = (counts + tm - 1) // tm
    tile_end = jnp.cumsum(tiles)
    off = (tile_end - tiles) * tm
    first = jnp.sum(jnp.where(e01[None] == jnp.arange(ne, dtype=jnp.int32)[:, None, None],
                              off[:, None, None], 0), axis=0)
    pos = first + r01
    nt = (t * TOP_K) // tm + ne
    p = nt * tm
    tok = jnp.broadcast_to(jnp.arange(t, dtype=jnp.int32)[None, :], (TOP_K, t))
    src = jnp.zeros((p,), jnp.int32).at[pos.reshape(-1)].set(tok.reshape(-1))
    n_valid = tile_end[-1]
    tile_idx = jnp.arange(nt, dtype=jnp.int32)
    tile_e = jnp.sum((tile_end[None, :] <= jnp.minimum(tile_idx, n_valid - 1)[:, None]).astype(jnp.int32), axis=1)
    half = nt // 2
    halves = []
    for lo, hi in ((0, half), (half, nt)):
        n_val = jnp.clip(n_valid - lo, 0, hi - lo).astype(jnp.int32)
        halves.append((f.at[src[lo * tm:hi * tm]].get(mode="promise_in_bounds"),
                       jnp.concatenate([tile_e[lo:hi], n_val[None]])))
    tile_e = jnp.concatenate([tile_e, n_valid[None].astype(jnp.int32)])
    out = _moe_experts(halves, tile_e, w_gate, w_up, w_down, layer)
    y0 = out.at[pos[0]].get(mode="promise_in_bounds")
    y1 = out.at[pos[1]].get(mode="promise_in_bounds")
    return y0, y1, jnp.transpose(ow[0:2])


def kernel(x, c, ctx, c_ctx, norm1_w, norm2_w, w_ada, b_ada, w_in, conv_w, a_log, dt_bias, gdn_norm_w,
           qk_norm_w, na_rpb, w_out, w_router, b_router, w_gate, w_up, w_down):
    bsz, s_len, d = x.shape
    lc = ctx.shape[1]
    lt = lc + s_len
    t = bsz * lt
    depth = w_ada.shape[0]
    gw = d // 2
    naw = d - gw
    nh_a = gw // HEAD_DIM
    nh_b = naw // HEAD_DIM
    rows = s_len // GRID_W
    assert s_len % GRID_W == 0 and rows >= NA_WIN_R
    assert lc % ROW_TILE == 0 and s_len % ROW_TILE == 0
    hg_a = min(GDN_HEADS_PER_STEP, nh_a)
    hg_b = min(NA_HEADS_PER_STEP, nh_b)
    nhg_a = nh_a // hg_a

    xx = jnp.concatenate([ctx, x], axis=1)
    mp = -(-(bsz + 1) // 8) * 8
    cc = jnp.zeros((mp, d), _F32).at[:bsz].set(c).at[bsz].set(c_ctx)
    mod_all = _ada(cc, w_ada, b_ada).reshape(depth, mp, N_MOD, d)
    w_router_t = jnp.transpose(w_router).astype(_F32)

    col_b = 4 * gw + 4 * nh_a
    w_b = w_in[:, :, col_b:]
    w_ab = jnp.zeros((depth, d, LANES), _F32).at[:, :, :4 * nh_a].set(w_in[:, :, 4 * gw:col_b])
    w_out16 = w_out.astype(_BF16)

    for layer in range(depth):
        last = layer == depth - 1
        mod = mod_all[layer]

        a = _norm(xx, norm1_w[layer], mod, lc, 0, 1)
        a2 = a.reshape(t, d)
        proj_a = _matmul(a2, w_in, layer, 4 * gw, _BF16).reshape(bsz, lt, 4 * gw)
        proj_b = _matmul(a2, w_b, layer, 3 * naw, _BF16).reshape(bsz, lt, 3 * naw)
        ab = _matmul(a2, w_ab, layer, LANES, _F32)[:, :4 * nh_a].reshape(bsz, lt, 2, 2, nhg_a, hg_a)

        q, k, v = _gprep(proj_a, conv_w[layer], lc, nh_a)
        lane_pad = GATE_LANES - 2 * hg_a
        ab_r = jnp.transpose(ab, (3, 4, 0, 1, 2, 5)).reshape(2, nhg_a, bsz, lt, 2 * hg_a)
        ab_r = jnp.pad(ab_r, ((0, 0),) * 4 + ((0, lane_pad),))
        zpad = jnp.zeros((2, nhg_a, 1, hg_a), _F32)
        zlane = jnp.zeros((2, nhg_a, 1, lane_pad), _F32)
        alog_r = jnp.concatenate([zpad, a_log[layer].astype(_F32).reshape(2, nhg_a, 1, hg_a), zlane], axis=-1)
        dtb_r = jnp.concatenate([zpad, dt_bias[layer].astype(_F32).reshape(2, nhg_a, 1, hg_a), zlane], axis=-1)
        o_dir = _gdn(q, k, v, ab_r, alog_r, dtb_r, lc, hg_a)
        ya = _gdnout(o_dir, proj_a, 3, gdn_norm_w[layer])

        tab = _na_bias_table(na_rpb[layer], rows)
        nb = naw // (hg_b * HEAD_DIM)
        yb = _na(proj_b, 0, nb, 2 * nb, tab, qk_norm_w[layer].astype(_F32), lc, nh_b, hg_b)

        xx = _outproj(ya, yb, w_out16, layer, xx, mod, lc, 2)

        f, logits_t = _norm(xx, norm2_w[layer], mod, lc, 3, 4, w_router_t, latent_only=last)
        rows_moe = f.shape[1]
        y0, y1, w01 = _moe(f.reshape(bsz * rows_moe, d), logits_t, b_router, w_gate, w_up, w_down, layer)
        xx = _combine(xx, y0.reshape(bsz, rows_moe, d), y1.reshape(bsz, rows_moe, d),
                      w01.reshape(bsz, rows_moe, 2), mod, lc, 5, last)
    return xx
```

```python
import functools
import math

import numpy as np
import jax
import jax.numpy as jnp
from jax import lax
from jax.experimental import pallas as pl
from jax.experimental.pallas import tpu as pltpu

HEAD_DIM = 128
GRID_W = 64
GDN_CHUNK = 64
SHORT_CONV = 5
NA_WIN_R = 8
NA_WIN_C = 16
N_EXPERT_GROUPS = 4
TOP_K = 2
N_MOD = 6
NORM_EPS = 1e-6
MASK_VALUE = -1e30

ROW_TILE = 256
MOE_TILE = 256
MOE_F_TILE = 512
MOE_D_TILE = 2048
MOE_GATHER_PARTS = 4
LANES = 128
CAST_COL_TILE = 512
GATE_LANES = 128
CONV_GAP = 8
GDN_HEADS_PER_STEP = 16
GDN_PACK = 4
NA_HEADS_PER_STEP = 8
VMEM_LIMIT = 56 * 1024 * 1024

_F32 = jnp.float32
_BF16 = jnp.bfloat16
_HI = lax.Precision.HIGHEST


def _cp(sem, vmem=VMEM_LIMIT, flags=None):
    return pltpu.CompilerParams(dimension_semantics=sem, vmem_limit_bytes=vmem, flags=flags)


def _pick(n, cands):
    for c in cands:
        if n % c == 0:
            return c
    raise ValueError(f"no tile for {n} in {cands}")


def _dot(a, b):
    return jnp.dot(a, b, preferred_element_type=_F32)


def _dot_nt(a, b, precision=None):
    return lax.dot_general(a, b, (((1,), (1,)), ((), ())), preferred_element_type=_F32, precision=precision)


def _dot_tn(a, b):
    return lax.dot_general(a, b, (((0,), (0,)), ((), ())), preferred_element_type=_F32)


def _sigmoid(x):
    return 1.0 / (1.0 + jnp.exp(-x))


def _silu(x):
    return x * _sigmoid(x)


def _ada_kernel(cc_ref, w_ref, b_ref, o_ref):
    a = _silu(cc_ref[...]).astype(_BF16)
    o_ref[0] = _dot(a, w_ref[0].astype(_BF16)) + b_ref[0]


def _ada(cc, w_ada, b_ada):
    nl, d, n = w_ada.shape
    mp = cc.shape[0]
    tn = _pick(n, (512, 256, 128))
    return pl.pallas_call(
        _ada_kernel,
        grid=(nl, n // tn),
        in_specs=[
            pl.BlockSpec((mp, d), lambda l, j: (0, 0)),
            pl.BlockSpec((1, d, tn), lambda l, j: (l, 0, j)),
            pl.BlockSpec((1, 1, tn), lambda l, j: (l, 0, j)),
        ],
        out_specs=pl.BlockSpec((1, mp, tn), lambda l, j: (l, 0, j)),
        out_shape=jax.ShapeDtypeStruct((nl, mp, n), _F32),
        compiler_params=_cp(("arbitrary", "arbitrary")),
        name="ada_mod",
    )(cc, w_ada, b_ada.reshape(nl, 1, n))


def _modnorm(x, nw, mod, k_shift, k_scale):
    ms = jnp.mean(x * x, axis=-1, keepdims=True)
    y = x * lax.rsqrt(ms + NORM_EPS) * nw
    return y * (1.0 + mod[k_scale:k_scale + 1, :]) + mod[k_shift:k_shift + 1, :]


def _norm_kernel(x_ref, nw_ref, mod_ref, o_ref, *, k_shift, k_scale):
    y = _modnorm(x_ref[0], nw_ref[...], mod_ref[0], k_shift, k_scale)
    o_ref[0] = y.astype(o_ref.dtype)


def _norm_router_kernel(x_ref, nw_ref, mod_ref, wr_ref, o_ref, lg_ref, *, k_shift, k_scale):
    y = _modnorm(x_ref[0], nw_ref[...], mod_ref[0], k_shift, k_scale)
    o_ref[0] = y.astype(o_ref.dtype)
    lg_ref[...] = _dot_nt(wr_ref[...], y, precision=_HI)


def _mod_row(bsz, n_ctx_tiles):
    return lambda b, i: (jnp.where(i < n_ctx_tiles, bsz, b), 0, 0)


def _norm(xx, nw, mod, lc, k_shift, k_scale, w_router_t=None, latent_only=False):
    bsz, lt, d = xx.shape
    tr = ROW_TILE
    skip = lc // tr if latent_only else 0
    nt = lt // tr - skip
    mrow = _mod_row(bsz, lc // tr)
    in_specs = [
        pl.BlockSpec((1, tr, d), lambda b, i: (b, i + skip, 0)),
        pl.BlockSpec((1, d), lambda b, i: (0, 0)),
        pl.BlockSpec((1, N_MOD, d), lambda b, i: mrow(b, i + skip)),
    ]
    a_spec = pl.BlockSpec((1, tr, d), lambda b, i: (b, i, 0))
    a_shape = jax.ShapeDtypeStruct((bsz, nt * tr, d), _BF16)
    if w_router_t is None:
        return pl.pallas_call(
            functools.partial(_norm_kernel, k_shift=k_shift, k_scale=k_scale),
            grid=(bsz, nt), in_specs=in_specs, out_specs=a_spec, out_shape=a_shape,
            compiler_params=_cp(("arbitrary", "arbitrary")), name="modnorm",
        )(xx, nw.reshape(1, d), mod)
    ne = w_router_t.shape[0]
    return pl.pallas_call(
        functools.partial(_norm_router_kernel, k_shift=k_shift, k_scale=k_scale),
        grid=(bsz, nt),
        in_specs=in_specs + [pl.BlockSpec((ne, d), lambda b, i: (0, 0))],
        out_specs=[a_spec, pl.BlockSpec((ne, tr), lambda b, i: (0, b * nt + i))],
        out_shape=[a_shape, jax.ShapeDtypeStruct((ne, bsz * nt * tr), _F32)],
        compiler_params=_cp(("arbitrary", "arbitrary")), name="modnorm_router",
    )(xx, nw.reshape(1, d), mod, w_router_t)


def _cast_cols_kernel(a_ref, b_ref, o_ref, *, shift):
    if shift == 0:
        o_ref[0] = a_ref[0].astype(o_ref.dtype)
    else:
        o_ref[0] = jnp.concatenate([a_ref[0][:, shift:], b_ref[0][:, :shift]], axis=1).astype(o_ref.dtype)


def _cast_cols(w, col0, n):
    depth, k, total = w.shape
    tc = _pick(n, (CAST_COL_TILE, 256, 128))
    tk = _pick(k, (1024, 512, 256, 128))
    b0, shift = divmod(col0, tc)
    assert shift == 0 or (b0 + n // tc) * tc < total
    nb = 1 if shift else 0
    return pl.pallas_call(
        functools.partial(_cast_cols_kernel, shift=shift),
        grid=(depth, k // tk, n // tc),
        in_specs=[pl.BlockSpec((1, tk, tc), lambda l, r, j: (l, r, b0 + j)),
                  pl.BlockSpec((1, tk, tc), lambda l, r, j: (l, r, b0 + j + nb))],
        out_specs=pl.BlockSpec((1, tk, tc), lambda l, r, j: (l, r, j)),
        out_shape=jax.ShapeDtypeStruct((depth, k, n), _BF16),
        compiler_params=_cp(("arbitrary", "arbitrary", "arbitrary")), name="cast_cols",
    )(w, w)


def _mm_kernel(a_ref, w_ref, o_ref):
    o_ref[...] = _dot(a_ref[...], w_ref[0]).astype(o_ref.dtype)


def _matmul(a, w, layer, out_dtype):
    m, k = a.shape
    n = w.shape[2]
    tm = _pick(m, (1024, 512, 256, 128))
    tn = _pick(n, (1024, 512, 256, 128))
    return pl.pallas_call(
        _mm_kernel,
        grid=(n // tn, m // tm),
        in_specs=[pl.BlockSpec((tm, k), lambda j, i: (i, 0)),
                  pl.BlockSpec((1, k, tn), lambda j, i: (layer, 0, j))],
        out_specs=pl.BlockSpec((tm, tn), lambda j, i: (i, j)),
        out_shape=jax.ShapeDtypeStruct((m, n), out_dtype),
        compiler_params=_cp(("arbitrary", "arbitrary")), name="matmul",
    )(a, w)


def _conv_silu(pad_ref, x, w, lc):
    lt = x.shape[0]
    g = CONV_GAP
    pad = SHORT_CONV // 2
    zeros = jnp.zeros((g, x.shape[1]), _F32)
    pad_ref[0:g] = zeros
    pad_ref[g:g + lc] = x[0:lc]
    pad_ref[g + lc:2 * g + lc] = zeros
    pad_ref[2 * g + lc:2 * g + lt] = x[lc:lt]
    pad_ref[2 * g + lt:3 * g + lt] = zeros
    acc = None
    for j in range(SHORT_CONV):
        term = pad_ref[g + j - pad:2 * g + lt + j - pad] * w[j:j + 1, :]
        acc = term if acc is None else acc + term
    y = _silu(acc)
    return jnp.concatenate([y[0:lc], y[lc + g:lt + g]], axis=0)


def _l2n(t):
    return t * lax.rsqrt(jnp.sum(t * t, axis=-1, keepdims=True) + NORM_EPS)


def _gprep_kernel(q_ref, k_ref, v_ref, wq_ref, wk_ref, wv_ref, qo_ref, ko_ref, vo_ref, pad_ref, *, lc):
    q = _conv_silu(pad_ref, q_ref[0].astype(_F32), wq_ref[...], lc)
    qo_ref[0] = (_l2n(q) * (HEAD_DIM ** -0.5)).astype(qo_ref.dtype)
    k = _conv_silu(pad_ref, k_ref[0].astype(_F32), wk_ref[...], lc)
    ko_ref[0] = _l2n(k).astype(ko_ref.dtype)
    v = _conv_silu(pad_ref, v_ref[0].astype(_F32), wv_ref[...], lc)
    vo_ref[0] = v.astype(vo_ref.dtype)


def _gprep(proj, conv_w, lc, nh):
    bsz, lt, _ = proj.shape
    hd = HEAD_DIM

    def xs(off):
        return pl.BlockSpec((1, lt, hd), lambda b, h: (b, 0, off + h))

    def ws(off):
        return pl.BlockSpec((SHORT_CONV, hd), lambda b, h: (0, off + h))

    o_spec = pl.BlockSpec((1, lt, hd), lambda b, h: (b, 0, h))
    o_shape = jax.ShapeDtypeStruct((bsz, lt, nh * hd), _BF16)
    return pl.pallas_call(
        functools.partial(_gprep_kernel, lc=lc),
        grid=(bsz, nh),
        in_specs=[xs(0), xs(nh), xs(2 * nh), ws(0), ws(nh), ws(2 * nh)],
        out_specs=[o_spec, o_spec, o_spec],
        out_shape=[o_shape, o_shape, o_shape],
        scratch_shapes=[pltpu.VMEM((lt + 3 * CONV_GAP, hd), _F32)],
        compiler_params=_cp(("arbitrary", "arbitrary")), name="gdn_prep",
    )(proj, proj, proj, conv_w, conv_w, conv_w)


def _gdn_masks():
    c = GDN_CHUNK
    i = np.arange(c)[:, None]
    j = np.arange(c)[None, :]
    tri = np.stack([(i >= j), (i <= j)]).astype(np.float32)
    eye = np.eye(c, dtype=np.float32)
    lv = [(i // 2 == j // 2)]
    s = 2
    while s < c:
        lv.append((i // (2 * s) == j // (2 * s)) & (i // s != j // s))
        s *= 2
    return tri, eye, np.stack(lv).astype(np.float32)


def _gdn_kernel(qf_ref, kf_ref, vf_ref, qb_ref, kb_ref, vb_ref, abf_ref, abb_ref, al_ref, dtb_ref, tri_ref, eye_ref,
                lv_ref, of_ref, ob_ref, s_ref, *, hg, pk):
    c = GDN_CHUNK
    hd = HEAD_DIM
    t = pl.program_id(2)

    @pl.when(t == 0)
    def _():
        s_ref[...] = jnp.zeros_like(s_ref)

    nd = 2
    wl = pk * c
    q_refs, k_refs, v_refs, o_refs = (qf_ref, qb_ref), (kf_ref, kb_ref), (vf_ref, vb_ref), (of_ref, ob_ref)
    eye = eye_ref[...]
    tri_d = [tri_ref[dr] for dr in range(nd)]
    strict_d = [tri_d[dr] - eye for dr in range(nd)]
    beta_d, gc_d, gtot_d = [], [], []
    for dr, ab_ref in enumerate((abf_ref, abb_ref)):
        ab = ab_ref[0, 0, 0]
        z = ab + dtb_ref[dr, 0]
        softplus = jnp.maximum(z, 0.0) + jnp.log(1.0 + jnp.exp(-jnp.abs(z)))
        g = -jnp.exp(al_ref[dr, 0]) * softplus
        beta_d.append(_sigmoid(ab))
        gc_d.append(jnp.dot(tri_d[dr][:, :c], g, preferred_element_type=_F32, precision=_HI))
        gtot_d.append(jnp.sum(g, axis=0, keepdims=True))
    n_lv = lv_ref.shape[0]
    gpd = hg // pk
    heads = range(nd * hg)
    groups = range(nd * gpd)
    members = range(pk)
    sls = [slice(j * hd, (j + 1) * hd) for j in range(hg)]
    gsl = [slice(gi * pk * hd, (gi + 1) * pk * hd) for gi in range(gpd)]
    tri = [tri_d[gi // gpd] for gi in groups]
    strict = [strict_d[gi // gpd] for gi in groups]
    lane_blk = lax.broadcasted_iota(jnp.int32, (c, wl), 1) // c
    head_blk = lax.broadcasted_iota(jnp.int32, (c, pk * hd), 1) // hd

    def block_diag(x):
        return jnp.concatenate([jnp.where(lane_blk == a, x, 0.0) for a in members], axis=0).astype(_BF16)

    def per_member_cols(vals, width):
        return jnp.concatenate([jnp.broadcast_to(v, (c, width)) for v in vals], axis=1)

    b = [beta_d[j // hg][:, j % hg:j % hg + 1] for j in heads]
    gj = [gc_d[j // hg][:, hg + j % hg:hg + j % hg + 1] for j in heads]
    gt = [gtot_d[j // hg][:, hg + j % hg:hg + j % hg + 1] for j in heads]
    eg = [jnp.exp(gj[j]) for j in heads]
    k16 = [k_refs[gi // gpd][0, :, gsl[gi % gpd]] for gi in groups]
    q16 = [q_refs[gi // gpd][0, :, gsl[gi % gpd]] for gi in groups]
    kf = [k16[gi].astype(_F32) for gi in groups]
    kbeta = [kf[gi] * per_member_cols([b[gi * pk + a] for a in members], hd) for gi in groups]
    decay = []
    for gi in groups:
        gcol = jnp.broadcast_to(gj[gi * pk], (c, wl))
        for a in members[1:]:
            gcol = jnp.where(lane_blk == a, gj[gi * pk + a], gcol)
        grow = jnp.sum(gcol * eye, axis=0, keepdims=True)
        decay.append(jnp.exp(jnp.where(tri[gi] > 0.0, gcol - grow, -jnp.inf)))
    kq = []
    for gi in groups:
        k_bd = jnp.concatenate([jnp.where(head_blk == a, kf[gi], 0.0) for a in members], axis=0).astype(_BF16)
        kq.append(_dot_nt(jnp.concatenate([kbeta[gi].astype(_BF16), q16[gi]], axis=0), k_bd))
    lmat = [kq[gi][:c] * decay[gi] * strict[gi] for gi in groups]
    qk = [kq[gi][c:] * decay[gi] for gi in groups]
    dinv = [eye - lmat[gi] * lv_ref[0] for gi in groups]
    for lv in range(1, n_lv):
        wmat = [_dot((lmat[gi] * lv_ref[lv]).astype(_BF16), block_diag(dinv[gi])) for gi in groups]
        dinv = [dinv[gi] - _dot(dinv[gi].astype(_BF16), block_diag(wmat[gi])) for gi in groups]
    uw = []
    for gi in groups:
        vf = v_refs[gi // gpd][0, :, gsl[gi % gpd]].astype(_F32)
        rhs = jnp.concatenate(
            [jnp.concatenate([vf[:, a * hd:(a + 1) * hd] * b[gi * pk + a],
                              kbeta[gi][:, a * hd:(a + 1) * hd] * eg[gi * pk + a]], axis=1) for a in members],
            axis=0).astype(_BF16)
        uw.append(_dot(block_diag(dinv[gi]), rhs))
    s = [s_ref[j] for j in heads]
    s16 = [s[j].astype(_BF16) for j in heads]
    q_dec = [(q16[j // pk][:, (j % pk) * hd:(j % pk + 1) * hd].astype(_F32) * eg[j]).astype(_BF16) for j in heads]
    k_dec = [(kf[j // pk][:, (j % pk) * hd:(j % pk + 1) * hd] * jnp.exp(gt[j] - gj[j])).astype(_BF16) for j in heads]
    uw_h = [uw[j // pk][(j % pk) * c:(j % pk + 1) * c] for j in heads]
    ws = [_dot(jnp.concatenate([uw_h[j][:, hd:].astype(_BF16), q_dec[j]], axis=0), s16[j]) for j in heads]
    v16 = [(uw_h[j][:, :hd] - ws[j][:c]).astype(_BF16) for j in heads]
    intra = [_dot(block_diag(qk[gi]), jnp.concatenate([v16[gi * pk + a] for a in members], axis=0))
             for gi in groups]
    outs = [ws[j][c:] + intra[j // pk][(j % pk) * c:(j % pk + 1) * c] for j in heads]
    states = [s[j] * jnp.exp(gt[j]) + _dot_tn(k_dec[j], v16[j]) for j in heads]

    for j in heads:
        o_refs[j // hg][0, :, sls[j % hg]] = outs[j].astype(of_ref.dtype)
    for j in heads:
        s_ref[j] = states[j]


def _gdn(q, k, v, ab_r, alog_r, dtb_r, lc, hg):
    bsz, lt, gw = q.shape
    nh = gw // HEAD_DIM
    nhg = nh // hg
    c = GDN_CHUNK
    nc_c = lc // c
    nc = lt // c
    pk = min(GDN_PACK, hg)
    assert hg % pk == 0 and (pk * c) % LANES == 0
    tri, eye, lvm = (np.tile(m, pk) for m in _gdn_masks())

    def bwd(t):
        return jnp.where(t < nc_c, nc_c - 1 - t, nc - 1 + nc_c - t)

    bw = hg * HEAD_DIM
    xf_spec = pl.BlockSpec((1, c, bw), lambda b, g, t: (b, t, g))
    xb_spec = pl.BlockSpec((1, c, bw), lambda b, g, t: (b, bwd(t), g))
    gate_spec = pl.BlockSpec((2, 1, 1, GATE_LANES), lambda b, g, t: (0, g, 0, 0))
    o_shape = jax.ShapeDtypeStruct((bsz, lt, gw), _BF16)
    return pl.pallas_call(
        functools.partial(_gdn_kernel, hg=hg, pk=pk),
        grid=(bsz, nhg, nc),
        in_specs=[
            xf_spec, xf_spec, xf_spec, xb_spec, xb_spec, xb_spec,
            pl.BlockSpec((1, 1, 1, c, GATE_LANES), lambda b, g, t: (0, g, b, t, 0)),
            pl.BlockSpec((1, 1, 1, c, GATE_LANES), lambda b, g, t: (1, g, b, bwd(t), 0)),
            gate_spec, gate_spec,
            pl.BlockSpec((2, c, pk * c), lambda b, g, t: (0, 0, 0)),
            pl.BlockSpec((c, pk * c), lambda b, g, t: (0, 0)),
            pl.BlockSpec(lvm.shape, lambda b, g, t: (0, 0, 0)),
        ],
        out_specs=[xf_spec, xb_spec],
        out_shape=[o_shape, o_shape],
        scratch_shapes=[pltpu.VMEM((2 * hg, HEAD_DIM, HEAD_DIM), _F32)],
        compiler_params=_cp(("arbitrary", "arbitrary", "arbitrary")), name="gdn_scan",
    )(q, k, v, q, k, v, ab_r, ab_r, alog_r, dtb_r, jnp.asarray(tri), jnp.asarray(eye), jnp.asarray(lvm))


def _gdnout_kernel(of_ref, ob_ref, z_ref, nw_ref, y_ref, *, nh):
    hd = HEAD_DIM
    for h in range(nh):
        sl = slice(h * hd, (h + 1) * hd)
        o = of_ref[0, :, sl].astype(_F32) + ob_ref[0, :, sl].astype(_F32)
        ms = jnp.mean(o * o, axis=-1, keepdims=True)
        y = o * lax.rsqrt(ms + NORM_EPS) * nw_ref[...]
        y_ref[0, :, sl] = (y * _silu(z_ref[0, :, sl].astype(_F32))).astype(y_ref.dtype)


def _gdnout(o_fwd, o_bwd, proj, z_off_blocks, norm_w):
    bsz, lt, gw = o_fwd.shape
    nh = gw // HEAD_DIM
    tr = ROW_TILE
    return pl.pallas_call(
        functools.partial(_gdnout_kernel, nh=nh),
        grid=(bsz, lt // tr),
        in_specs=[
            pl.BlockSpec((1, tr, gw), lambda b, i: (b, i, 0)),
            pl.BlockSpec((1, tr, gw), lambda b, i: (b, i, 0)),
            pl.BlockSpec((1, tr, gw), lambda b, i: (b, i, z_off_blocks)),
            pl.BlockSpec((1, HEAD_DIM), lambda b, i: (0, 0)),
        ],
        out_specs=pl.BlockSpec((1, tr, gw), lambda b, i: (b, i, 0)),
        out_shape=jax.ShapeDtypeStruct((bsz, lt, gw), _BF16),
        compiler_params=_cp(("arbitrary", "arbitrary")), name="gdn_out",
    )(o_fwd, o_bwd, proj, norm_w.reshape(1, HEAD_DIM))


def _na_bias_table(rpb, rows):
    wc = NA_WIN_C
    qc = np.arange(GRID_W)[:, None]
    kc = np.arange(GRID_W)[None, :]
    win_start = np.clip(qc - wc // 2, 0, GRID_W - wc)
    valid = (kc >= win_start) & (kc < win_start + wc)
    off = np.clip(kc - qc + wc - 1, 0, 2 * wc - 2)
    ro = np.arange(NA_WIN_R)[:, None] + np.arange(NA_WIN_R)[None, :]
    tab = rpb.astype(_F32)[:, ro][:, :, :, off]
    tab = jnp.where(valid[None, None, None], tab, MASK_VALUE)
    tab = jnp.transpose(tab, (0, 1, 3, 2, 4))
    return tab.reshape(rpb.shape[0], NA_WIN_R, GRID_W, NA_WIN_R * GRID_W)


def _rms_head(t, w):
    return t * lax.rsqrt(jnp.mean(t * t, axis=-1, keepdims=True) + NORM_EPS) * w


def _na_kernel(q_ref, k_ref, v_ref, tab_ref, nw_ref, o_ref, kn_ref, *, hg, lc, rows):
    hd = HEAD_DIM
    w = GRID_W
    nloc = NA_WIN_R * w
    ncc = lc // w
    r = pl.program_id(2)

    @pl.when(r == 0)
    def _():
        for j in range(hg):
            sl = slice(j * hd, (j + 1) * hd)
            kn_ref[:, sl] = _rms_head(k_ref[0, :, sl].astype(_F32), nw_ref[1:2, :]).astype(kn_ref.dtype)

    rr = r - ncc
    sr = jnp.clip(rr - NA_WIN_R // 2, 0, rows - NA_WIN_R)
    start = pl.multiple_of(lc + sr * w, w)
    ro = jnp.clip(sr - rr + NA_WIN_R - 1, 0, NA_WIN_R - 1)

    heads = range(hg)
    sls = [slice(j * hd, (j + 1) * hd) for j in heads]

    def normed_q():
        return [(_rms_head(q_ref[0, :, sls[j]].astype(_F32), nw_ref[0:1, :]) * (hd ** -0.5)).astype(_BF16)
                for j in heads]

    def store(outs):
        for j in heads:
            o_ref[0, :, sls[j]] = outs[j].astype(o_ref.dtype)

    def lane_tile_reduce(op, *arrays):
        tiles = [a[:, i:i + LANES] for a in arrays for i in range(0, a.shape[1], LANES)]
        acc = tiles[0]
        for tl in tiles[1:]:
            acc = op(acc, tl)
        return acc

    @pl.when(r < ncc)
    def _():
        qn = normed_q()
        s_ctx = [_dot_nt(qn[j], kn_ref[0:lc, sls[j]]) for j in heads]
        p = [jnp.exp(s_ctx[j] - jnp.max(s_ctx[j], axis=-1, keepdims=True)) for j in heads]
        den = [jnp.sum(p[j], axis=-1, keepdims=True) for j in heads]
        store([_dot(p[j].astype(_BF16), v_ref[0, 0:lc, sls[j]]) / den[j] for j in heads])

    @pl.when(r >= ncc)
    def _():
        qn = normed_q()
        s_ctx = [_dot_nt(qn[j], kn_ref[0:lc, sls[j]]) for j in heads]
        s_loc = [_dot_nt(qn[j], kn_ref[pl.ds(start, nloc), sls[j]]) + tab_ref[j, ro] for j in heads]
        m = [jnp.max(lane_tile_reduce(jnp.maximum, s_loc[j], s_ctx[j]), axis=-1, keepdims=True) for j in heads]
        p_loc = [jnp.exp(s_loc[j] - m[j]) for j in heads]
        p_ctx = [jnp.exp(s_ctx[j] - m[j]) for j in heads]
        den = [jnp.sum(lane_tile_reduce(jnp.add, p_loc[j], p_ctx[j]), axis=-1, keepdims=True) for j in heads]
        o = [_dot(p_loc[j].astype(_BF16), v_ref[0, pl.ds(start, nloc), sls[j]])
             + _dot(p_ctx[j].astype(_BF16), v_ref[0, 0:lc, sls[j]]) for j in heads]
        store([o[j] / den[j] for j in heads])


def _na(proj, q_off, k_off, v_off, tab, qk_norm_w, lc, nh, hg):
    bsz, lt, _ = proj.shape
    rows = (lt - lc) // GRID_W
    nhg = nh // hg
    bw = hg * HEAD_DIM
    return pl.pallas_call(
        functools.partial(_na_kernel, hg=hg, lc=lc, rows=rows),
        grid=(bsz, nhg, lt // GRID_W),
        in_specs=[
            pl.BlockSpec((1, GRID_W, bw), lambda b, g, r: (b, r, q_off + g)),
            pl.BlockSpec((1, lt, bw), lambda b, g, r: (b, 0, k_off + g)),
            pl.BlockSpec((1, lt, bw), lambda b, g, r: (b, 0, v_off + g)),
            pl.BlockSpec((hg,) + tab.shape[1:], lambda b, g, r: (g, 0, 0, 0)),
            pl.BlockSpec((2, HEAD_DIM), lambda b, g, r: (0, 0)),
        ],
        out_specs=pl.BlockSpec((1, GRID_W, bw), lambda b, g, r: (b, r, g)),
        out_shape=jax.ShapeDtypeStruct((bsz, lt, nh * HEAD_DIM), _BF16),
        scratch_shapes=[pltpu.VMEM((lt, bw), _BF16)],
        compiler_params=_cp(("arbitrary", "arbitrary", "arbitrary")), name="natten",
    )(proj, proj, proj, tab, qk_norm_w)


def _outproj_kernel(ya_ref, yb_ref, wa_ref, wb_ref, x_ref, ml_ref, mc_ref, o_ref, *, lc, tiles_per_sample, k_gate):
    tm = x_ref.shape[0]
    acc = _dot(ya_ref[...], wa_ref[0]) + _dot(yb_ref[...], wb_ref[0])
    row = (pl.program_id(0) % tiles_per_sample) * tm + lax.broadcasted_iota(jnp.int32, acc.shape, 0)
    gate = jnp.where(row < lc, mc_ref[0, k_gate:k_gate + 1, :], ml_ref[0, k_gate:k_gate + 1, :])
    o_ref[...] = x_ref[...] + gate * acc


def _outproj(ya, yb, w_out, layer, xx, mod, lc, k_gate):
    bsz, lt, d = xx.shape
    t = bsz * lt
    ga = ya.shape[-1]
    gb = yb.shape[-1]
    tm = _pick(lt, (1152, 768, 576, 384, 256, 128))
    tn = _pick(d, (512, 256, 128))
    tps = lt // tm
    out = pl.pallas_call(
        functools.partial(_outproj_kernel, lc=lc, tiles_per_sample=tps, k_gate=k_gate),
        grid=(t // tm, d // tn),
        in_specs=[
            pl.BlockSpec((tm, ga), lambda i, j: (i, 0)),
            pl.BlockSpec((tm, gb), lambda i, j: (i, 0)),
            pl.BlockSpec((1, ga, tn), lambda i, j: (layer, 0, j)),
            pl.BlockSpec((1, gb, tn), lambda i, j: (layer, ga // gb, j)),
            pl.BlockSpec((tm, tn), lambda i, j: (i, j)),
            pl.BlockSpec((1, N_MOD, tn), lambda i, j: (i // tps, 0, j)),
            pl.BlockSpec((1, N_MOD, tn), lambda i, j: (bsz, 0, j)),
        ],
        out_specs=pl.BlockSpec((tm, tn), lambda i, j: (i, j)),
        out_shape=jax.ShapeDtypeStruct((t, d), _F32),
        compiler_params=_cp(("arbitrary", "arbitrary")), name="out_proj",
    )(ya.reshape(t, ga), yb.reshape(t, gb), w_out, w_out, xx.reshape(t, d), mod, mod)
    return out.reshape(bsz, lt, d)


def _route_kernel(lg_ref, br_ref, su_ref, oi_ref, ow_ref, cnt_ref, carry_ref, *, ne):
    epg = ne // N_EXPERT_GROUPS
    tq = lg_ref.shape[1]

    @pl.when(pl.program_id(0) == 0)
    def _():
        carry_ref[...] = jnp.zeros_like(carry_ref)

    sc = _sigmoid(lg_ref[...])
    sel = sc + br_ref[...]
    s = [sel[e:e + 1, :] for e in range(ne)]

    def top2(vals):
        a, b, c_, d_ = vals
        hi1, lo1 = jnp.maximum(a, b), jnp.minimum(a, b)
        hi2, lo2 = jnp.maximum(c_, d_), jnp.minimum(c_, d_)
        return jnp.maximum(hi1, hi2) + jnp.maximum(jnp.maximum(lo1, lo2), jnp.minimum(hi1, hi2))

    gs = [top2(s[g * epg:(g + 1) * epg]) for g in range(N_EXPERT_GROUPS)]
    best = gs[0]
    gsel = jnp.zeros_like(best, dtype=jnp.int32)
    for g in range(1, N_EXPERT_GROUPS):
        better = gs[g] > best
        gsel = jnp.where(better, g, gsel)
        best = jnp.maximum(best, gs[g])
    ing = []
    for j in range(epg):
        v = s[j]
        for g in range(1, N_EXPERT_GROUPS):
            v = jnp.where(gsel == g, s[g * epg + j], v)
        ing.append(v)
    b1 = ing[0]
    i1 = jnp.zeros_like(gsel)
    for j in range(1, epg):
        better = ing[j] > b1
        i1 = jnp.where(better, j, i1)
        b1 = jnp.maximum(b1, ing[j])
    b2 = jnp.full_like(b1, -jnp.inf)
    i2 = jnp.zeros_like(gsel)
    for j in range(epg):
        better = (ing[j] > b2) & (i1 != j)
        i2 = jnp.where(better, j, i2)
        b2 = jnp.where(better, ing[j], b2)
    e1 = gsel * epg + i1
    e2 = gsel * epg + i2

    eio = lax.broadcasted_iota(jnp.int32, (ne, tq), 0)
    oh1 = eio == e1
    oh2 = eio == e2
    w1 = jnp.sum(jnp.where(oh1, sc, 0.0), axis=0, keepdims=True)
    w2 = jnp.sum(jnp.where(oh2, sc, 0.0), axis=0, keepdims=True)
    wsum = w1 + w2
    oh = jnp.where(oh1, 1.0, 0.0) + jnp.where(oh2, 1.0, 0.0)
    rank = _dot(oh.astype(_BF16), su_ref[...]) + carry_ref[:, 0:1]
    r1 = jnp.sum(jnp.where(oh1, rank, 0.0), axis=0, keepdims=True)
    r2 = jnp.sum(jnp.where(oh2, rank, 0.0), axis=0, keepdims=True)
    carry = carry_ref[...] + jnp.sum(oh, axis=1, keepdims=True)
    carry_ref[...] = carry
    cnt_ref[...] = carry

    sub = lax.broadcasted_iota(jnp.int32, (8, tq), 0)
    oi = jnp.where(sub == 0, e1, jnp.where(sub == 1, e2,
         jnp.where(sub == 2, r1.astype(jnp.int32), jnp.where(sub == 3, r2.astype(jnp.int32), 0))))
    oi_ref[...] = oi
    ow_ref[...] = jnp.where(sub == 0, w1 / wsum, jnp.where(sub == 1, w2 / wsum, 0.0))


def _route(logits_t, b_router):
    ne, t = logits_t.shape
    tq = ROW_TILE
    su = np.triu(np.ones((tq, tq), np.float32), 1)
    return pl.pallas_call(
        functools.partial(_route_kernel, ne=ne),
        grid=(t // tq,),
        in_specs=[pl.BlockSpec((ne, tq), lambda i: (0, i)),
                  pl.BlockSpec((ne, 1), lambda i: (0, 0)),
                  pl.BlockSpec((tq, tq), lambda i: (0, 0))],
        out_specs=[pl.BlockSpec((8, tq), lambda i: (0, i)),
                   pl.BlockSpec((8, tq), lambda i: (0, i)),
                   pl.BlockSpec((ne, 128), lambda i: (0, 0))],
        out_shape=[jax.ShapeDtypeStruct((8, t), jnp.int32),
                   jax.ShapeDtypeStruct((8, t), _F32),
                   jax.ShapeDtypeStruct((ne, 128), _F32)],
        scratch_shapes=[pltpu.VMEM((ne, 128), _F32)],
        compiler_params=_cp(("arbitrary",)), name="router",
    )(logits_t, b_router.reshape(ne, 1).astype(_F32), jnp.asarray(su, _BF16))


def _new_weight_block(te_ref, i):
    return (i == 0) | (te_ref[i] != te_ref[jnp.maximum(i - 1, 0)])


def _moe_up_kernel(te_ref, x_ref, wg_ref, wu_ref, h_ref, wg16_ref, wu16_ref):
    i = pl.program_id(1)
    n_valid = te_ref[te_ref.shape[0] - 1]

    @pl.when(_new_weight_block(te_ref, i))
    def _():
        wg16_ref[...] = wg_ref[0, 0].astype(_BF16)
        wu16_ref[...] = wu_ref[0, 0].astype(_BF16)

    @pl.when(i < n_valid)
    def _():
        x = x_ref[...]
        gt = _dot(x, wg16_ref[...])
        up = _dot(x, wu16_ref[...])
        h_ref[...] = (_silu(gt) * up).astype(h_ref.dtype)

    @pl.when(i >= n_valid)
    def _():
        h_ref[...] = jnp.zeros_like(h_ref)


def _moe_down_kernel(te_ref, *refs, n_parts, part_tiles):
    h_refs = refs[:n_parts]
    wd_ref, o_ref, wd16_ref = refs[n_parts:]
    i = pl.program_id(1)
    n_valid = te_ref[te_ref.shape[0] - 1]

    @pl.when(_new_weight_block(te_ref, i))
    def _():
        wd16_ref[...] = wd_ref[0, 0].astype(_BF16)

    for part in range(n_parts):
        @pl.when((i >= part * part_tiles) & (i < jnp.minimum(n_valid, (part + 1) * part_tiles)))
        def _(h_ref=h_refs[part]):
            o_ref[...] = _dot(h_ref[...], wd16_ref[...]).astype(o_ref.dtype)

    @pl.when(i >= n_valid)
    def _():
        o_ref[...] = jnp.zeros_like(o_ref)


def _moe_experts(parts, tile_e, w_gate, w_up, w_down, layer):
    d = parts[0][0].shape[1]
    f = w_gate.shape[-1]
    tm = MOE_TILE
    tf = _pick(f, (MOE_F_TILE, 256, 128))
    td = _pick(d, (MOE_D_TILE, 1024, 512, 256, 128))
    hids = []
    for xs, te_half in parts:
        nth = xs.shape[0] // tm
        hids.append(pl.pallas_call(
            _moe_up_kernel,
            grid_spec=pltpu.PrefetchScalarGridSpec(
                num_scalar_prefetch=1, grid=(f // tf, nth),
                in_specs=[pl.BlockSpec((tm, d), lambda j, i, te: (i, 0)),
                          pl.BlockSpec((1, 1, d, tf), lambda j, i, te: (layer, te[i], 0, j)),
                          pl.BlockSpec((1, 1, d, tf), lambda j, i, te: (layer, te[i], 0, j))],
                out_specs=pl.BlockSpec((tm, tf), lambda j, i, te: (i, j)),
                scratch_shapes=[pltpu.VMEM((d, tf), _BF16), pltpu.VMEM((d, tf), _BF16)]),
            out_shape=jax.ShapeDtypeStruct((nth * tm, f), _BF16),
            compiler_params=_cp(("arbitrary", "arbitrary")), name="moe_up",
        )(te_half, xs, w_gate, w_up))
    n_parts = len(hids)
    part_tiles = hids[0].shape[0] // tm
    nt = n_parts * part_tiles

    def part_spec(part):
        return pl.BlockSpec((tm, f), lambda j, i, te: (jnp.clip(i - part * part_tiles, 0, part_tiles - 1), 0))

    return pl.pallas_call(
        functools.partial(_moe_down_kernel, n_parts=n_parts, part_tiles=part_tiles),
        grid_spec=pltpu.PrefetchScalarGridSpec(
            num_scalar_prefetch=1, grid=(d // td, nt),
            in_specs=[part_spec(part) for part in range(n_parts)]
                     + [pl.BlockSpec((1, 1, f, td), lambda j, i, te: (layer, te[i], 0, j))],
            out_specs=pl.BlockSpec((tm, td), lambda j, i, te: (i, j)),
            scratch_shapes=[pltpu.VMEM((f, td), _BF16)]),
        out_shape=jax.ShapeDtypeStruct((nt * tm, d), _BF16),
        compiler_params=_cp(("arbitrary", "arbitrary")), name="moe_down",
    )(tile_e, *hids, w_down)


def _comb_kernel(x_ref, y0_ref, y1_ref, w_ref, mod_ref, o_ref, *, k_gate):
    y = y0_ref[0].astype(_F32) * w_ref[0, :, 0:1] + y1_ref[0].astype(_F32) * w_ref[0, :, 1:2]
    o_ref[0] = x_ref[0] + mod_ref[0, k_gate:k_gate + 1, :] * y


def _combine(xx, y0, y1, w01, mod, lc, k_gate, latent_only):
    bsz, lt, d = xx.shape
    tr = ROW_TILE
    skip = lc // tr if latent_only else 0
    nt = lt // tr - skip
    row = lambda b, i: (b, i + skip, 0)
    yrow = lambda b, i: (b, i, 0)
    mrow = _mod_row(bsz, lc // tr)
    return pl.pallas_call(
        functools.partial(_comb_kernel, k_gate=k_gate),
        grid=(bsz, nt),
        in_specs=[pl.BlockSpec((1, tr, d), row), pl.BlockSpec((1, tr, d), yrow), pl.BlockSpec((1, tr, d), yrow),
                  pl.BlockSpec((1, tr, 2), yrow),
                  pl.BlockSpec((1, N_MOD, d), lambda b, i: mrow(b, i + skip))],
        out_specs=pl.BlockSpec((1, tr, d), lambda b, i: (b, i, 0)),
        out_shape=jax.ShapeDtypeStruct((bsz, nt * tr, d), _F32),
        compiler_params=_cp(("arbitrary", "arbitrary")), name="moe_combine",
    )(xx, y0, y1, w01, mod)


def _moe(f, logits_t, b_router, w_gate, w_up, w_down, layer):
    t, d = f.shape
    ne = w_gate.shape[1]
    tm = MOE_TILE
    oi, ow, cnt = _route(logits_t, b_router)
    e01 = oi[0:2]
    r01 = oi[2:4]
    counts = cnt[:, 0].astype(jnp.int32)
    tiles = (counts + tm - 1) // tm
    tile_end = jnp.cumsum(tiles)
    off = (tile_end - tiles) * tm
    first = jnp.sum(jnp.where(e01[None] == jnp.arange(ne, dtype=jnp.int32)[:, None, None],
                              off[:, None, None], 0), axis=0)
    pos = first + r01
    nt = (t * TOP_K) // tm + ne
    p = nt * tm
    tok = jnp.broadcast_to(jnp.arange(t, dtype=jnp.int32)[None, :], (TOP_K, t))
    src = jnp.zeros((p,), jnp.int32).at[pos.reshape(-1)].set(tok.reshape(-1))
    n_valid = tile_end[-1]
    tile_idx = jnp.arange(nt, dtype=jnp.int32)
    tile_e = jnp.sum((tile_end[None, :] <= jnp.minimum(tile_idx, n_valid - 1)[:, None]).astype(jnp.int32), axis=1)
    assert nt % MOE_GATHER_PARTS == 0
    part_tiles = nt // MOE_GATHER_PARTS
    parts = []
    for lo in range(0, nt, part_tiles):
        hi = lo + part_tiles
        n_val = jnp.clip(n_valid - lo, 0, part_tiles).astype(jnp.int32)
        parts.append((f.at[src[lo * tm:hi * tm]].get(mode="promise_in_bounds"),
                      jnp.concatenate([tile_e[lo:hi], n_val[None]])))
    tile_e = jnp.concatenate([tile_e, n_valid[None].astype(jnp.int32)])
    out = _moe_experts(parts, tile_e, w_gate, w_up, w_down, layer)
    y0 = out.at[pos[0]].get(mode="promise_in_bounds")
    y1 = out.at[pos[1]].get(mode="promise_in_bounds")
    return y0, y1, jnp.transpose(ow[0:2])


def kernel(x, c, ctx, c_ctx, norm1_w, norm2_w, w_ada, b_ada, w_in, conv_w, a_log, dt_bias, gdn_norm_w,
           qk_norm_w, na_rpb, w_out, w_router, b_router, w_gate, w_up, w_down):
    bsz, s_len, d = x.shape
    lc = ctx.shape[1]
    lt = lc + s_len
    t = bsz * lt
    depth = w_ada.shape[0]
    gw = d // 2
    naw = d - gw
    nh_a = gw // HEAD_DIM
    nh_b = naw // HEAD_DIM
    rows = s_len // GRID_W
    assert s_len % GRID_W == 0 and rows >= NA_WIN_R
    assert lc % ROW_TILE == 0 and s_len % ROW_TILE == 0
    hg_a = min(GDN_HEADS_PER_STEP, nh_a)
    hg_b = min(NA_HEADS_PER_STEP, nh_b)
    nhg_a = nh_a // hg_a

    xx = jnp.concatenate([ctx, x], axis=1)
    mp = -(-(bsz + 1) // 8) * 8
    cc = jnp.zeros((mp, d), _F32).at[:bsz].set(c).at[bsz].set(c_ctx)
    mod_all = _ada(cc, w_ada, b_ada).reshape(depth, mp, N_MOD, d)
    w_router_t = jnp.transpose(w_router).astype(_F32)

    col_b = 4 * gw + 4 * nh_a
    w_a = _cast_cols(w_in, 0, 4 * gw)
    w_b = _cast_cols(w_in, col_b, 3 * naw)
    w_ab = jnp.zeros((depth, d, LANES), _BF16).at[:, :, :4 * nh_a].set(w_in[:, :, 4 * gw:col_b].astype(_BF16))
    w_out16 = w_out.astype(_BF16)

    for layer in range(depth):
        last = layer == depth - 1
        mod = mod_all[layer]

        a = _norm(xx, norm1_w[layer], mod, lc, 0, 1)
        a2 = a.reshape(t, d)
        proj_a = _matmul(a2, w_a, layer, _BF16).reshape(bsz, lt, 4 * gw)
        proj_b = _matmul(a2, w_b, layer, _BF16).reshape(bsz, lt, 3 * naw)
        ab = _matmul(a2, w_ab, layer, _F32)[:, :4 * nh_a].reshape(bsz, lt, 2, 2, nhg_a, hg_a)

        q, k, v = _gprep(proj_a, conv_w[layer], lc, nh_a)
        lane_pad = GATE_LANES - 2 * hg_a
        ab_r = jnp.transpose(ab, (3, 4, 0, 1, 2, 5)).reshape(2, nhg_a, bsz, lt, 2 * hg_a)
        ab_r = jnp.pad(ab_r, ((0, 0),) * 4 + ((0, lane_pad),))
        zpad = jnp.zeros((2, nhg_a, 1, hg_a), _F32)
        zlane = jnp.zeros((2, nhg_a, 1, lane_pad), _F32)
        alog_r = jnp.concatenate([zpad, a_log[layer].astype(_F32).reshape(2, nhg_a, 1, hg_a), zlane], axis=-1)
        dtb_r = jnp.concatenate([zpad, dt_bias[layer].astype(_F32).reshape(2, nhg_a, 1, hg_a), zlane], axis=-1)
        o_fwd, o_bwd = _gdn(q, k, v, ab_r, alog_r, dtb_r, lc, hg_a)
        ya = _gdnout(o_fwd, o_bwd, proj_a, 3, gdn_norm_w[layer])

        tab = _na_bias_table(na_rpb[layer], rows)
        nb = naw // (hg_b * HEAD_DIM)
        yb = _na(proj_b, 0, nb, 2 * nb, tab, qk_norm_w[layer].astype(_F32), lc, nh_b, hg_b)

        xx = _outproj(ya, yb, w_out16, layer, xx, mod, lc, 2)

        f, logits_t = _norm(xx, norm2_w[layer], mod, lc, 3, 4, w_router_t, latent_only=last)
        rows_moe = f.shape[1]
        y0, y1, w01 = _moe(f.reshape(bsz * rows_moe, d), logits_t, b_router, w_gate, w_up, w_down, layer)
        xx = _combine(xx, y0.reshape(bsz, rows_moe, d), y1.reshape(bsz, rows_moe, d),
                      w01.reshape(bsz, rows_moe, 2), mod, lc, 5, last)
    return xx
```

```python
import functools
import math

import numpy as np
import jax
import jax.numpy as jnp
from jax import lax
from jax.experimental import pallas as pl
from jax.experimental.pallas import tpu as pltpu

HEAD_DIM = 128
GRID_W = 64
GDN_CHUNK = 64
SHORT_CONV = 5
NA_WIN_R = 8
NA_WIN_C = 16
N_EXPERT_GROUPS = 4
TOP_K = 2
N_MOD = 6
NORM_EPS = 1e-6
MASK_VALUE = -1e30

ROW_TILE = 256
MOE_TILE = 256
MOE_F_TILE = 512
MOE_D_TILE = 2048
MOE_GATHER_PARTS = 4
LANES = 128
GATE_LANES = 128
CONV_GAP = 8
GDN_HEADS_PER_STEP = 16
GDN_PACK = 4
NA_HEADS_PER_STEP = 8
VMEM_LIMIT = 56 * 1024 * 1024

_F32 = jnp.float32
_BF16 = jnp.bfloat16
_HI = lax.Precision.HIGHEST


def _cp(sem, vmem=VMEM_LIMIT, flags=None):
    return pltpu.CompilerParams(dimension_semantics=sem, vmem_limit_bytes=vmem, flags=flags)


def _pick(n, cands):
    for c in cands:
        if n % c == 0:
            return c
    raise ValueError(f"no tile for {n} in {cands}")


def _dot(a, b):
    return jnp.dot(a, b, preferred_element_type=_F32)


def _dot_nt(a, b, precision=None):
    return lax.dot_general(a, b, (((1,), (1,)), ((), ())), preferred_element_type=_F32, precision=precision)


def _dot_tn(a, b):
    return lax.dot_general(a, b, (((0,), (0,)), ((), ())), preferred_element_type=_F32)


def _sigmoid(x):
    return 1.0 / (1.0 + jnp.exp(-x))


def _silu(x):
    return x * _sigmoid(x)


def _ada_kernel(cc_ref, w_ref, b_ref, o_ref):
    a = _silu(cc_ref[...]).astype(_BF16)
    o_ref[0] = _dot(a, w_ref[0].astype(_BF16)) + b_ref[0]


def _ada(cc, w_ada, b_ada):
    nl, d, n = w_ada.shape
    mp = cc.shape[0]
    tn = _pick(n, (512, 256, 128))
    return pl.pallas_call(
        _ada_kernel,
        grid=(nl, n // tn),
        in_specs=[
            pl.BlockSpec((mp, d), lambda l, j: (0, 0)),
            pl.BlockSpec((1, d, tn), lambda l, j: (l, 0, j)),
            pl.BlockSpec((1, 1, tn), lambda l, j: (l, 0, j)),
        ],
        out_specs=pl.BlockSpec((1, mp, tn), lambda l, j: (l, 0, j)),
        out_shape=jax.ShapeDtypeStruct((nl, mp, n), _F32),
        compiler_params=_cp(("arbitrary", "arbitrary")),
        name="ada_mod",
    )(cc, w_ada, b_ada.reshape(nl, 1, n))


def _modnorm(x, nw, mod, k_shift, k_scale):
    ms = jnp.mean(x * x, axis=-1, keepdims=True)
    y = x * lax.rsqrt(ms + NORM_EPS) * nw
    return y * (1.0 + mod[k_scale:k_scale + 1, :]) + mod[k_shift:k_shift + 1, :]


def _norm_kernel(x_ref, nw_ref, mod_ref, o_ref, *, k_shift, k_scale):
    y = _modnorm(x_ref[0], nw_ref[...], mod_ref[0], k_shift, k_scale)
    o_ref[0] = y.astype(o_ref.dtype)


def _norm_router_kernel(x_ref, nw_ref, mod_ref, wr_ref, o_ref, lg_ref, *, k_shift, k_scale):
    y = _modnorm(x_ref[0], nw_ref[...], mod_ref[0], k_shift, k_scale)
    o_ref[0] = y.astype(o_ref.dtype)
    lg_ref[...] = _dot_nt(wr_ref[...], y, precision=_HI)


def _mod_row(bsz, n_ctx_tiles):
    return lambda b, i: (jnp.where(i < n_ctx_tiles, bsz, b), 0, 0)


def _norm(xx, nw, mod, lc, k_shift, k_scale, w_router_t=None, latent_only=False):
    bsz, lt, d = xx.shape
    tr = ROW_TILE
    skip = lc // tr if latent_only else 0
    nt = lt // tr - skip
    mrow = _mod_row(bsz, lc // tr)
    in_specs = [
        pl.BlockSpec((1, tr, d), lambda b, i: (b, i + skip, 0)),
        pl.BlockSpec((1, d), lambda b, i: (0, 0)),
        pl.BlockSpec((1, N_MOD, d), lambda b, i: mrow(b, i + skip)),
    ]
    a_spec = pl.BlockSpec((1, tr, d), lambda b, i: (b, i, 0))
    a_shape = jax.ShapeDtypeStruct((bsz, nt * tr, d), _BF16)
    if w_router_t is None:
        return pl.pallas_call(
            functools.partial(_norm_kernel, k_shift=k_shift, k_scale=k_scale),
            grid=(bsz, nt), in_specs=in_specs, out_specs=a_spec, out_shape=a_shape,
            compiler_params=_cp(("arbitrary", "arbitrary")), name="modnorm",
        )(xx, nw.reshape(1, d), mod)
    ne = w_router_t.shape[0]
    return pl.pallas_call(
        functools.partial(_norm_router_kernel, k_shift=k_shift, k_scale=k_scale),
        grid=(bsz, nt),
        in_specs=in_specs + [pl.BlockSpec((ne, d), lambda b, i: (0, 0))],
        out_specs=[a_spec, pl.BlockSpec((ne, tr), lambda b, i: (0, b * nt + i))],
        out_shape=[a_shape, jax.ShapeDtypeStruct((ne, bsz * nt * tr), _F32)],
        compiler_params=_cp(("arbitrary", "arbitrary")), name="modnorm_router",
    )(xx, nw.reshape(1, d), mod, w_router_t)


def _mm_kernel(a_ref, w_ref, o_ref):
    o_ref[...] = _dot(a_ref[...], w_ref[0]).astype(o_ref.dtype)


def _matmul(a, w, layer, n, out_dtype):
    m, k = a.shape
    tm = _pick(m, (1024, 512, 256, 128))
    tn = _pick(n, (1024, 512, 256, 128))
    return pl.pallas_call(
        _mm_kernel,
        grid=(n // tn, m // tm),
        in_specs=[pl.BlockSpec((tm, k), lambda j, i: (i, 0)),
                  pl.BlockSpec((1, k, tn), lambda j, i: (layer, 0, j))],
        out_specs=pl.BlockSpec((tm, tn), lambda j, i: (i, j)),
        out_shape=jax.ShapeDtypeStruct((m, n), out_dtype),
        compiler_params=_cp(("arbitrary", "arbitrary")), name="matmul",
    )(a, w)


def _conv_silu(pad_ref, x, w, lc):
    lt = x.shape[0]
    g = CONV_GAP
    pad = SHORT_CONV // 2
    zeros = jnp.zeros((g, x.shape[1]), _F32)
    pad_ref[0:g] = zeros
    pad_ref[g:g + lc] = x[0:lc]
    pad_ref[g + lc:2 * g + lc] = zeros
    pad_ref[2 * g + lc:2 * g + lt] = x[lc:lt]
    pad_ref[2 * g + lt:3 * g + lt] = zeros
    acc = None
    for j in range(SHORT_CONV):
        term = pad_ref[g + j - pad:2 * g + lt + j - pad] * w[j:j + 1, :]
        acc = term if acc is None else acc + term
    y = _silu(acc)
    return jnp.concatenate([y[0:lc], y[lc + g:lt + g]], axis=0)


def _l2n(t):
    return t * lax.rsqrt(jnp.sum(t * t, axis=-1, keepdims=True) + NORM_EPS)


def _gprep_kernel(q_ref, k_ref, v_ref, wq_ref, wk_ref, wv_ref, qo_ref, ko_ref, vo_ref, pad_ref, *, lc):
    q = _conv_silu(pad_ref, q_ref[0].astype(_F32), wq_ref[...], lc)
    qo_ref[0] = (_l2n(q) * (HEAD_DIM ** -0.5)).astype(qo_ref.dtype)
    k = _conv_silu(pad_ref, k_ref[0].astype(_F32), wk_ref[...], lc)
    ko_ref[0] = _l2n(k).astype(ko_ref.dtype)
    v = _conv_silu(pad_ref, v_ref[0].astype(_F32), wv_ref[...], lc)
    vo_ref[0] = v.astype(vo_ref.dtype)


def _gprep(proj, conv_w, lc, nh):
    bsz, lt, _ = proj.shape
    hd = HEAD_DIM

    def xs(off):
        return pl.BlockSpec((1, lt, hd), lambda b, h: (b, 0, off + h))

    def ws(off):
        return pl.BlockSpec((SHORT_CONV, hd), lambda b, h: (0, off + h))

    o_spec = pl.BlockSpec((1, lt, hd), lambda b, h: (b, 0, h))
    o_shape = jax.ShapeDtypeStruct((bsz, lt, nh * hd), _BF16)
    return pl.pallas_call(
        functools.partial(_gprep_kernel, lc=lc),
        grid=(bsz, nh),
        in_specs=[xs(0), xs(nh), xs(2 * nh), ws(0), ws(nh), ws(2 * nh)],
        out_specs=[o_spec, o_spec, o_spec],
        out_shape=[o_shape, o_shape, o_shape],
        scratch_shapes=[pltpu.VMEM((lt + 3 * CONV_GAP, hd), _F32)],
        compiler_params=_cp(("arbitrary", "arbitrary")), name="gdn_prep",
    )(proj, proj, proj, conv_w, conv_w, conv_w)


def _gdn_masks():
    c = GDN_CHUNK
    i = np.arange(c)[:, None]
    j = np.arange(c)[None, :]
    tri = np.stack([(i >= j), (i <= j)]).astype(np.float32)
    eye = np.eye(c, dtype=np.float32)
    lv = [(i // 2 == j // 2)]
    s = 2
    while s < c:
        lv.append((i // (2 * s) == j // (2 * s)) & (i // s != j // s))
        s *= 2
    return tri, eye, np.stack(lv).astype(np.float32)


def _gdn_kernel(qf_ref, kf_ref, vf_ref, qb_ref, kb_ref, vb_ref, abf_ref, abb_ref, al_ref, dtb_ref, tri_ref, eye_ref,
                lv_ref, of_ref, ob_ref, s_ref, *, hg, pk):
    c = GDN_CHUNK
    hd = HEAD_DIM
    t = pl.program_id(2)

    @pl.when(t == 0)
    def _():
        s_ref[...] = jnp.zeros_like(s_ref)

    nd = 2
    wl = pk * c
    q_refs, k_refs, v_refs, o_refs = (qf_ref, qb_ref), (kf_ref, kb_ref), (vf_ref, vb_ref), (of_ref, ob_ref)
    eye = eye_ref[...]
    tri_d = [tri_ref[dr] for dr in range(nd)]
    strict_d = [tri_d[dr] - eye for dr in range(nd)]
    beta_d, gc_d, gtot_d = [], [], []
    for dr, ab_ref in enumerate((abf_ref, abb_ref)):
        ab = ab_ref[0, 0, 0]
        z = ab + dtb_ref[dr, 0]
        softplus = jnp.maximum(z, 0.0) + jnp.log(1.0 + jnp.exp(-jnp.abs(z)))
        g = -jnp.exp(al_ref[dr, 0]) * softplus
        beta_d.append(_sigmoid(ab))
        gc_d.append(jnp.dot(tri_d[dr][:, :c], g, preferred_element_type=_F32, precision=_HI))
        gtot_d.append(jnp.sum(g, axis=0, keepdims=True))
    n_lv = lv_ref.shape[0]
    gpd = hg // pk
    heads = range(nd * hg)
    groups = range(nd * gpd)
    members = range(pk)
    sls = [slice(j * hd, (j + 1) * hd) for j in range(hg)]
    gsl = [slice(gi * pk * hd, (gi + 1) * pk * hd) for gi in range(gpd)]
    tri = [tri_d[gi // gpd] for gi in groups]
    strict = [strict_d[gi // gpd] for gi in groups]
    lane_blk = lax.broadcasted_iota(jnp.int32, (c, wl), 1) // c
    head_blk = lax.broadcasted_iota(jnp.int32, (c, pk * hd), 1) // hd

    def block_diag(x):
        return jnp.concatenate([jnp.where(lane_blk == a, x, 0.0) for a in members], axis=0).astype(_BF16)

    def per_member_cols(vals, width):
        return jnp.concatenate([jnp.broadcast_to(v, (c, width)) for v in vals], axis=1)

    b = [beta_d[j // hg][:, j % hg:j % hg + 1] for j in heads]
    gj = [gc_d[j // hg][:, hg + j % hg:hg + j % hg + 1] for j in heads]
    gt = [gtot_d[j // hg][:, hg + j % hg:hg + j % hg + 1] for j in heads]
    eg = [jnp.exp(gj[j]) for j in heads]
    k16 = [k_refs[gi // gpd][0, :, gsl[gi % gpd]] for gi in groups]
    q16 = [q_refs[gi // gpd][0, :, gsl[gi % gpd]] for gi in groups]
    kf = [k16[gi].astype(_F32) for gi in groups]
    kbeta = [kf[gi] * per_member_cols([b[gi * pk + a] for a in members], hd) for gi in groups]
    decay = []
    for gi in groups:
        gcol = jnp.broadcast_to(gj[gi * pk], (c, wl))
        for a in members[1:]:
            gcol = jnp.where(lane_blk == a, gj[gi * pk + a], gcol)
        grow = jnp.sum(gcol * eye, axis=0, keepdims=True)
        decay.append(jnp.exp(jnp.where(tri[gi] > 0.0, gcol - grow, -jnp.inf)))
    kq = []
    for gi in groups:
        k_bd = jnp.concatenate([jnp.where(head_blk == a, kf[gi], 0.0) for a in members], axis=0).astype(_BF16)
        kq.append(_dot_nt(jnp.concatenate([kbeta[gi].astype(_BF16), q16[gi]], axis=0), k_bd))
    lmat = [kq[gi][:c] * decay[gi] * strict[gi] for gi in groups]
    qk = [kq[gi][c:] * decay[gi] for gi in groups]
    dinv = [eye - lmat[gi] * lv_ref[0] for gi in groups]
    for lv in range(1, n_lv):
        wmat = [_dot((lmat[gi] * lv_ref[lv]).astype(_BF16), block_diag(dinv[gi])) for gi in groups]
        dinv = [dinv[gi] - _dot(dinv[gi].astype(_BF16), block_diag(wmat[gi])) for gi in groups]
    uw = []
    for gi in groups:
        vf = v_refs[gi // gpd][0, :, gsl[gi % gpd]].astype(_F32)
        rhs = jnp.concatenate(
            [jnp.concatenate([vf[:, a * hd:(a + 1) * hd] * b[gi * pk + a],
                              kbeta[gi][:, a * hd:(a + 1) * hd] * eg[gi * pk + a]], axis=1) for a in members],
            axis=0).astype(_BF16)
        uw.append(_dot(block_diag(dinv[gi]), rhs))
    s = [s_ref[j] for j in heads]
    s16 = [s[j].astype(_BF16) for j in heads]
    q_dec = [(q16[j // pk][:, (j % pk) * hd:(j % pk + 1) * hd].astype(_F32) * eg[j]).astype(_BF16) for j in heads]
    k_dec = [(kf[j // pk][:, (j % pk) * hd:(j % pk + 1) * hd] * jnp.exp(gt[j] - gj[j])).astype(_BF16) for j in heads]
    uw_h = [uw[j // pk][(j % pk) * c:(j % pk + 1) * c] for j in heads]
    ws = [_dot(jnp.concatenate([uw_h[j][:, hd:].astype(_BF16), q_dec[j]], axis=0), s16[j]) for j in heads]
    v16 = [(uw_h[j][:, :hd] - ws[j][:c]).astype(_BF16) for j in heads]
    intra = [_dot(block_diag(qk[gi]), jnp.concatenate([v16[gi * pk + a] for a in members], axis=0))
             for gi in groups]
    outs = [ws[j][c:] + intra[j // pk][(j % pk) * c:(j % pk + 1) * c] for j in heads]
    states = [s[j] * jnp.exp(gt[j]) + _dot_tn(k_dec[j], v16[j]) for j in heads]

    for j in heads:
        o_refs[j // hg][0, :, sls[j % hg]] = outs[j].astype(of_ref.dtype)
    for j in heads:
        s_ref[j] = states[j]


def _gdn(q, k, v, ab_r, alog_r, dtb_r, lc, hg):
    bsz, lt, gw = q.shape
    nh = gw // HEAD_DIM
    nhg = nh // hg
    c = GDN_CHUNK
    nc_c = lc // c
    nc = lt // c
    pk = min(GDN_PACK, hg)
    assert hg % pk == 0 and (pk * c) % LANES == 0
    tri, eye, lvm = (np.tile(m, pk) for m in _gdn_masks())

    def bwd(t):
        return jnp.where(t < nc_c, nc_c - 1 - t, nc - 1 + nc_c - t)

    bw = hg * HEAD_DIM
    xf_spec = pl.BlockSpec((1, c, bw), lambda b, g, t: (b, t, g))
    xb_spec = pl.BlockSpec((1, c, bw), lambda b, g, t: (b, bwd(t), g))
    gate_spec = pl.BlockSpec((2, 1, 1, GATE_LANES), lambda b, g, t: (0, g, 0, 0))
    o_shape = jax.ShapeDtypeStruct((bsz, lt, gw), _BF16)
    return pl.pallas_call(
        functools.partial(_gdn_kernel, hg=hg, pk=pk),
        grid=(bsz, nhg, nc),
        in_specs=[
            xf_spec, xf_spec, xf_spec, xb_spec, xb_spec, xb_spec,
            pl.BlockSpec((1, 1, 1, c, GATE_LANES), lambda b, g, t: (0, g, b, t, 0)),
            pl.BlockSpec((1, 1, 1, c, GATE_LANES), lambda b, g, t: (1, g, b, bwd(t), 0)),
            gate_spec, gate_spec,
            pl.BlockSpec((2, c, pk * c), lambda b, g, t: (0, 0, 0)),
            pl.BlockSpec((c, pk * c), lambda b, g, t: (0, 0)),
            pl.BlockSpec(lvm.shape, lambda b, g, t: (0, 0, 0)),
        ],
        out_specs=[xf_spec, xb_spec],
        out_shape=[o_shape, o_shape],
        scratch_shapes=[pltpu.VMEM((2 * hg, HEAD_DIM, HEAD_DIM), _F32)],
        compiler_params=_cp(("arbitrary", "arbitrary", "arbitrary")), name="gdn_scan",
    )(q, k, v, q, k, v, ab_r, ab_r, alog_r, dtb_r, jnp.asarray(tri), jnp.asarray(eye), jnp.asarray(lvm))


def _gdnout_kernel(of_ref, ob_ref, z_ref, nw_ref, y_ref, *, nh):
    hd = HEAD_DIM
    for h in range(nh):
        sl = slice(h * hd, (h + 1) * hd)
        o = of_ref[0, :, sl].astype(_F32) + ob_ref[0, :, sl].astype(_F32)
        ms = jnp.mean(o * o, axis=-1, keepdims=True)
        y = o * lax.rsqrt(ms + NORM_EPS) * nw_ref[...]
        y_ref[0, :, sl] = (y * _silu(z_ref[0, :, sl].astype(_F32))).astype(y_ref.dtype)


def _gdnout(o_fwd, o_bwd, proj, z_off_blocks, norm_w):
    bsz, lt, gw = o_fwd.shape
    nh = gw // HEAD_DIM
    tr = ROW_TILE
    return pl.pallas_call(
        functools.partial(_gdnout_kernel, nh=nh),
        grid=(bsz, lt // tr),
        in_specs=[
            pl.BlockSpec((1, tr, gw), lambda b, i: (b, i, 0)),
            pl.BlockSpec((1, tr, gw), lambda b, i: (b, i, 0)),
            pl.BlockSpec((1, tr, gw), lambda b, i: (b, i, z_off_blocks)),
            pl.BlockSpec((1, HEAD_DIM), lambda b, i: (0, 0)),
        ],
        out_specs=pl.BlockSpec((1, tr, gw), lambda b, i: (b, i, 0)),
        out_shape=jax.ShapeDtypeStruct((bsz, lt, gw), _BF16),
        compiler_params=_cp(("arbitrary", "arbitrary")), name="gdn_out",
    )(o_fwd, o_bwd, proj, norm_w.reshape(1, HEAD_DIM))


def _na_bias_table(rpb, rows):
    wc = NA_WIN_C
    qc = np.arange(GRID_W)[:, None]
    kc = np.arange(GRID_W)[None, :]
    win_start = np.clip(qc - wc // 2, 0, GRID_W - wc)
    valid = (kc >= win_start) & (kc < win_start + wc)
    off = np.clip(kc - qc + wc - 1, 0, 2 * wc - 2)
    ro = np.arange(NA_WIN_R)[:, None] + np.arange(NA_WIN_R)[None, :]
    tab = rpb.astype(_F32)[:, ro][:, :, :, off]
    tab = jnp.where(valid[None, None, None], tab, MASK_VALUE)
    tab = jnp.transpose(tab, (0, 1, 3, 2, 4))
    return tab.reshape(rpb.shape[0], NA_WIN_R, GRID_W, NA_WIN_R * GRID_W)


def _rms_head(t, w):
    return t * lax.rsqrt(jnp.mean(t * t, axis=-1, keepdims=True) + NORM_EPS) * w


def _na_kernel(q_ref, k_ref, v_ref, tab_ref, nw_ref, o_ref, kn_ref, *, hg, lc, rows):
    hd = HEAD_DIM
    w = GRID_W
    nloc = NA_WIN_R * w
    ncc = lc // w
    r = pl.program_id(2)

    @pl.when(r == 0)
    def _():
        for j in range(hg):
            sl = slice(j * hd, (j + 1) * hd)
            kn_ref[:, sl] = _rms_head(k_ref[0, :, sl].astype(_F32), nw_ref[1:2, :]).astype(kn_ref.dtype)

    rr = r - ncc
    sr = jnp.clip(rr - NA_WIN_R // 2, 0, rows - NA_WIN_R)
    start = pl.multiple_of(lc + sr * w, w)
    ro = jnp.clip(sr - rr + NA_WIN_R - 1, 0, NA_WIN_R - 1)

    heads = range(hg)
    sls = [slice(j * hd, (j + 1) * hd) for j in heads]

    def normed_q():
        return [(_rms_head(q_ref[0, :, sls[j]].astype(_F32), nw_ref[0:1, :]) * (hd ** -0.5)).astype(_BF16)
                for j in heads]

    def store(outs):
        for j in heads:
            o_ref[0, :, sls[j]] = outs[j].astype(o_ref.dtype)

    def lane_tile_reduce(op, *arrays):
        tiles = [a[:, i:i + LANES] for a in arrays for i in range(0, a.shape[1], LANES)]
        acc = tiles[0]
        for tl in tiles[1:]:
            acc = op(acc, tl)
        return acc

    @pl.when(r < ncc)
    def _():
        qn = normed_q()
        s_ctx = [_dot_nt(qn[j], kn_ref[0:lc, sls[j]]) for j in heads]
        p = [jnp.exp(s_ctx[j] - jnp.max(s_ctx[j], axis=-1, keepdims=True)) for j in heads]
        den = [jnp.sum(p[j], axis=-1, keepdims=True) for j in heads]
        store([_dot(p[j].astype(_BF16), v_ref[0, 0:lc, sls[j]]) / den[j] for j in heads])

    @pl.when(r >= ncc)
    def _():
        qn = normed_q()
        s_ctx = [_dot_nt(qn[j], kn_ref[0:lc, sls[j]]) for j in heads]
        s_loc = [_dot_nt(qn[j], kn_ref[pl.ds(start, nloc), sls[j]]) + tab_ref[j, ro] for j in heads]
        m = [jnp.max(lane_tile_reduce(jnp.maximum, s_loc[j], s_ctx[j]), axis=-1, keepdims=True) for j in heads]
        p_loc = [jnp.exp(s_loc[j] - m[j]) for j in heads]
        p_ctx = [jnp.exp(s_ctx[j] - m[j]) for j in heads]
        den = [jnp.sum(lane_tile_reduce(jnp.add, p_loc[j], p_ctx[j]), axis=-1, keepdims=True) for j in heads]
        o = [_dot(p_loc[j].astype(_BF16), v_ref[0, pl.ds(start, nloc), sls[j]])
             + _dot(p_ctx[j].astype(_BF16), v_ref[0, 0:lc, sls[j]]) for j in heads]
        store([o[j] / den[j] for j in heads])


def _na(proj, q_off, k_off, v_off, tab, qk_norm_w, lc, nh, hg):
    bsz, lt, _ = proj.shape
    rows = (lt - lc) // GRID_W
    nhg = nh // hg
    bw = hg * HEAD_DIM
    return pl.pallas_call(
        functools.partial(_na_kernel, hg=hg, lc=lc, rows=rows),
        grid=(bsz, nhg, lt // GRID_W),
        in_specs=[
            pl.BlockSpec((1, GRID_W, bw), lambda b, g, r: (b, r, q_off + g)),
            pl.BlockSpec((1, lt, bw), lambda b, g, r: (b, 0, k_off + g)),
            pl.BlockSpec((1, lt, bw), lambda b, g, r: (b, 0, v_off + g)),
            pl.BlockSpec((hg,) + tab.shape[1:], lambda b, g, r: (g, 0, 0, 0)),
            pl.BlockSpec((2, HEAD_DIM), lambda b, g, r: (0, 0)),
        ],
        out_specs=pl.BlockSpec((1, GRID_W, bw), lambda b, g, r: (b, r, g)),
        out_shape=jax.ShapeDtypeStruct((bsz, lt, nh * HEAD_DIM), _BF16),
        scratch_shapes=[pltpu.VMEM((lt, bw), _BF16)],
        compiler_params=_cp(("arbitrary", "arbitrary", "arbitrary")), name="natten",
    )(proj, proj, proj, tab, qk_norm_w)


def _outproj_kernel(ya_ref, yb_ref, wa_ref, wb_ref, x_ref, ml_ref, mc_ref, o_ref, *, lc, tiles_per_sample, k_gate):
    tm = x_ref.shape[0]
    acc = _dot(ya_ref[...], wa_ref[0]) + _dot(yb_ref[...], wb_ref[0])
    row = (pl.program_id(0) % tiles_per_sample) * tm + lax.broadcasted_iota(jnp.int32, acc.shape, 0)
    gate = jnp.where(row < lc, mc_ref[0, k_gate:k_gate + 1, :], ml_ref[0, k_gate:k_gate + 1, :])
    o_ref[...] = x_ref[...] + gate * acc


def _outproj(ya, yb, w_out, layer, xx, mod, lc, k_gate):
    bsz, lt, d = xx.shape
    t = bsz * lt
    ga = ya.shape[-1]
    gb = yb.shape[-1]
    tm = _pick(lt, (1152, 768, 576, 384, 256, 128))
    tn = _pick(d, (512, 256, 128))
    tps = lt // tm
    out = pl.pallas_call(
        functools.partial(_outproj_kernel, lc=lc, tiles_per_sample=tps, k_gate=k_gate),
        grid=(t // tm, d // tn),
        in_specs=[
            pl.BlockSpec((tm, ga), lambda i, j: (i, 0)),
            pl.BlockSpec((tm, gb), lambda i, j: (i, 0)),
            pl.BlockSpec((1, ga, tn), lambda i, j: (layer, 0, j)),
            pl.BlockSpec((1, gb, tn), lambda i, j: (layer, ga // gb, j)),
            pl.BlockSpec((tm, tn), lambda i, j: (i, j)),
            pl.BlockSpec((1, N_MOD, tn), lambda i, j: (i // tps, 0, j)),
            pl.BlockSpec((1, N_MOD, tn), lambda i, j: (bsz, 0, j)),
        ],
        out_specs=pl.BlockSpec((tm, tn), lambda i, j: (i, j)),
        out_shape=jax.ShapeDtypeStruct((t, d), _F32),
        compiler_params=_cp(("arbitrary", "arbitrary")), name="out_proj",
    )(ya.reshape(t, ga), yb.reshape(t, gb), w_out, w_out, xx.reshape(t, d), mod, mod)
    return out.reshape(bsz, lt, d)


def _route_kernel(lg_ref, br_ref, su_ref, oi_ref, ow_ref, cnt_ref, carry_ref, *, ne):
    epg = ne // N_EXPERT_GROUPS
    tq = lg_ref.shape[1]

    @pl.when(pl.program_id(0) == 0)
    def _():
        carry_ref[...] = jnp.zeros_like(carry_ref)

    sc = _sigmoid(lg_ref[...])
    sel = sc + br_ref[...]
    s = [sel[e:e + 1, :] for e in range(ne)]

    def top2(vals):
        a, b, c_, d_ = vals
        hi1, lo1 = jnp.maximum(a, b), jnp.minimum(a, b)
        hi2, lo2 = jnp.maximum(c_, d_), jnp.minimum(c_, d_)
        return jnp.maximum(hi1, hi2) + jnp.maximum(jnp.maximum(lo1, lo2), jnp.minimum(hi1, hi2))

    gs = [top2(s[g * epg:(g + 1) * epg]) for g in range(N_EXPERT_GROUPS)]
    best = gs[0]
    gsel = jnp.zeros_like(best, dtype=jnp.int32)
    for g in range(1, N_EXPERT_GROUPS):
        better = gs[g] > best
        gsel = jnp.where(better, g, gsel)
        best = jnp.maximum(best, gs[g])
    ing = []
    for j in range(epg):
        v = s[j]
        for g in range(1, N_EXPERT_GROUPS):
            v = jnp.where(gsel == g, s[g * epg + j], v)
        ing.append(v)
    b1 = ing[0]
    i1 = jnp.zeros_like(gsel)
    for j in range(1, epg):
        better = ing[j] > b1
        i1 = jnp.where(better, j, i1)
        b1 = jnp.maximum(b1, ing[j])
    b2 = jnp.full_like(b1, -jnp.inf)
    i2 = jnp.zeros_like(gsel)
    for j in range(epg):
        better = (ing[j] > b2) & (i1 != j)
        i2 = jnp.where(better, j, i2)
        b2 = jnp.where(better, ing[j], b2)
    e1 = gsel * epg + i1
    e2 = gsel * epg + i2

    eio = lax.broadcasted_iota(jnp.int32, (ne, tq), 0)
    oh1 = eio == e1
    oh2 = eio == e2
    w1 = jnp.sum(jnp.where(oh1, sc, 0.0), axis=0, keepdims=True)
    w2 = jnp.sum(jnp.where(oh2, sc, 0.0), axis=0, keepdims=True)
    wsum = w1 + w2
    oh = jnp.where(oh1, 1.0, 0.0) + jnp.where(oh2, 1.0, 0.0)
    rank = _dot(oh.astype(_BF16), su_ref[...]) + carry_ref[:, 0:1]
    r1 = jnp.sum(jnp.where(oh1, rank, 0.0), axis=0, keepdims=True)
    r2 = jnp.sum(jnp.where(oh2, rank, 0.0), axis=0, keepdims=True)
    carry = carry_ref[...] + jnp.sum(oh, axis=1, keepdims=True)
    carry_ref[...] = carry
    cnt_ref[...] = carry

    sub = lax.broadcasted_iota(jnp.int32, (8, tq), 0)
    oi = jnp.where(sub == 0, e1, jnp.where(sub == 1, e2,
         jnp.where(sub == 2, r1.astype(jnp.int32), jnp.where(sub == 3, r2.astype(jnp.int32), 0))))
    oi_ref[...] = oi
    ow_ref[...] = jnp.where(sub == 0, w1 / wsum, jnp.where(sub == 1, w2 / wsum, 0.0))


def _route(logits_t, b_router):
    ne, t = logits_t.shape
    tq = ROW_TILE
    su = np.triu(np.ones((tq, tq), np.float32), 1)
    return pl.pallas_call(
        functools.partial(_route_kernel, ne=ne),
        grid=(t // tq,),
        in_specs=[pl.BlockSpec((ne, tq), lambda i: (0, i)),
                  pl.BlockSpec((ne, 1), lambda i: (0, 0)),
                  pl.BlockSpec((tq, tq), lambda i: (0, 0))],
        out_specs=[pl.BlockSpec((8, tq), lambda i: (0, i)),
                   pl.BlockSpec((8, tq), lambda i: (0, i)),
                   pl.BlockSpec((ne, 128), lambda i: (0, 0))],
        out_shape=[jax.ShapeDtypeStruct((8, t), jnp.int32),
                   jax.ShapeDtypeStruct((8, t), _F32),
                   jax.ShapeDtypeStruct((ne, 128), _F32)],
        scratch_shapes=[pltpu.VMEM((ne, 128), _F32)],
        compiler_params=_cp(("arbitrary",)), name="router",
    )(logits_t, b_router.reshape(ne, 1).astype(_F32), jnp.asarray(su, _BF16))


def _new_weight_block(te_ref, i):
    return (i == 0) | (te_ref[i] != te_ref[jnp.maximum(i - 1, 0)])


def _moe_up_kernel(te_ref, x_ref, wg_ref, wu_ref, h_ref, wg16_ref, wu16_ref):
    i = pl.program_id(1)
    n_valid = te_ref[te_ref.shape[0] - 1]

    @pl.when(_new_weight_block(te_ref, i))
    def _():
        wg16_ref[...] = wg_ref[0, 0].astype(_BF16)
        wu16_ref[...] = wu_ref[0, 0].astype(_BF16)

    @pl.when(i < n_valid)
    def _():
        x = x_ref[...]
        gt = _dot(x, wg16_ref[...])
        up = _dot(x, wu16_ref[...])
        h_ref[...] = (_silu(gt) * up).astype(h_ref.dtype)

    @pl.when(i >= n_valid)
    def _():
        h_ref[...] = jnp.zeros_like(h_ref)


def _moe_down_kernel(te_ref, *refs, n_parts, part_tiles):
    h_refs = refs[:n_parts]
    wd_ref, o_ref, wd16_ref = refs[n_parts:]
    i = pl.program_id(1)
    n_valid = te_ref[te_ref.shape[0] - 1]

    @pl.when(_new_weight_block(te_ref, i))
    def _():
        wd16_ref[...] = wd_ref[0, 0].astype(_BF16)

    for part in range(n_parts):
        @pl.when((i >= part * part_tiles) & (i < jnp.minimum(n_valid, (part + 1) * part_tiles)))
        def _(h_ref=h_refs[part]):
            o_ref[...] = _dot(h_ref[...], wd16_ref[...]).astype(o_ref.dtype)

    @pl.when(i >= n_valid)
    def _():
        o_ref[...] = jnp.zeros_like(o_ref)


def _moe_experts(parts, tile_e, w_gate, w_up, w_down, layer):
    d = parts[0][0].shape[1]
    f = w_gate.shape[-1]
    tm = MOE_TILE
    tf = _pick(f, (MOE_F_TILE, 256, 128))
    td = _pick(d, (MOE_D_TILE, 1024, 512, 256, 128))
    hids = []
    for xs, te_half in parts:
        nth = xs.shape[0] // tm
        hids.append(pl.pallas_call(
            _moe_up_kernel,
            grid_spec=pltpu.PrefetchScalarGridSpec(
                num_scalar_prefetch=1, grid=(f // tf, nth),
                in_specs=[pl.BlockSpec((tm, d), lambda j, i, te: (i, 0)),
                          pl.BlockSpec((1, 1, d, tf), lambda j, i, te: (layer, te[i], 0, j)),
                          pl.BlockSpec((1, 1, d, tf), lambda j, i, te: (layer, te[i], 0, j))],
                out_specs=pl.BlockSpec((tm, tf), lambda j, i, te: (i, j)),
                scratch_shapes=[pltpu.VMEM((d, tf), _BF16), pltpu.VMEM((d, tf), _BF16)]),
            out_shape=jax.ShapeDtypeStruct((nth * tm, f), _BF16),
            compiler_params=_cp(("arbitrary", "arbitrary")), name="moe_up",
        )(te_half, xs, w_gate, w_up))
    n_parts = len(hids)
    part_tiles = hids[0].shape[0] // tm
    nt = n_parts * part_tiles

    def part_spec(part):
        return pl.BlockSpec((tm, f), lambda j, i, te: (jnp.clip(i - part * part_tiles, 0, part_tiles - 1), 0))

    return pl.pallas_call(
        functools.partial(_moe_down_kernel, n_parts=n_parts, part_tiles=part_tiles),
        grid_spec=pltpu.PrefetchScalarGridSpec(
            num_scalar_prefetch=1, grid=(d // td, nt),
            in_specs=[part_spec(part) for part in range(n_parts)]
                     + [pl.BlockSpec((1, 1, f, td), lambda j, i, te: (layer, te[i], 0, j))],
            out_specs=pl.BlockSpec((tm, td), lambda j, i, te: (i, j)),
            scratch_shapes=[pltpu.VMEM((f, td), _BF16)]),
        out_shape=jax.ShapeDtypeStruct((nt * tm, d), _BF16),
        compiler_params=_cp(("arbitrary", "arbitrary")), name="moe_down",
    )(tile_e, *hids, w_down)


def _comb_kernel(x_ref, y0_ref, y1_ref, w_ref, mod_ref, o_ref, *, k_gate):
    y = y0_ref[0].astype(_F32) * w_ref[0, :, 0:1] + y1_ref[0].astype(_F32) * w_ref[0, :, 1:2]
    o_ref[0] = x_ref[0] + mod_ref[0, k_gate:k_gate + 1, :] * y


def _combine(xx, y0, y1, w01, mod, lc, k_gate, latent_only):
    bsz, lt, d = xx.shape
    tr = ROW_TILE
    skip = lc // tr if latent_only else 0
    nt = lt // tr - skip
    row = lambda b, i: (b, i + skip, 0)
    yrow = lambda b, i: (b, i, 0)
    mrow = _mod_row(bsz, lc // tr)
    return pl.pallas_call(
        functools.partial(_comb_kernel, k_gate=k_gate),
        grid=(bsz, nt),
        in_specs=[pl.BlockSpec((1, tr, d), row), pl.BlockSpec((1, tr, d), yrow), pl.BlockSpec((1, tr, d), yrow),
                  pl.BlockSpec((1, tr, 2), yrow),
                  pl.BlockSpec((1, N_MOD, d), lambda b, i: mrow(b, i + skip))],
        out_specs=pl.BlockSpec((1, tr, d), lambda b, i: (b, i, 0)),
        out_shape=jax.ShapeDtypeStruct((bsz, nt * tr, d), _F32),
        compiler_params=_cp(("arbitrary", "arbitrary")), name="moe_combine",
    )(xx, y0, y1, w01, mod)


def _moe(f, logits_t, b_router, w_gate, w_up, w_down, layer):
    t, d = f.shape
    ne = w_gate.shape[1]
    tm = MOE_TILE
    oi, ow, cnt = _route(logits_t, b_router)
    e01 = oi[0:2]
    r01 = oi[2:4]
    counts = cnt[:, 0].astype(jnp.int32)
    tiles = (counts + tm - 1) // tm
    tile_end = jnp.cumsum(tiles)
    off = (tile_end - tiles) * tm
    first = jnp.sum(jnp.where(e01[None] == jnp.arange(ne, dtype=jnp.int32)[:, None, None],
                              off[:, None, None], 0), axis=0)
    pos = first + r01
    nt = (t * TOP_K) // tm + ne
    p = nt * tm
    tok = jnp.broadcast_to(jnp.arange(t, dtype=jnp.int32)[None, :], (TOP_K, t))
    src = jnp.zeros((p,), jnp.int32).at[pos.reshape(-1)].set(tok.reshape(-1))
    n_valid = tile_end[-1]
    tile_idx = jnp.arange(nt, dtype=jnp.int32)
    tile_e = jnp.sum((tile_end[None, :] <= jnp.minimum(tile_idx, n_valid - 1)[:, None]).astype(jnp.int32), axis=1)
    assert nt % MOE_GATHER_PARTS == 0
    part_tiles = nt // MOE_GATHER_PARTS
    parts = []
    for lo in range(0, nt, part_tiles):
        hi = lo + part_tiles
        n_val = jnp.clip(n_valid - lo, 0, part_tiles).astype(jnp.int32)
        parts.append((f.at[src[lo * tm:hi * tm]].get(mode="promise_in_bounds"),
                      jnp.concatenate([tile_e[lo:hi], n_val[None]])))
    tile_e = jnp.concatenate([tile_e, n_valid[None].astype(jnp.int32)])
    out = _moe_experts(parts, tile_e, w_gate, w_up, w_down, layer)
    y0 = out.at[pos[0]].get(mode="promise_in_bounds")
    y1 = out.at[pos[1]].get(mode="promise_in_bounds")
    return y0, y1, jnp.transpose(ow[0:2])


def kernel(x, c, ctx, c_ctx, norm1_w, norm2_w, w_ada, b_ada, w_in, conv_w, a_log, dt_bias, gdn_norm_w,
           qk_norm_w, na_rpb, w_out, w_router, b_router, w_gate, w_up, w_down):
    bsz, s_len, d = x.shape
    lc = ctx.shape[1]
    lt = lc + s_len
    t = bsz * lt
    depth = w_ada.shape[0]
    gw = d // 2
    naw = d - gw
    nh_a = gw // HEAD_DIM
    nh_b = naw // HEAD_DIM
    rows = s_len // GRID_W
    assert s_len % GRID_W == 0 and rows >= NA_WIN_R
    assert lc % ROW_TILE == 0 and s_len % ROW_TILE == 0
    hg_a = min(GDN_HEADS_PER_STEP, nh_a)
    hg_b = min(NA_HEADS_PER_STEP, nh_b)
    nhg_a = nh_a // hg_a

    xx = jnp.concatenate([ctx, x], axis=1)
    mp = -(-(bsz + 1) // 8) * 8
    cc = jnp.zeros((mp, d), _F32).at[:bsz].set(c).at[bsz].set(c_ctx)
    mod_all = _ada(cc, w_ada, b_ada).reshape(depth, mp, N_MOD, d)
    w_router_t = jnp.transpose(w_router).astype(_F32)

    col_b = 4 * gw + 4 * nh_a
    w_in16 = w_in.astype(_BF16)
    w_b = w_in16[:, :, col_b:]
    w_ab = jnp.zeros((depth, d, LANES), _BF16).at[:, :, :4 * nh_a].set(w_in16[:, :, 4 * gw:col_b])
    w_out16 = w_out.astype(_BF16)

    for layer in range(depth):
        last = layer == depth - 1
        mod = mod_all[layer]

        a = _norm(xx, norm1_w[layer], mod, lc, 0, 1)
        a2 = a.reshape(t, d)
        proj_a = _matmul(a2, w_in16, layer, 4 * gw, _BF16).reshape(bsz, lt, 4 * gw)
        proj_b = _matmul(a2, w_b, layer, 3 * naw, _BF16).reshape(bsz, lt, 3 * naw)
        ab = _matmul(a2, w_ab, layer, LANES, _F32)[:, :4 * nh_a].reshape(bsz, lt, 2, 2, nhg_a, hg_a)

        q, k, v = _gprep(proj_a, conv_w[layer], lc, nh_a)
        lane_pad = GATE_LANES - 2 * hg_a
        ab_r = jnp.transpose(ab, (3, 4, 0, 1, 2, 5)).reshape(2, nhg_a, bsz, lt, 2 * hg_a)
        ab_r = jnp.pad(ab_r, ((0, 0),) * 4 + ((0, lane_pad),))
        zpad = jnp.zeros((2, nhg_a, 1, hg_a), _F32)
        zlane = jnp.zeros((2, nhg_a, 1, lane_pad), _F32)
        alog_r = jnp.concatenate([zpad, a_log[layer].astype(_F32).reshape(2, nhg_a, 1, hg_a), zlane], axis=-1)
        dtb_r = jnp.concatenate([zpad, dt_bias[layer].astype(_F32).reshape(2, nhg_a, 1, hg_a), zlane], axis=-1)
        o_fwd, o_bwd = _gdn(q, k, v, ab_r, alog_r, dtb_r, lc, hg_a)
        ya = _gdnout(o_fwd, o_bwd, proj_a, 3, gdn_norm_w[layer])

        tab = _na_bias_table(na_rpb[layer], rows)
        nb = naw // (hg_b * HEAD_DIM)
        yb = _na(proj_b, 0, nb, 2 * nb, tab, qk_norm_w[layer].astype(_F32), lc, nh_b, hg_b)

        xx = _outproj(ya, yb, w_out16, layer, xx, mod, lc, 2)

        f, logits_t = _norm(xx, norm2_w[layer], mod, lc, 3, 4, w_router_t, latent_only=last)
        rows_moe = f.shape[1]
        y0, y1, w01 = _moe(f.reshape(bsz * rows_moe, d), logits_t, b_router, w_gate, w_up, w_down, layer)
        xx = _combine(xx, y0.reshape(bsz, rows_moe, d), y1.reshape(bsz, rows_moe, d),
                      w01.reshape(bsz, rows_moe, 2), mod, lc, 5, last)
    return xx
```

```python
import functools
import math

import numpy as np
import jax
import jax.numpy as jnp
from jax import lax
from jax.experimental import pallas as pl
from jax.experimental.pallas import tpu as pltpu

HEAD_DIM = 128
GRID_W = 64
GDN_CHUNK = 64
SHORT_CONV = 5
NA_WIN_R = 8
NA_WIN_C = 16
N_EXPERT_GROUPS = 4
TOP_K = 2
N_MOD = 6
NORM_EPS = 1e-6
MASK_VALUE = -1e30

ROW_TILE = 256
MOE_TILE = 256
MOE_F_TILE = 512
MOE_D_TILE = 4096
MOE_GATHER_PARTS = 4
LANES = 128
GATE_LANES = 128
CONV_GAP = 8
GDN_HEADS_PER_STEP = 16
GDN_PACK = 4
NA_HEADS_PER_STEP = 8
VMEM_LIMIT = 56 * 1024 * 1024

_F32 = jnp.float32
_BF16 = jnp.bfloat16
_HI = lax.Precision.HIGHEST


def _cp(sem, vmem=VMEM_LIMIT, flags=None):
    return pltpu.CompilerParams(dimension_semantics=sem, vmem_limit_bytes=vmem, flags=flags)


def _pick(n, cands):
    for c in cands:
        if n % c == 0:
            return c
    raise ValueError(f"no tile for {n} in {cands}")


def _dot(a, b):
    return jnp.dot(a, b, preferred_element_type=_F32)


def _dot_nt(a, b, precision=None):
    return lax.dot_general(a, b, (((1,), (1,)), ((), ())), preferred_element_type=_F32, precision=precision)


def _dot_tn(a, b):
    return lax.dot_general(a, b, (((0,), (0,)), ((), ())), preferred_element_type=_F32)


def _sigmoid(x):
    return 1.0 / (1.0 + jnp.exp(-x))


def _silu(x):
    return x * _sigmoid(x)


def _ada_kernel(cc_ref, w_ref, b_ref, o_ref):
    a = _silu(cc_ref[...]).astype(_BF16)
    o_ref[0] = _dot(a, w_ref[0].astype(_BF16)) + b_ref[0]


def _ada(cc, w_ada, b_ada):
    nl, d, n = w_ada.shape
    mp = cc.shape[0]
    tn = _pick(n, (512, 256, 128))
    return pl.pallas_call(
        _ada_kernel,
        grid=(nl, n // tn),
        in_specs=[
            pl.BlockSpec((mp, d), lambda l, j: (0, 0)),
            pl.BlockSpec((1, d, tn), lambda l, j: (l, 0, j)),
            pl.BlockSpec((1, 1, tn), lambda l, j: (l, 0, j)),
        ],
        out_specs=pl.BlockSpec((1, mp, tn), lambda l, j: (l, 0, j)),
        out_shape=jax.ShapeDtypeStruct((nl, mp, n), _F32),
        compiler_params=_cp(("arbitrary", "arbitrary")),
        name="ada_mod",
    )(cc, w_ada, b_ada.reshape(nl, 1, n))


def _modnorm(x, nw, mod, k_shift, k_scale):
    ms = jnp.mean(x * x, axis=-1, keepdims=True)
    y = x * lax.rsqrt(ms + NORM_EPS) * nw
    return y * (1.0 + mod[k_scale:k_scale + 1, :]) + mod[k_shift:k_shift + 1, :]


def _norm_kernel(x_ref, nw_ref, mod_ref, o_ref, *, k_shift, k_scale):
    y = _modnorm(x_ref[0], nw_ref[...], mod_ref[0], k_shift, k_scale)
    o_ref[0] = y.astype(o_ref.dtype)


def _norm_router_kernel(x_ref, nw_ref, mod_ref, wr_ref, o_ref, lg_ref, *, k_shift, k_scale):
    y = _modnorm(x_ref[0], nw_ref[...], mod_ref[0], k_shift, k_scale)
    o_ref[0] = y.astype(o_ref.dtype)
    lg_ref[...] = _dot_nt(wr_ref[...], y, precision=_HI)


def _mod_row(bsz, n_ctx_tiles):
    return lambda b, i: (jnp.where(i < n_ctx_tiles, bsz, b), 0, 0)


def _norm(xx, nw, mod, lc, k_shift, k_scale, w_router_t=None, latent_only=False):
    bsz, lt, d = xx.shape
    tr = ROW_TILE
    skip = lc // tr if latent_only else 0
    nt = lt // tr - skip
    mrow = _mod_row(bsz, lc // tr)
    in_specs = [
        pl.BlockSpec((1, tr, d), lambda b, i: (b, i + skip, 0)),
        pl.BlockSpec((1, d), lambda b, i: (0, 0)),
        pl.BlockSpec((1, N_MOD, d), lambda b, i: mrow(b, i + skip)),
    ]
    a_spec = pl.BlockSpec((1, tr, d), lambda b, i: (b, i, 0))
    a_shape = jax.ShapeDtypeStruct((bsz, nt * tr, d), _BF16)
    if w_router_t is None:
        return pl.pallas_call(
            functools.partial(_norm_kernel, k_shift=k_shift, k_scale=k_scale),
            grid=(bsz, nt), in_specs=in_specs, out_specs=a_spec, out_shape=a_shape,
            compiler_params=_cp(("arbitrary", "arbitrary")), name="modnorm",
        )(xx, nw.reshape(1, d), mod)
    ne = w_router_t.shape[0]
    return pl.pallas_call(
        functools.partial(_norm_router_kernel, k_shift=k_shift, k_scale=k_scale),
        grid=(bsz, nt),
        in_specs=in_specs + [pl.BlockSpec((ne, d), lambda b, i: (0, 0))],
        out_specs=[a_spec, pl.BlockSpec((ne, tr), lambda b, i: (0, b * nt + i))],
        out_shape=[a_shape, jax.ShapeDtypeStruct((ne, bsz * nt * tr), _F32)],
        compiler_params=_cp(("arbitrary", "arbitrary")), name="modnorm_router",
    )(xx, nw.reshape(1, d), mod, w_router_t)


def _mm_kernel(a_ref, w_ref, o_ref):
    o_ref[...] = _dot(a_ref[...], w_ref[0]).astype(o_ref.dtype)


def _matmul(a, w, layer, n, out_dtype):
    m, k = a.shape
    tm = _pick(m, (1024, 512, 256, 128))
    tn = _pick(n, (1024, 512, 256, 128))
    return pl.pallas_call(
        _mm_kernel,
        grid=(n // tn, m // tm),
        in_specs=[pl.BlockSpec((tm, k), lambda j, i: (i, 0)),
                  pl.BlockSpec((1, k, tn), lambda j, i: (layer, 0, j))],
        out_specs=pl.BlockSpec((tm, tn), lambda j, i: (i, j)),
        out_shape=jax.ShapeDtypeStruct((m, n), out_dtype),
        compiler_params=_cp(("arbitrary", "arbitrary")), name="matmul",
    )(a, w)


def _conv_silu(pad_ref, x, w, lc):
    lt = x.shape[0]
    g = CONV_GAP
    pad = SHORT_CONV // 2
    zeros = jnp.zeros((g, x.shape[1]), _F32)
    pad_ref[0:g] = zeros
    pad_ref[g:g + lc] = x[0:lc]
    pad_ref[g + lc:2 * g + lc] = zeros
    pad_ref[2 * g + lc:2 * g + lt] = x[lc:lt]
    pad_ref[2 * g + lt:3 * g + lt] = zeros
    acc = None
    for j in range(SHORT_CONV):
        term = pad_ref[g + j - pad:2 * g + lt + j - pad] * w[j:j + 1, :]
        acc = term if acc is None else acc + term
    y = _silu(acc)
    return jnp.concatenate([y[0:lc], y[lc + g:lt + g]], axis=0)


def _l2n(t):
    return t * lax.rsqrt(jnp.sum(t * t, axis=-1, keepdims=True) + NORM_EPS)


def _gprep_kernel(q_ref, k_ref, v_ref, wq_ref, wk_ref, wv_ref, qo_ref, ko_ref, vo_ref, pad_ref, *, lc):
    q = _conv_silu(pad_ref, q_ref[0].astype(_F32), wq_ref[...], lc)
    qo_ref[0] = (_l2n(q) * (HEAD_DIM ** -0.5)).astype(qo_ref.dtype)
    k = _conv_silu(pad_ref, k_ref[0].astype(_F32), wk_ref[...], lc)
    ko_ref[0] = _l2n(k).astype(ko_ref.dtype)
    v = _conv_silu(pad_ref, v_ref[0].astype(_F32), wv_ref[...], lc)
    vo_ref[0] = v.astype(vo_ref.dtype)


def _gprep(proj, conv_w, lc, nh):
    bsz, lt, _ = proj.shape
    hd = HEAD_DIM

    def xs(off):
        return pl.BlockSpec((1, lt, hd), lambda b, h: (b, 0, off + h))

    def ws(off):
        return pl.BlockSpec((SHORT_CONV, hd), lambda b, h: (0, off + h))

    o_spec = pl.BlockSpec((1, lt, hd), lambda b, h: (b, 0, h))
    o_shape = jax.ShapeDtypeStruct((bsz, lt, nh * hd), _BF16)
    return pl.pallas_call(
        functools.partial(_gprep_kernel, lc=lc),
        grid=(bsz, nh),
        in_specs=[xs(0), xs(nh), xs(2 * nh), ws(0), ws(nh), ws(2 * nh)],
        out_specs=[o_spec, o_spec, o_spec],
        out_shape=[o_shape, o_shape, o_shape],
        scratch_shapes=[pltpu.VMEM((lt + 3 * CONV_GAP, hd), _F32)],
        compiler_params=_cp(("arbitrary", "arbitrary")), name="gdn_prep",
    )(proj, proj, proj, conv_w, conv_w, conv_w)


def _gdn_masks():
    c = GDN_CHUNK
    i = np.arange(c)[:, None]
    j = np.arange(c)[None, :]
    tri = np.stack([(i >= j), (i <= j)]).astype(np.float32)
    eye = np.eye(c, dtype=np.float32)
    lv = [(i // 2 == j // 2)]
    s = 2
    while s < c:
        lv.append((i // (2 * s) == j // (2 * s)) & (i // s != j // s))
        s *= 2
    return tri, eye, np.stack(lv).astype(np.float32)


def _gdn_kernel(qf_ref, kf_ref, vf_ref, qb_ref, kb_ref, vb_ref, abf_ref, abb_ref, al_ref, dtb_ref, tri_ref, eye_ref,
                lv_ref, of_ref, ob_ref, s_ref, *, hg, pk):
    c = GDN_CHUNK
    hd = HEAD_DIM
    t = pl.program_id(2)

    @pl.when(t == 0)
    def _():
        s_ref[...] = jnp.zeros_like(s_ref)

    nd = 2
    wl = pk * c
    q_refs, k_refs, v_refs, o_refs = (qf_ref, qb_ref), (kf_ref, kb_ref), (vf_ref, vb_ref), (of_ref, ob_ref)
    eye = eye_ref[...]
    tri_d = [tri_ref[dr] for dr in range(nd)]
    strict_d = [tri_d[dr] - eye for dr in range(nd)]
    beta_d, gc_d, gtot_d = [], [], []
    for dr, ab_ref in enumerate((abf_ref, abb_ref)):
        ab = ab_ref[0, 0, 0]
        z = ab + dtb_ref[dr, 0]
        softplus = jnp.maximum(z, 0.0) + jnp.log(1.0 + jnp.exp(-jnp.abs(z)))
        g = -jnp.exp(al_ref[dr, 0]) * softplus
        beta_d.append(_sigmoid(ab))
        gc_d.append(jnp.dot(tri_d[dr][:, :c], g, preferred_element_type=_F32, precision=_HI))
        gtot_d.append(jnp.sum(g, axis=0, keepdims=True))
    n_lv = lv_ref.shape[0]
    gpd = hg // pk
    heads = range(nd * hg)
    groups = range(nd * gpd)
    members = range(pk)
    sls = [slice(j * hd, (j + 1) * hd) for j in range(hg)]
    gsl = [slice(gi * pk * hd, (gi + 1) * pk * hd) for gi in range(gpd)]
    tri = [tri_d[gi // gpd] for gi in groups]
    strict = [strict_d[gi // gpd] for gi in groups]
    lane_blk = lax.broadcasted_iota(jnp.int32, (c, wl), 1) // c
    head_blk = lax.broadcasted_iota(jnp.int32, (c, pk * hd), 1) // hd

    def block_diag(x):
        return jnp.concatenate([jnp.where(lane_blk == a, x, 0.0) for a in members], axis=0).astype(_BF16)

    def per_member_cols(vals, width):
        return jnp.concatenate([jnp.broadcast_to(v, (c, width)) for v in vals], axis=1)

    b = [beta_d[j // hg][:, j % hg:j % hg + 1] for j in heads]
    gj = [gc_d[j // hg][:, hg + j % hg:hg + j % hg + 1] for j in heads]
    gt = [gtot_d[j // hg][:, hg + j % hg:hg + j % hg + 1] for j in heads]
    eg = [jnp.exp(gj[j]) for j in heads]
    k16 = [k_refs[gi // gpd][0, :, gsl[gi % gpd]] for gi in groups]
    q16 = [q_refs[gi // gpd][0, :, gsl[gi % gpd]] for gi in groups]
    kf = [k16[gi].astype(_F32) for gi in groups]
    kbeta = [kf[gi] * per_member_cols([b[gi * pk + a] for a in members], hd) for gi in groups]
    decay = []
    for gi in groups:
        gcol = jnp.broadcast_to(gj[gi * pk], (c, wl))
        for a in members[1:]:
            gcol = jnp.where(lane_blk == a, gj[gi * pk + a], gcol)
        grow = jnp.sum(gcol * eye, axis=0, keepdims=True)
        decay.append(jnp.exp(jnp.where(tri[gi] > 0.0, gcol - grow, -jnp.inf)))
    kq = []
    for gi in groups:
        k_bd = jnp.concatenate([jnp.where(head_blk == a, kf[gi], 0.0) for a in members], axis=0).astype(_BF16)
        kq.append(_dot_nt(jnp.concatenate([kbeta[gi].astype(_BF16), q16[gi]], axis=0), k_bd))
    lmat = [kq[gi][:c] * decay[gi] * strict[gi] for gi in groups]
    qk = [kq[gi][c:] * decay[gi] for gi in groups]
    dinv = [eye - lmat[gi] * lv_ref[0] for gi in groups]
    for lv in range(1, n_lv):
        wmat = [_dot((lmat[gi] * lv_ref[lv]).astype(_BF16), block_diag(dinv[gi])) for gi in groups]
        dinv = [dinv[gi] - _dot(dinv[gi].astype(_BF16), block_diag(wmat[gi])) for gi in groups]
    uw = []
    for gi in groups:
        vf = v_refs[gi // gpd][0, :, gsl[gi % gpd]].astype(_F32)
        rhs = jnp.concatenate(
            [jnp.concatenate([vf[:, a * hd:(a + 1) * hd] * b[gi * pk + a],
                              kbeta[gi][:, a * hd:(a + 1) * hd] * eg[gi * pk + a]], axis=1) for a in members],
            axis=0).astype(_BF16)
        uw.append(_dot(block_diag(dinv[gi]), rhs))
    s = [s_ref[j] for j in heads]
    s16 = [s[j].astype(_BF16) for j in heads]
    q_dec = [(q16[j // pk][:, (j % pk) * hd:(j % pk + 1) * hd].astype(_F32) * eg[j]).astype(_BF16) for j in heads]
    k_dec = [(kf[j // pk][:, (j % pk) * hd:(j % pk + 1) * hd] * jnp.exp(gt[j] - gj[j])).astype(_BF16) for j in heads]
    uw_h = [uw[j // pk][(j % pk) * c:(j % pk + 1) * c] for j in heads]
    ws = [_dot(jnp.concatenate([uw_h[j][:, hd:].astype(_BF16), q_dec[j]], axis=0), s16[j]) for j in heads]
    v16 = [(uw_h[j][:, :hd] - ws[j][:c]).astype(_BF16) for j in heads]
    intra = [_dot(block_diag(qk[gi]), jnp.concatenate([v16[gi * pk + a] for a in members], axis=0))
             for gi in groups]
    outs = [ws[j][c:] + intra[j // pk][(j % pk) * c:(j % pk + 1) * c] for j in heads]
    states = [s[j] * jnp.exp(gt[j]) + _dot_tn(k_dec[j], v16[j]) for j in heads]

    for j in heads:
        o_refs[j // hg][0, :, sls[j % hg]] = outs[j].astype(of_ref.dtype)
    for j in heads:
        s_ref[j] = states[j]


def _gdn(q, k, v, ab_r, alog_r, dtb_r, lc, hg):
    bsz, lt, gw = q.shape
    nh = gw // HEAD_DIM
    nhg = nh // hg
    c = GDN_CHUNK
    nc_c = lc // c
    nc = lt // c
    pk = min(GDN_PACK, hg)
    assert hg % pk == 0 and (pk * c) % LANES == 0
    tri, eye, lvm = (np.tile(m, pk) for m in _gdn_masks())

    def bwd(t):
        return jnp.where(t < nc_c, nc_c - 1 - t, nc - 1 + nc_c - t)

    bw = hg * HEAD_DIM
    xf_spec = pl.BlockSpec((1, c, bw), lambda b, g, t: (b, t, g))
    xb_spec = pl.BlockSpec((1, c, bw), lambda b, g, t: (b, bwd(t), g))
    gate_spec = pl.BlockSpec((2, 1, 1, GATE_LANES), lambda b, g, t: (0, g, 0, 0))
    o_shape = jax.ShapeDtypeStruct((bsz, lt, gw), _BF16)
    return pl.pallas_call(
        functools.partial(_gdn_kernel, hg=hg, pk=pk),
        grid=(bsz, nhg, nc),
        in_specs=[
            xf_spec, xf_spec, xf_spec, xb_spec, xb_spec, xb_spec,
            pl.BlockSpec((1, 1, 1, c, GATE_LANES), lambda b, g, t: (0, g, b, t, 0)),
            pl.BlockSpec((1, 1, 1, c, GATE_LANES), lambda b, g, t: (1, g, b, bwd(t), 0)),
            gate_spec, gate_spec,
            pl.BlockSpec((2, c, pk * c), lambda b, g, t: (0, 0, 0)),
            pl.BlockSpec((c, pk * c), lambda b, g, t: (0, 0)),
            pl.BlockSpec(lvm.shape, lambda b, g, t: (0, 0, 0)),
        ],
        out_specs=[xf_spec, xb_spec],
        out_shape=[o_shape, o_shape],
        scratch_shapes=[pltpu.VMEM((2 * hg, HEAD_DIM, HEAD_DIM), _F32)],
        compiler_params=_cp(("arbitrary", "arbitrary", "arbitrary")), name="gdn_scan",
    )(q, k, v, q, k, v, ab_r, ab_r, alog_r, dtb_r, jnp.asarray(tri), jnp.asarray(eye), jnp.asarray(lvm))


def _gdnout_kernel(of_ref, ob_ref, z_ref, nw_ref, y_ref, *, nh):
    hd = HEAD_DIM
    for h in range(nh):
        sl = slice(h * hd, (h + 1) * hd)
        o = of_ref[0, :, sl].astype(_F32) + ob_ref[0, :, sl].astype(_F32)
        ms = jnp.mean(o * o, axis=-1, keepdims=True)
        y = o * lax.rsqrt(ms + NORM_EPS) * nw_ref[...]
        y_ref[0, :, sl] = (y * _silu(z_ref[0, :, sl].astype(_F32))).astype(y_ref.dtype)


def _gdnout(o_fwd, o_bwd, proj, z_off_blocks, norm_w):
    bsz, lt, gw = o_fwd.shape
    nh = gw // HEAD_DIM
    tr = ROW_TILE
    return pl.pallas_call(
        functools.partial(_gdnout_kernel, nh=nh),
        grid=(bsz, lt // tr),
        in_specs=[
            pl.BlockSpec((1, tr, gw), lambda b, i: (b, i, 0)),
            pl.BlockSpec((1, tr, gw), lambda b, i: (b, i, 0)),
            pl.BlockSpec((1, tr, gw), lambda b, i: (b, i, z_off_blocks)),
            pl.BlockSpec((1, HEAD_DIM), lambda b, i: (0, 0)),
        ],
        out_specs=pl.BlockSpec((1, tr, gw), lambda b, i: (b, i, 0)),
        out_shape=jax.ShapeDtypeStruct((bsz, lt, gw), _BF16),
        compiler_params=_cp(("arbitrary", "arbitrary")), name="gdn_out",
    )(o_fwd, o_bwd, proj, norm_w.reshape(1, HEAD_DIM))


def _na_bias_table(rpb, rows):
    wc = NA_WIN_C
    qc = np.arange(GRID_W)[:, None]
    kc = np.arange(GRID_W)[None, :]
    win_start = np.clip(qc - wc // 2, 0, GRID_W - wc)
    valid = (kc >= win_start) & (kc < win_start + wc)
    off = np.clip(kc - qc + wc - 1, 0, 2 * wc - 2)
    ro = np.arange(NA_WIN_R)[:, None] + np.arange(NA_WIN_R)[None, :]
    tab = rpb.astype(_F32)[:, ro][:, :, :, off]
    tab = jnp.where(valid[None, None, None], tab, MASK_VALUE)
    tab = jnp.transpose(tab, (0, 1, 3, 2, 4))
    return tab.reshape(rpb.shape[0], NA_WIN_R, GRID_W, NA_WIN_R * GRID_W)


def _rms_head(t, w):
    return t * lax.rsqrt(jnp.mean(t * t, axis=-1, keepdims=True) + NORM_EPS) * w


def _na_kernel(q_ref, k_ref, v_ref, tab_ref, nw_ref, o_ref, kn_ref, *, hg, lc, rows):
    hd = HEAD_DIM
    w = GRID_W
    nloc = NA_WIN_R * w
    ncc = lc // w
    r = pl.program_id(2)

    @pl.when(r == 0)
    def _():
        for j in range(hg):
            sl = slice(j * hd, (j + 1) * hd)
            kn_ref[:, sl] = _rms_head(k_ref[0, :, sl].astype(_F32), nw_ref[1:2, :]).astype(kn_ref.dtype)

    rr = r - ncc
    sr = jnp.clip(rr - NA_WIN_R // 2, 0, rows - NA_WIN_R)
    start = pl.multiple_of(lc + sr * w, w)
    ro = jnp.clip(sr - rr + NA_WIN_R - 1, 0, NA_WIN_R - 1)

    heads = range(hg)
    sls = [slice(j * hd, (j + 1) * hd) for j in heads]

    def normed_q():
        return [(_rms_head(q_ref[0, :, sls[j]].astype(_F32), nw_ref[0:1, :]) * (hd ** -0.5)).astype(_BF16)
                for j in heads]

    def store(outs):
        for j in heads:
            o_ref[0, :, sls[j]] = outs[j].astype(o_ref.dtype)

    def lane_tile_reduce(op, *arrays):
        tiles = [a[:, i:i + LANES] for a in arrays for i in range(0, a.shape[1], LANES)]
        acc = tiles[0]
        for tl in tiles[1:]:
            acc = op(acc, tl)
        return acc

    @pl.when(r < ncc)
    def _():
        qn = normed_q()
        s_ctx = [_dot_nt(qn[j], kn_ref[0:lc, sls[j]]) for j in heads]
        p = [jnp.exp(s_ctx[j] - jnp.max(s_ctx[j], axis=-1, keepdims=True)) for j in heads]
        den = [jnp.sum(p[j], axis=-1, keepdims=True) for j in heads]
        store([_dot(p[j].astype(_BF16), v_ref[0, 0:lc, sls[j]]) / den[j] for j in heads])

    @pl.when(r >= ncc)
    def _():
        qn = normed_q()
        s_ctx = [_dot_nt(qn[j], kn_ref[0:lc, sls[j]]) for j in heads]
        s_loc = [_dot_nt(qn[j], kn_ref[pl.ds(start, nloc), sls[j]]) + tab_ref[j, ro] for j in heads]
        m = [jnp.max(lane_tile_reduce(jnp.maximum, s_loc[j], s_ctx[j]), axis=-1, keepdims=True) for j in heads]
        p_loc = [jnp.exp(s_loc[j] - m[j]) for j in heads]
        p_ctx = [jnp.exp(s_ctx[j] - m[j]) for j in heads]
        den = [jnp.sum(lane_tile_reduce(jnp.add, p_loc[j], p_ctx[j]), axis=-1, keepdims=True) for j in heads]
        o = [_dot(p_loc[j].astype(_BF16), v_ref[0, pl.ds(start, nloc), sls[j]])
             + _dot(p_ctx[j].astype(_BF16), v_ref[0, 0:lc, sls[j]]) for j in heads]
        store([o[j] / den[j] for j in heads])


def _na(proj, q_off, k_off, v_off, tab, qk_norm_w, lc, nh, hg):
    bsz, lt, _ = proj.shape
    rows = (lt - lc) // GRID_W
    nhg = nh // hg
    bw = hg * HEAD_DIM
    return pl.pallas_call(
        functools.partial(_na_kernel, hg=hg, lc=lc, rows=rows),
        grid=(bsz, nhg, lt // GRID_W),
        in_specs=[
            pl.BlockSpec((1, GRID_W, bw), lambda b, g, r: (b, r, q_off + g)),
            pl.BlockSpec((1, lt, bw), lambda b, g, r: (b, 0, k_off + g)),
            pl.BlockSpec((1, lt, bw), lambda b, g, r: (b, 0, v_off + g)),
            pl.BlockSpec((hg,) + tab.shape[1:], lambda b, g, r: (g, 0, 0, 0)),
            pl.BlockSpec((2, HEAD_DIM), lambda b, g, r: (0, 0)),
        ],
        out_specs=pl.BlockSpec((1, GRID_W, bw), lambda b, g, r: (b, r, g)),
        out_shape=jax.ShapeDtypeStruct((bsz, lt, nh * HEAD_DIM), _BF16),
        scratch_shapes=[pltpu.VMEM((lt, bw), _BF16)],
        compiler_params=_cp(("arbitrary", "arbitrary", "arbitrary")), name="natten",
    )(proj, proj, proj, tab, qk_norm_w)


def _outproj_kernel(ya_ref, yb_ref, wa_ref, wb_ref, x_ref, ml_ref, mc_ref, o_ref, *, lc, tiles_per_sample, k_gate):
    tm = x_ref.shape[0]
    acc = _dot(ya_ref[...], wa_ref[0]) + _dot(yb_ref[...], wb_ref[0])
    row = (pl.program_id(0) % tiles_per_sample) * tm + lax.broadcasted_iota(jnp.int32, acc.shape, 0)
    gate = jnp.where(row < lc, mc_ref[0, k_gate:k_gate + 1, :], ml_ref[0, k_gate:k_gate + 1, :])
    o_ref[...] = x_ref[...] + gate * acc


def _outproj(ya, yb, w_out, layer, xx, mod, lc, k_gate):
    bsz, lt, d = xx.shape
    t = bsz * lt
    ga = ya.shape[-1]
    gb = yb.shape[-1]
    tm = _pick(lt, (1152, 768, 576, 384, 256, 128))
    tn = _pick(d, (512, 256, 128))
    tps = lt // tm
    out = pl.pallas_call(
        functools.partial(_outproj_kernel, lc=lc, tiles_per_sample=tps, k_gate=k_gate),
        grid=(t // tm, d // tn),
        in_specs=[
            pl.BlockSpec((tm, ga), lambda i, j: (i, 0)),
            pl.BlockSpec((tm, gb), lambda i, j: (i, 0)),
            pl.BlockSpec((1, ga, tn), lambda i, j: (layer, 0, j)),
            pl.BlockSpec((1, gb, tn), lambda i, j: (layer, ga // gb, j)),
            pl.BlockSpec((tm, tn), lambda i, j: (i, j)),
            pl.BlockSpec((1, N_MOD, tn), lambda i, j: (i // tps, 0, j)),
            pl.BlockSpec((1, N_MOD, tn), lambda i, j: (bsz, 0, j)),
        ],
        out_specs=pl.BlockSpec((tm, tn), lambda i, j: (i, j)),
        out_shape=jax.ShapeDtypeStruct((t, d), _F32),
        compiler_params=_cp(("arbitrary", "arbitrary")), name="out_proj",
    )(ya.reshape(t, ga), yb.reshape(t, gb), w_out, w_out, xx.reshape(t, d), mod, mod)
    return out.reshape(bsz, lt, d)


def _route_kernel(lg_ref, br_ref, su_ref, oi_ref, ow_ref, cnt_ref, carry_ref, *, ne):
    epg = ne // N_EXPERT_GROUPS
    tq = lg_ref.shape[1]

    @pl.when(pl.program_id(0) == 0)
    def _():
        carry_ref[...] = jnp.zeros_like(carry_ref)

    sc = _sigmoid(lg_ref[...])
    sel = sc + br_ref[...]
    s = [sel[e:e + 1, :] for e in range(ne)]

    def top2(vals):
        a, b, c_, d_ = vals
        hi1, lo1 = jnp.maximum(a, b), jnp.minimum(a, b)
        hi2, lo2 = jnp.maximum(c_, d_), jnp.minimum(c_, d_)
        return jnp.maximum(hi1, hi2) + jnp.maximum(jnp.maximum(lo1, lo2), jnp.minimum(hi1, hi2))

    gs = [top2(s[g * epg:(g + 1) * epg]) for g in range(N_EXPERT_GROUPS)]
    best = gs[0]
    gsel = jnp.zeros_like(best, dtype=jnp.int32)
    for g in range(1, N_EXPERT_GROUPS):
        better = gs[g] > best
        gsel = jnp.where(better, g, gsel)
        best = jnp.maximum(best, gs[g])
    ing = []
    for j in range(epg):
        v = s[j]
        for g in range(1, N_EXPERT_GROUPS):
            v = jnp.where(gsel == g, s[g * epg + j], v)
        ing.append(v)
    b1 = ing[0]
    i1 = jnp.zeros_like(gsel)
    for j in range(1, epg):
        better = ing[j] > b1
        i1 = jnp.where(better, j, i1)
        b1 = jnp.maximum(b1, ing[j])
    b2 = jnp.full_like(b1, -jnp.inf)
    i2 = jnp.zeros_like(gsel)
    for j in range(epg):
        better = (ing[j] > b2) & (i1 != j)
        i2 = jnp.where(better, j, i2)
        b2 = jnp.where(better, ing[j], b2)
    e1 = gsel * epg + i1
    e2 = gsel * epg + i2

    eio = lax.broadcasted_iota(jnp.int32, (ne, tq), 0)
    oh1 = eio == e1
    oh2 = eio == e2
    w1 = jnp.sum(jnp.where(oh1, sc, 0.0), axis=0, keepdims=True)
    w2 = jnp.sum(jnp.where(oh2, sc, 0.0), axis=0, keepdims=True)
    wsum = w1 + w2
    oh = jnp.where(oh1, 1.0, 0.0) + jnp.where(oh2, 1.0, 0.0)
    rank = _dot(oh.astype(_BF16), su_ref[...]) + carry_ref[:, 0:1]
    r1 = jnp.sum(jnp.where(oh1, rank, 0.0), axis=0, keepdims=True)
    r2 = jnp.sum(jnp.where(oh2, rank, 0.0), axis=0, keepdims=True)
    carry = carry_ref[...] + jnp.sum(oh, axis=1, keepdims=True)
    carry_ref[...] = carry
    cnt_ref[...] = carry

    sub = lax.broadcasted_iota(jnp.int32, (8, tq), 0)
    oi = jnp.where(sub == 0, e1, jnp.where(sub == 1, e2,
         jnp.where(sub == 2, r1.astype(jnp.int32), jnp.where(sub == 3, r2.astype(jnp.int32), 0))))
    oi_ref[...] = oi
    ow_ref[...] = jnp.where(sub == 0, w1 / wsum, jnp.where(sub == 1, w2 / wsum, 0.0))


def _route(logits_t, b_router):
    ne, t = logits_t.shape
    tq = ROW_TILE
    su = np.triu(np.ones((tq, tq), np.float32), 1)
    return pl.pallas_call(
        functools.partial(_route_kernel, ne=ne),
        grid=(t // tq,),
        in_specs=[pl.BlockSpec((ne, tq), lambda i: (0, i)),
                  pl.BlockSpec((ne, 1), lambda i: (0, 0)),
                  pl.BlockSpec((tq, tq), lambda i: (0, 0))],
        out_specs=[pl.BlockSpec((8, tq), lambda i: (0, i)),
                   pl.BlockSpec((8, tq), lambda i: (0, i)),
                   pl.BlockSpec((ne, 128), lambda i: (0, 0))],
        out_shape=[jax.ShapeDtypeStruct((8, t), jnp.int32),
                   jax.ShapeDtypeStruct((8, t), _F32),
                   jax.ShapeDtypeStruct((ne, 128), _F32)],
        scratch_shapes=[pltpu.VMEM((ne, 128), _F32)],
        compiler_params=_cp(("arbitrary",)), name="router",
    )(logits_t, b_router.reshape(ne, 1).astype(_F32), jnp.asarray(su, _BF16))


def _new_weight_block(te_ref, i):
    return (i == 0) | (te_ref[i] != te_ref[jnp.maximum(i - 1, 0)])


def _moe_up_kernel(te_ref, x_ref, wg_ref, wu_ref, h_ref, wg16_ref, wu16_ref):
    i = pl.program_id(1)
    n_valid = te_ref[te_ref.shape[0] - 1]

    @pl.when(_new_weight_block(te_ref, i))
    def _():
        wg16_ref[...] = wg_ref[0, 0].astype(_BF16)
        wu16_ref[...] = wu_ref[0, 0].astype(_BF16)

    @pl.when(i < n_valid)
    def _():
        x = x_ref[...]
        gt = _dot(x, wg16_ref[...])
        up = _dot(x, wu16_ref[...])
        h_ref[...] = (_silu(gt) * up).astype(h_ref.dtype)

    @pl.when(i >= n_valid)
    def _():
        h_ref[...] = jnp.zeros_like(h_ref)


def _moe_down_kernel(te_ref, *refs, n_parts, part_tiles):
    h_refs = refs[:n_parts]
    wd_ref, o_ref, wd16_ref = refs[n_parts:]
    i = pl.program_id(1)
    n_valid = te_ref[te_ref.shape[0] - 1]

    @pl.when(_new_weight_block(te_ref, i))
    def _():
        wd16_ref[...] = wd_ref[0, 0].astype(_BF16)

    for part in range(n_parts):
        @pl.when((i >= part * part_tiles) & (i < jnp.minimum(n_valid, (part + 1) * part_tiles)))
        def _(h_ref=h_refs[part]):
            o_ref[...] = _dot(h_ref[...], wd16_ref[...]).astype(o_ref.dtype)

    @pl.when(i >= n_valid)
    def _():
        o_ref[...] = jnp.zeros_like(o_ref)


def _moe_experts(parts, tile_e, w_gate, w_up, w_down, layer):
    d = parts[0][0].shape[1]
    f = w_gate.shape[-1]
    tm = MOE_TILE
    tf = _pick(f, (MOE_F_TILE, 256, 128))
    td = _pick(d, (MOE_D_TILE, 1024, 512, 256, 128))
    hids = []
    for xs, te_half in parts:
        nth = xs.shape[0] // tm
        hids.append(pl.pallas_call(
            _moe_up_kernel,
            grid_spec=pltpu.PrefetchScalarGridSpec(
                num_scalar_prefetch=1, grid=(f // tf, nth),
                in_specs=[pl.BlockSpec((tm, d), lambda j, i, te: (i, 0)),
                          pl.BlockSpec((1, 1, d, tf), lambda j, i, te: (layer, te[i], 0, j)),
                          pl.BlockSpec((1, 1, d, tf), lambda j, i, te: (layer, te[i], 0, j))],
                out_specs=pl.BlockSpec((tm, tf), lambda j, i, te: (i, j)),
                scratch_shapes=[pltpu.VMEM((d, tf), _BF16), pltpu.VMEM((d, tf), _BF16)]),
            out_shape=jax.ShapeDtypeStruct((nth * tm, f), _BF16),
            compiler_params=_cp(("arbitrary", "arbitrary")), name="moe_up",
        )(te_half, xs, w_gate, w_up))
    n_parts = len(hids)
    part_tiles = hids[0].shape[0] // tm
    nt = n_parts * part_tiles

    def part_spec(part):
        return pl.BlockSpec((tm, f), lambda j, i, te: (jnp.clip(i - part * part_tiles, 0, part_tiles - 1), 0))

    return pl.pallas_call(
        functools.partial(_moe_down_kernel, n_parts=n_parts, part_tiles=part_tiles),
        grid_spec=pltpu.PrefetchScalarGridSpec(
            num_scalar_prefetch=1, grid=(d // td, nt),
            in_specs=[part_spec(part) for part in range(n_parts)]
                     + [pl.BlockSpec((1, 1, f, td), lambda j, i, te: (layer, te[i], 0, j))],
            out_specs=pl.BlockSpec((tm, td), lambda j, i, te: (i, j)),
            scratch_shapes=[pltpu.VMEM((f, td), _BF16)]),
        out_shape=jax.ShapeDtypeStruct((nt * tm, d), _BF16),
        compiler_params=_cp(("arbitrary", "arbitrary")), name="moe_down",
    )(tile_e, *hids, w_down)


def _comb_kernel(x_ref, y0_ref, y1_ref, w_ref, mod_ref, o_ref, *, k_gate):
    y = y0_ref[0].astype(_F32) * w_ref[0, :, 0:1] + y1_ref[0].astype(_F32) * w_ref[0, :, 1:2]
    o_ref[0] = x_ref[0] + mod_ref[0, k_gate:k_gate + 1, :] * y


def _comb_norm_kernel(x_ref, y0_ref, y1_ref, w_ref, mod_ref, nw_ref, modn_ref, o_ref, a_ref, *, k_gate):
    y = y0_ref[0].astype(_F32) * w_ref[0, :, 0:1] + y1_ref[0].astype(_F32) * w_ref[0, :, 1:2]
    o = x_ref[0] + mod_ref[0, k_gate:k_gate + 1, :] * y
    o_ref[0] = o
    a_ref[0] = _modnorm(o, nw_ref[...], modn_ref[0], 0, 1).astype(a_ref.dtype)


def _combine(xx, y0, y1, w01, mod, lc, k_gate, latent_only, next_norm=None):
    bsz, lt, d = xx.shape
    tr = ROW_TILE
    skip = lc // tr if latent_only else 0
    nt = lt // tr - skip
    row = lambda b, i: (b, i + skip, 0)
    yrow = lambda b, i: (b, i, 0)
    mrow = _mod_row(bsz, lc // tr)
    mod_spec = pl.BlockSpec((1, N_MOD, d), lambda b, i: mrow(b, i + skip))
    in_specs = [pl.BlockSpec((1, tr, d), row), pl.BlockSpec((1, tr, d), yrow), pl.BlockSpec((1, tr, d), yrow),
                pl.BlockSpec((1, tr, 2), yrow), mod_spec]
    o_spec = pl.BlockSpec((1, tr, d), yrow)
    o_shape = jax.ShapeDtypeStruct((bsz, nt * tr, d), _F32)
    if next_norm is None:
        return pl.pallas_call(
            functools.partial(_comb_kernel, k_gate=k_gate),
            grid=(bsz, nt), in_specs=in_specs, out_specs=o_spec, out_shape=o_shape,
            compiler_params=_cp(("arbitrary", "arbitrary")), name="moe_combine",
        )(xx, y0, y1, w01, mod)
    nw_next, mod_next = next_norm
    return pl.pallas_call(
        functools.partial(_comb_norm_kernel, k_gate=k_gate),
        grid=(bsz, nt),
        in_specs=in_specs + [pl.BlockSpec((1, d), lambda b, i: (0, 0)), mod_spec],
        out_specs=[o_spec, o_spec],
        out_shape=[o_shape, jax.ShapeDtypeStruct((bsz, nt * tr, d), _BF16)],
        compiler_params=_cp(("arbitrary", "arbitrary")), name="moe_combine_norm",
    )(xx, y0, y1, w01, mod, nw_next.reshape(1, d), mod_next)


def _moe(f, logits_t, b_router, w_gate, w_up, w_down, layer):
    t, d = f.shape
    ne = w_gate.shape[1]
    tm = MOE_TILE
    oi, ow, cnt = _route(logits_t, b_router)
    e01 = oi[0:2]
    r01 = oi[2:4]
    counts = cnt[:, 0].astype(jnp.int32)
    tiles = (counts + tm - 1) // tm
    tile_end = jnp.cumsum(tiles)
    off = (tile_end - tiles) * tm
    first = jnp.sum(jnp.where(e01[None] == jnp.arange(ne, dtype=jnp.int32)[:, None, None],
                              off[:, None, None], 0), axis=0)
    pos = first + r01
    nt = (t * TOP_K) // tm + ne
    p = nt * tm
    tok = jnp.broadcast_to(jnp.arange(t, dtype=jnp.int32)[None, :], (TOP_K, t))
    src = (jnp.arange(p, dtype=jnp.int32) % t).at[pos.reshape(-1)].set(tok.reshape(-1))
    n_valid = tile_end[-1]
    tile_idx = jnp.arange(nt, dtype=jnp.int32)
    tile_e = jnp.sum((tile_end[None, :] <= jnp.minimum(tile_idx, n_valid - 1)[:, None]).astype(jnp.int32), axis=1)
    assert nt % MOE_GATHER_PARTS == 0
    part_tiles = nt // MOE_GATHER_PARTS
    parts = []
    for lo in range(0, nt, part_tiles):
        hi = lo + part_tiles
        n_val = jnp.clip(n_valid - lo, 0, part_tiles).astype(jnp.int32)
        parts.append((f.at[src[lo * tm:hi * tm]].get(mode="promise_in_bounds"),
                      jnp.concatenate([tile_e[lo:hi], n_val[None]])))
    tile_e = jnp.concatenate([tile_e, n_valid[None].astype(jnp.int32)])
    out = _moe_experts(parts, tile_e, w_gate, w_up, w_down, layer)
    y0 = out.at[pos[0]].get(mode="promise_in_bounds")
    y1 = out.at[pos[1]].get(mode="promise_in_bounds")
    return y0, y1, jnp.transpose(ow[0:2])


def kernel(x, c, ctx, c_ctx, norm1_w, norm2_w, w_ada, b_ada, w_in, conv_w, a_log, dt_bias, gdn_norm_w,
           qk_norm_w, na_rpb, w_out, w_router, b_router, w_gate, w_up, w_down):
    bsz, s_len, d = x.shape
    lc = ctx.shape[1]
    lt = lc + s_len
    t = bsz * lt
    depth = w_ada.shape[0]
    gw = d // 2
    naw = d - gw
    nh_a = gw // HEAD_DIM
    nh_b = naw // HEAD_DIM
    rows = s_len // GRID_W
    assert s_len % GRID_W == 0 and rows >= NA_WIN_R
    assert lc % ROW_TILE == 0 and s_len % ROW_TILE == 0
    hg_a = min(GDN_HEADS_PER_STEP, nh_a)
    hg_b = min(NA_HEADS_PER_STEP, nh_b)
    nhg_a = nh_a // hg_a

    xx = jnp.concatenate([ctx, x], axis=1)
    mp = -(-(bsz + 1) // 8) * 8
    cc = jnp.zeros((mp, d), _F32).at[:bsz].set(c).at[bsz].set(c_ctx)
    mod_all = _ada(cc, w_ada, b_ada).reshape(depth, mp, N_MOD, d)
    w_router_t = jnp.transpose(w_router).astype(_F32)

    col_b = 4 * gw + 4 * nh_a
    w_in16 = w_in.astype(_BF16)
    w_b = w_in16[:, :, col_b:]
    w_ab = jnp.zeros((depth, d, LANES), _BF16).at[:, :, :4 * nh_a].set(w_in16[:, :, 4 * gw:col_b])
    w_out16 = w_out.astype(_BF16)

    for layer in range(depth):
        last = layer == depth - 1
        mod = mod_all[layer]

        if layer == 0:
            a = _norm(xx, norm1_w[layer], mod, lc, 0, 1)
        a2 = a.reshape(t, d)
        proj_a = _matmul(a2, w_in16, layer, 4 * gw, _BF16).reshape(bsz, lt, 4 * gw)
        proj_b = _matmul(a2, w_b, layer, 3 * naw, _BF16).reshape(bsz, lt, 3 * naw)
        ab = _matmul(a2, w_ab, layer, LANES, _F32)[:, :4 * nh_a].reshape(bsz, lt, 2, 2, nhg_a, hg_a)

        q, k, v = _gprep(proj_a, conv_w[layer], lc, nh_a)
        lane_pad = GATE_LANES - 2 * hg_a
        ab_r = jnp.transpose(ab, (3, 4, 0, 1, 2, 5)).reshape(2, nhg_a, bsz, lt, 2 * hg_a)
        ab_r = jnp.pad(ab_r, ((0, 0),) * 4 + ((0, lane_pad),))
        zpad = jnp.zeros((2, nhg_a, 1, hg_a), _F32)
        zlane = jnp.zeros((2, nhg_a, 1, lane_pad), _F32)
        alog_r = jnp.concatenate([zpad, a_log[layer].astype(_F32).reshape(2, nhg_a, 1, hg_a), zlane], axis=-1)
        dtb_r = jnp.concatenate([zpad, dt_bias[layer].astype(_F32).reshape(2, nhg_a, 1, hg_a), zlane], axis=-1)
        o_fwd, o_bwd = _gdn(q, k, v, ab_r, alog_r, dtb_r, lc, hg_a)
        ya = _gdnout(o_fwd, o_bwd, proj_a, 3, gdn_norm_w[layer])

        tab = _na_bias_table(na_rpb[layer], rows)
        nb = naw // (hg_b * HEAD_DIM)
        yb = _na(proj_b, 0, nb, 2 * nb, tab, qk_norm_w[layer].astype(_F32), lc, nh_b, hg_b)

        xx = _outproj(ya, yb, w_out16, layer, xx, mod, lc, 2)

        f, logits_t = _norm(xx, norm2_w[layer], mod, lc, 3, 4, w_router_t, latent_only=last)
        rows_moe = f.shape[1]
        y0, y1, w01 = _moe(f.reshape(bsz * rows_moe, d), logits_t, b_router, w_gate, w_up, w_down, layer)
        ys = (y0.reshape(bsz, rows_moe, d), y1.reshape(bsz, rows_moe, d), w01.reshape(bsz, rows_moe, 2))
        if last:
            xx = _combine(xx, *ys, mod, lc, 5, True)
        else:
            xx, a = _combine(xx, *ys, mod, lc, 5, False, next_norm=(norm1_w[layer + 1], mod_all[layer + 1]))
    return xx
```

```python
import functools
import math

import numpy as np
import jax
import jax.numpy as jnp
from jax import lax
from jax.experimental import pallas as pl
from jax.experimental.pallas import tpu as pltpu

HEAD_DIM = 128
GRID_W = 64
GDN_CHUNK = 64
SHORT_CONV = 5
NA_WIN_R = 8
NA_WIN_C = 16
N_EXPERT_GROUPS = 4
TOP_K = 2
N_MOD = 6
NORM_EPS = 1e-6
MASK_VALUE = -1e30

ROW_TILE = 256
MOE_TILE = 256
MOE_F_TILE = 512
MOE_D_TILE = 4096
MOE_GATHER_PARTS = 4
LANES = 128
GATE_LANES = 128
CONV_GAP = 8
GDN_HEADS_PER_STEP = 16
GDN_PACK = 4
NA_HEADS_PER_STEP = 8
NA_SCORE_LOOKAHEAD = 3
VMEM_LIMIT = 56 * 1024 * 1024

_F32 = jnp.float32
_BF16 = jnp.bfloat16
_HI = lax.Precision.HIGHEST


def _cp(sem, vmem=VMEM_LIMIT, flags=None):
    return pltpu.CompilerParams(dimension_semantics=sem, vmem_limit_bytes=vmem, flags=flags)


def _pick(n, cands):
    for c in cands:
        if n % c == 0:
            return c
    raise ValueError(f"no tile for {n} in {cands}")


def _dot(a, b):
    return jnp.dot(a, b, preferred_element_type=_F32)


def _dot_nt(a, b, precision=None):
    return lax.dot_general(a, b, (((1,), (1,)), ((), ())), preferred_element_type=_F32, precision=precision)


def _dot_tn(a, b):
    return lax.dot_general(a, b, (((0,), (0,)), ((), ())), preferred_element_type=_F32)


def _sigmoid(x):
    return 1.0 / (1.0 + jnp.exp(-x))


def _silu(x):
    return x * _sigmoid(x)


def _ada_kernel(cc_ref, w_ref, b_ref, o_ref):
    a = _silu(cc_ref[...]).astype(_BF16)
    o_ref[0] = _dot(a, w_ref[0].astype(_BF16)) + b_ref[0]


def _ada(cc, w_ada, b_ada):
    nl, d, n = w_ada.shape
    mp = cc.shape[0]
    tn = _pick(n, (512, 256, 128))
    return pl.pallas_call(
        _ada_kernel,
        grid=(nl, n // tn),
        in_specs=[
            pl.BlockSpec((mp, d), lambda l, j: (0, 0)),
            pl.BlockSpec((1, d, tn), lambda l, j: (l, 0, j)),
            pl.BlockSpec((1, 1, tn), lambda l, j: (l, 0, j)),
        ],
        out_specs=pl.BlockSpec((1, mp, tn), lambda l, j: (l, 0, j)),
        out_shape=jax.ShapeDtypeStruct((nl, mp, n), _F32),
        compiler_params=_cp(("arbitrary", "arbitrary")),
        name="ada_mod",
    )(cc, w_ada, b_ada.reshape(nl, 1, n))


def _modnorm(x, nw, mod, k_shift, k_scale):
    ms = jnp.mean(x * x, axis=-1, keepdims=True)
    y = x * lax.rsqrt(ms + NORM_EPS) * nw
    return y * (1.0 + mod[k_scale:k_scale + 1, :]) + mod[k_shift:k_shift + 1, :]


def _norm_kernel(x_ref, nw_ref, mod_ref, o_ref, *, k_shift, k_scale):
    y = _modnorm(x_ref[0], nw_ref[...], mod_ref[0], k_shift, k_scale)
    o_ref[0] = y.astype(o_ref.dtype)


def _norm_router_kernel(x_ref, nw_ref, mod_ref, wr_ref, o_ref, lg_ref, *, k_shift, k_scale):
    y = _modnorm(x_ref[0], nw_ref[...], mod_ref[0], k_shift, k_scale)
    o_ref[0] = y.astype(o_ref.dtype)
    lg_ref[...] = _dot_nt(wr_ref[...], y, precision=_HI)


def _mod_row(bsz, n_ctx_tiles):
    return lambda b, i: (jnp.where(i < n_ctx_tiles, bsz, b), 0, 0)


def _norm(xx, nw, mod, lc, k_shift, k_scale, w_router_t=None, latent_only=False):
    bsz, lt, d = xx.shape
    tr = ROW_TILE
    skip = lc // tr if latent_only else 0
    nt = lt // tr - skip
    mrow = _mod_row(bsz, lc // tr)
    in_specs = [
        pl.BlockSpec((1, tr, d), lambda b, i: (b, i + skip, 0)),
        pl.BlockSpec((1, d), lambda b, i: (0, 0)),
        pl.BlockSpec((1, N_MOD, d), lambda b, i: mrow(b, i + skip)),
    ]
    a_spec = pl.BlockSpec((1, tr, d), lambda b, i: (b, i, 0))
    a_shape = jax.ShapeDtypeStruct((bsz, nt * tr, d), _BF16)
    if w_router_t is None:
        return pl.pallas_call(
            functools.partial(_norm_kernel, k_shift=k_shift, k_scale=k_scale),
            grid=(bsz, nt), in_specs=in_specs, out_specs=a_spec, out_shape=a_shape,
            compiler_params=_cp(("arbitrary", "arbitrary")), name="modnorm",
        )(xx, nw.reshape(1, d), mod)
    ne = w_router_t.shape[0]
    return pl.pallas_call(
        functools.partial(_norm_router_kernel, k_shift=k_shift, k_scale=k_scale),
        grid=(bsz, nt),
        in_specs=in_specs + [pl.BlockSpec((ne, d), lambda b, i: (0, 0))],
        out_specs=[a_spec, pl.BlockSpec((ne, tr), lambda b, i: (0, b * nt + i))],
        out_shape=[a_shape, jax.ShapeDtypeStruct((ne, bsz * nt * tr), _F32)],
        compiler_params=_cp(("arbitrary", "arbitrary")), name="modnorm_router",
    )(xx, nw.reshape(1, d), mod, w_router_t)


def _mm_kernel(a_ref, wt_ref, o_ref):
    o_ref[...] = _dot_nt(a_ref[...], wt_ref[0]).astype(o_ref.dtype)


def _matmul(a, wt, layer, n, out_dtype):
    m, k = a.shape
    tm = _pick(m, (1024, 512, 256, 128))
    tn = _pick(n, (1024, 512, 256, 128))
    return pl.pallas_call(
        _mm_kernel,
        grid=(n // tn, m // tm),
        in_specs=[pl.BlockSpec((tm, k), lambda j, i: (i, 0)),
                  pl.BlockSpec((1, tn, k), lambda j, i: (layer, j, 0))],
        out_specs=pl.BlockSpec((tm, tn), lambda j, i: (i, j)),
        out_shape=jax.ShapeDtypeStruct((m, n), out_dtype),
        compiler_params=_cp(("arbitrary", "arbitrary")), name="matmul",
    )(a, wt)


def _conv_silu(pad_ref, x, w, lc):
    lt = x.shape[0]
    g = CONV_GAP
    pad = SHORT_CONV // 2
    zeros = jnp.zeros((g, x.shape[1]), _F32)
    pad_ref[0:g] = zeros
    pad_ref[g:g + lc] = x[0:lc]
    pad_ref[g + lc:2 * g + lc] = zeros
    pad_ref[2 * g + lc:2 * g + lt] = x[lc:lt]
    pad_ref[2 * g + lt:3 * g + lt] = zeros
    acc = None
    for j in range(SHORT_CONV):
        term = pad_ref[g + j - pad:2 * g + lt + j - pad] * w[j:j + 1, :]
        acc = term if acc is None else acc + term
    y = _silu(acc)
    return jnp.concatenate([y[0:lc], y[lc + g:lt + g]], axis=0)


def _l2n(t):
    return t * lax.rsqrt(jnp.sum(t * t, axis=-1, keepdims=True) + NORM_EPS)


def _gprep_kernel(q_ref, k_ref, v_ref, wq_ref, wk_ref, wv_ref, qo_ref, ko_ref, vo_ref, pad_ref, *, lc):
    q = _conv_silu(pad_ref, q_ref[0].astype(_F32), wq_ref[...], lc)
    qo_ref[0] = (_l2n(q) * (HEAD_DIM ** -0.5)).astype(qo_ref.dtype)
    k = _conv_silu(pad_ref, k_ref[0].astype(_F32), wk_ref[...], lc)
    ko_ref[0] = _l2n(k).astype(ko_ref.dtype)
    v = _conv_silu(pad_ref, v_ref[0].astype(_F32), wv_ref[...], lc)
    vo_ref[0] = v.astype(vo_ref.dtype)


def _gprep(proj, conv_w, lc, nh):
    bsz, lt, _ = proj.shape
    hd = HEAD_DIM

    def xs(off):
        return pl.BlockSpec((1, lt, hd), lambda b, h: (b, 0, off + h))

    def ws(off):
        return pl.BlockSpec((SHORT_CONV, hd), lambda b, h: (0, off + h))

    o_spec = pl.BlockSpec((1, lt, hd), lambda b, h: (b, 0, h))
    o_shape = jax.ShapeDtypeStruct((bsz, lt, nh * hd), _BF16)
    return pl.pallas_call(
        functools.partial(_gprep_kernel, lc=lc),
        grid=(bsz, nh),
        in_specs=[xs(0), xs(nh), xs(2 * nh), ws(0), ws(nh), ws(2 * nh)],
        out_specs=[o_spec, o_spec, o_spec],
        out_shape=[o_shape, o_shape, o_shape],
        scratch_shapes=[pltpu.VMEM((lt + 3 * CONV_GAP, hd), _F32)],
        compiler_params=_cp(("arbitrary", "arbitrary")), name="gdn_prep",
    )(proj, proj, proj, conv_w, conv_w, conv_w)


def _gdn_masks():
    c = GDN_CHUNK
    i = np.arange(c)[:, None]
    j = np.arange(c)[None, :]
    tri = np.stack([(i >= j), (i <= j)]).astype(np.float32)
    eye = np.eye(c, dtype=np.float32)
    lv = [(i // 2 == j // 2)]
    s = 2
    while s < c:
        lv.append((i // (2 * s) == j // (2 * s)) & (i // s != j // s))
        s *= 2
    return tri, eye, np.stack(lv).astype(np.float32)


def _gdn_kernel(qf_ref, kf_ref, vf_ref, qb_ref, kb_ref, vb_ref, abf_ref, abb_ref, al_ref, dtb_ref, tri_ref, eye_ref,
                lv_ref, of_ref, ob_ref, s_ref, *, hg, pk):
    c = GDN_CHUNK
    hd = HEAD_DIM
    t = pl.program_id(2)

    @pl.when(t == 0)
    def _():
        s_ref[...] = jnp.zeros_like(s_ref)

    nd = 2
    wl = pk * c
    q_refs, k_refs, v_refs, o_refs = (qf_ref, qb_ref), (kf_ref, kb_ref), (vf_ref, vb_ref), (of_ref, ob_ref)
    eye = eye_ref[...]
    tri_d = [tri_ref[dr] for dr in range(nd)]
    strict_d = [tri_d[dr] - eye for dr in range(nd)]
    beta_d, gc_d, gtot_d = [], [], []
    for dr, ab_ref in enumerate((abf_ref, abb_ref)):
        ab = ab_ref[0, 0, 0]
        z = ab + dtb_ref[dr, 0]
        softplus = jnp.maximum(z, 0.0) + jnp.log(1.0 + jnp.exp(-jnp.abs(z)))
        g = -jnp.exp(al_ref[dr, 0]) * softplus
        beta_d.append(_sigmoid(ab))
        gc_d.append(jnp.dot(tri_d[dr][:, :c], g, preferred_element_type=_F32, precision=_HI))
        gtot_d.append(jnp.sum(g, axis=0, keepdims=True))
    n_lv = lv_ref.shape[0]
    gpd = hg // pk
    heads = range(nd * hg)
    groups = range(nd * gpd)
    members = range(pk)
    sls = [slice(j * hd, (j + 1) * hd) for j in range(hg)]
    gsl = [slice(gi * pk * hd, (gi + 1) * pk * hd) for gi in range(gpd)]
    tri = [tri_d[gi // gpd] for gi in groups]
    strict = [strict_d[gi // gpd] for gi in groups]
    lane_blk = lax.broadcasted_iota(jnp.int32, (c, wl), 1) // c
    head_blk = lax.broadcasted_iota(jnp.int32, (c, pk * hd), 1) // hd

    def block_diag(x):
        return jnp.concatenate([jnp.where(lane_blk == a, x, 0.0) for a in members], axis=0).astype(_BF16)

    def per_member_cols(vals, width):
        return jnp.concatenate([jnp.broadcast_to(v, (c, width)) for v in vals], axis=1)

    b = [beta_d[j // hg][:, j % hg:j % hg + 1] for j in heads]
    gj = [gc_d[j // hg][:, hg + j % hg:hg + j % hg + 1] for j in heads]
    gt = [gtot_d[j // hg][:, hg + j % hg:hg + j % hg + 1] for j in heads]
    eg = [jnp.exp(gj[j]) for j in heads]
    k16 = [k_refs[gi // gpd][0, :, gsl[gi % gpd]] for gi in groups]
    q16 = [q_refs[gi // gpd][0, :, gsl[gi % gpd]] for gi in groups]
    kf = [k16[gi].astype(_F32) for gi in groups]
    kbeta = [kf[gi] * per_member_cols([b[gi * pk + a] for a in members], hd) for gi in groups]
    decay = []
    for gi in groups:
        gcol = jnp.broadcast_to(gj[gi * pk], (c, wl))
        for a in members[1:]:
            gcol = jnp.where(lane_blk == a, gj[gi * pk + a], gcol)
        grow = jnp.sum(gcol * eye, axis=0, keepdims=True)
        decay.append(jnp.exp(jnp.where(tri[gi] > 0.0, gcol - grow, -jnp.inf)))
    kq = []
    for gi in groups:
        k_bd = jnp.concatenate([jnp.where(head_blk == a, kf[gi], 0.0) for a in members], axis=0).astype(_BF16)
        kq.append(_dot_nt(jnp.concatenate([kbeta[gi].astype(_BF16), q16[gi]], axis=0), k_bd))
    lmat = [kq[gi][:c] * decay[gi] * strict[gi] for gi in groups]
    qk = [kq[gi][c:] * decay[gi] for gi in groups]
    dinv = [eye - lmat[gi] * lv_ref[0] for gi in groups]
    for lv in range(1, n_lv):
        wmat = [_dot((lmat[gi] * lv_ref[lv]).astype(_BF16), block_diag(dinv[gi])) for gi in groups]
        dinv = [dinv[gi] - _dot(dinv[gi].astype(_BF16), block_diag(wmat[gi])) for gi in groups]
    uw = []
    for gi in groups:
        vf = v_refs[gi // gpd][0, :, gsl[gi % gpd]].astype(_F32)
        rhs = jnp.concatenate(
            [jnp.concatenate([vf[:, a * hd:(a + 1) * hd] * b[gi * pk + a],
                              kbeta[gi][:, a * hd:(a + 1) * hd] * eg[gi * pk + a]], axis=1) for a in members],
            axis=0).astype(_BF16)
        uw.append(_dot(block_diag(dinv[gi]), rhs))
    s = [s_ref[j] for j in heads]
    s16 = [s[j].astype(_BF16) for j in heads]
    q_dec = [(q16[j // pk][:, (j % pk) * hd:(j % pk + 1) * hd].astype(_F32) * eg[j]).astype(_BF16) for j in heads]
    k_dec = [(kf[j // pk][:, (j % pk) * hd:(j % pk + 1) * hd] * jnp.exp(gt[j] - gj[j])).astype(_BF16) for j in heads]
    uw_h = [uw[j // pk][(j % pk) * c:(j % pk + 1) * c] for j in heads]
    ws = [_dot(jnp.concatenate([uw_h[j][:, hd:].astype(_BF16), q_dec[j]], axis=0), s16[j]) for j in heads]
    v16 = [(uw_h[j][:, :hd] - ws[j][:c]).astype(_BF16) for j in heads]
    intra = [_dot(block_diag(qk[gi]), jnp.concatenate([v16[gi * pk + a] for a in members], axis=0))
             for gi in groups]
    outs = [ws[j][c:] + intra[j // pk][(j % pk) * c:(j % pk + 1) * c] for j in heads]
    states = [s[j] * jnp.exp(gt[j]) + _dot_tn(k_dec[j], v16[j]) for j in heads]

    for j in heads:
        o_refs[j // hg][0, :, sls[j % hg]] = outs[j].astype(of_ref.dtype)
    for j in heads:
        s_ref[j] = states[j]


def _gdn(q, k, v, ab_r, alog_r, dtb_r, lc, hg):
    bsz, lt, gw = q.shape
    nh = gw // HEAD_DIM
    nhg = nh // hg
    c = GDN_CHUNK
    nc_c = lc // c
    nc = lt // c
    pk = min(GDN_PACK, hg)
    assert hg % pk == 0 and (pk * c) % LANES == 0
    tri, eye, lvm = (np.tile(m, pk) for m in _gdn_masks())

    def bwd(t):
        return jnp.where(t < nc_c, nc_c - 1 - t, nc - 1 + nc_c - t)

    bw = hg * HEAD_DIM
    xf_spec = pl.BlockSpec((1, c, bw), lambda b, g, t: (b, t, g))
    xb_spec = pl.BlockSpec((1, c, bw), lambda b, g, t: (b, bwd(t), g))
    gate_spec = pl.BlockSpec((2, 1, 1, GATE_LANES), lambda b, g, t: (0, g, 0, 0))
    o_shape = jax.ShapeDtypeStruct((bsz, lt, gw), _BF16)
    return pl.pallas_call(
        functools.partial(_gdn_kernel, hg=hg, pk=pk),
        grid=(bsz, nhg, nc),
        in_specs=[
            xf_spec, xf_spec, xf_spec, xb_spec, xb_spec, xb_spec,
            pl.BlockSpec((1, 1, 1, c, GATE_LANES), lambda b, g, t: (0, g, b, t, 0)),
            pl.BlockSpec((1, 1, 1, c, GATE_LANES), lambda b, g, t: (1, g, b, bwd(t), 0)),
            gate_spec, gate_spec,
            pl.BlockSpec((2, c, pk * c), lambda b, g, t: (0, 0, 0)),
            pl.BlockSpec((c, pk * c), lambda b, g, t: (0, 0)),
            pl.BlockSpec(lvm.shape, lambda b, g, t: (0, 0, 0)),
        ],
        out_specs=[xf_spec, xb_spec],
        out_shape=[o_shape, o_shape],
        scratch_shapes=[pltpu.VMEM((2 * hg, HEAD_DIM, HEAD_DIM), _F32)],
        compiler_params=_cp(("arbitrary", "arbitrary", "arbitrary")), name="gdn_scan",
    )(q, k, v, q, k, v, ab_r, ab_r, alog_r, dtb_r, jnp.asarray(tri), jnp.asarray(eye), jnp.asarray(lvm))


def _gdnout_kernel(of_ref, ob_ref, z_ref, nw_ref, y_ref, *, nh):
    hd = HEAD_DIM
    for h in range(nh):
        sl = slice(h * hd, (h + 1) * hd)
        o = of_ref[0, :, sl].astype(_F32) + ob_ref[0, :, sl].astype(_F32)
        ms = jnp.mean(o * o, axis=-1, keepdims=True)
        y = o * lax.rsqrt(ms + NORM_EPS) * nw_ref[...]
        y_ref[0, :, sl] = (y * _silu(z_ref[0, :, sl].astype(_F32))).astype(y_ref.dtype)


def _gdnout(o_fwd, o_bwd, proj, z_off_blocks, norm_w):
    bsz, lt, gw = o_fwd.shape
    nh = gw // HEAD_DIM
    tr = ROW_TILE
    return pl.pallas_call(
        functools.partial(_gdnout_kernel, nh=nh),
        grid=(bsz, lt // tr),
        in_specs=[
            pl.BlockSpec((1, tr, gw), lambda b, i: (b, i, 0)),
            pl.BlockSpec((1, tr, gw), lambda b, i: (b, i, 0)),
            pl.BlockSpec((1, tr, gw), lambda b, i: (b, i, z_off_blocks)),
            pl.BlockSpec((1, HEAD_DIM), lambda b, i: (0, 0)),
        ],
        out_specs=pl.BlockSpec((1, tr, gw), lambda b, i: (b, i, 0)),
        out_shape=jax.ShapeDtypeStruct((bsz, lt, gw), _BF16),
        compiler_params=_cp(("arbitrary", "arbitrary")), name="gdn_out",
    )(o_fwd, o_bwd, proj, norm_w.reshape(1, HEAD_DIM))


def _na_bias_table(rpb, rows):
    wc = NA_WIN_C
    qc = np.arange(GRID_W)[:, None]
    kc = np.arange(GRID_W)[None, :]
    win_start = np.clip(qc - wc // 2, 0, GRID_W - wc)
    valid = (kc >= win_start) & (kc < win_start + wc)
    off = np.clip(kc - qc + wc - 1, 0, 2 * wc - 2)
    ro = np.arange(NA_WIN_R)[:, None] + np.arange(NA_WIN_R)[None, :]
    tab = rpb.astype(_F32)[:, ro][:, :, :, off]
    tab = jnp.where(valid[None, None, None], tab, MASK_VALUE)
    tab = jnp.transpose(tab, (0, 1, 3, 2, 4))
    return tab.reshape(rpb.shape[0], NA_WIN_R, GRID_W, NA_WIN_R * GRID_W)


def _rms_head(t, w):
    return t * lax.rsqrt(jnp.mean(t * t, axis=-1, keepdims=True) + NORM_EPS) * w


def _na_kernel(q_ref, k_ref, v_ref, tab_ref, nw_ref, o_ref, kn_ref, *, hg, lc, rows):
    hd = HEAD_DIM
    w = GRID_W
    nloc = NA_WIN_R * w
    ncc = lc // w
    r = pl.program_id(2)

    @pl.when(r == 0)
    def _():
        for j in range(hg):
            sl = slice(j * hd, (j + 1) * hd)
            kn_ref[:, sl] = _rms_head(k_ref[0, :, sl].astype(_F32), nw_ref[1:2, :]).astype(kn_ref.dtype)

    rr = r - ncc
    sr = jnp.clip(rr - NA_WIN_R // 2, 0, rows - NA_WIN_R)
    start = pl.multiple_of(lc + sr * w, w)
    ro = jnp.clip(sr - rr + NA_WIN_R - 1, 0, NA_WIN_R - 1)

    heads = range(hg)
    sls = [slice(j * hd, (j + 1) * hd) for j in heads]

    def normed_q():
        return [(_rms_head(q_ref[0, :, sls[j]].astype(_F32), nw_ref[0:1, :]) * (hd ** -0.5)).astype(_BF16)
                for j in heads]

    def store(outs):
        for j in heads:
            o_ref[0, :, sls[j]] = outs[j].astype(o_ref.dtype)

    def lane_tile_reduce(op, *arrays):
        tiles = [a[:, i:i + LANES] for a in arrays for i in range(0, a.shape[1], LANES)]
        acc = tiles[0]
        for tl in tiles[1:]:
            acc = op(acc, tl)
        return acc

    @pl.when(r < ncc)
    def _():
        qn = normed_q()
        s_ctx = [_dot_nt(qn[j], kn_ref[0:lc, sls[j]]) for j in heads]
        p = [jnp.exp(s_ctx[j] - jnp.max(s_ctx[j], axis=-1, keepdims=True)) for j in heads]
        den = [jnp.sum(p[j], axis=-1, keepdims=True) for j in heads]
        store([_dot(p[j].astype(_BF16), v_ref[0, 0:lc, sls[j]]) / den[j] for j in heads])

    @pl.when(r >= ncc)
    def _():
        qn = normed_q()

        def scores(j):
            s_loc = _dot_nt(qn[j], kn_ref[pl.ds(start, nloc), sls[j]]) + tab_ref[j, ro]
            return s_loc, _dot_nt(qn[j], kn_ref[0:lc, sls[j]])

        def finish(j, s_loc, s_ctx):
            m = jnp.max(lane_tile_reduce(jnp.maximum, s_loc, s_ctx), axis=-1, keepdims=True)
            p_loc = jnp.exp(s_loc - m)
            p_ctx = jnp.exp(s_ctx - m)
            den = jnp.sum(lane_tile_reduce(jnp.add, p_loc, p_ctx), axis=-1, keepdims=True)
            o = (_dot(p_loc.astype(_BF16), v_ref[0, pl.ds(start, nloc), sls[j]])
                 + _dot(p_ctx.astype(_BF16), v_ref[0, 0:lc, sls[j]]))
            return o / den

        ahead = min(NA_SCORE_LOOKAHEAD, hg)
        pending = [scores(j) for j in range(ahead)]
        outs = []
        for j in heads:
            if j + ahead < hg:
                pending.append(scores(j + ahead))
            outs.append(finish(j, *pending.pop(0)))
        store(outs)


def _na(proj, q_off, k_off, v_off, tab, qk_norm_w, lc, nh, hg):
    bsz, lt, _ = proj.shape
    rows = (lt - lc) // GRID_W
    nhg = nh // hg
    bw = hg * HEAD_DIM
    return pl.pallas_call(
        functools.partial(_na_kernel, hg=hg, lc=lc, rows=rows),
        grid=(bsz, nhg, lt // GRID_W),
        in_specs=[
            pl.BlockSpec((1, GRID_W, bw), lambda b, g, r: (b, r, q_off + g)),
            pl.BlockSpec((1, lt, bw), lambda b, g, r: (b, 0, k_off + g)),
            pl.BlockSpec((1, lt, bw), lambda b, g, r: (b, 0, v_off + g)),
            pl.BlockSpec((hg,) + tab.shape[1:], lambda b, g, r: (g, 0, 0, 0)),
            pl.BlockSpec((2, HEAD_DIM), lambda b, g, r: (0, 0)),
        ],
        out_specs=pl.BlockSpec((1, GRID_W, bw), lambda b, g, r: (b, r, g)),
        out_shape=jax.ShapeDtypeStruct((bsz, lt, nh * HEAD_DIM), _BF16),
        scratch_shapes=[pltpu.VMEM((lt, bw), _BF16)],
        compiler_params=_cp(("arbitrary", "arbitrary", "arbitrary")), name="natten",
    )(proj, proj, proj, tab, qk_norm_w)


def _outproj_kernel(ya_ref, yb_ref, wa_ref, wb_ref, x_ref, ml_ref, mc_ref, o_ref, *, lc, tiles_per_sample, k_gate):
    tm = x_ref.shape[0]
    acc = _dot(ya_ref[...], wa_ref[0]) + _dot(yb_ref[...], wb_ref[0])
    row = (pl.program_id(0) % tiles_per_sample) * tm + lax.broadcasted_iota(jnp.int32, acc.shape, 0)
    gate = jnp.where(row < lc, mc_ref[0, k_gate:k_gate + 1, :], ml_ref[0, k_gate:k_gate + 1, :])
    o_ref[...] = x_ref[...] + gate * acc


def _outproj(ya, yb, w_out, layer, xx, mod, lc, k_gate):
    bsz, lt, d = xx.shape
    t = bsz * lt
    ga = ya.shape[-1]
    gb = yb.shape[-1]
    tm = _pick(lt, (1152, 768, 576, 384, 256, 128))
    tn = _pick(d, (512, 256, 128))
    tps = lt // tm
    out = pl.pallas_call(
        functools.partial(_outproj_kernel, lc=lc, tiles_per_sample=tps, k_gate=k_gate),
        grid=(t // tm, d // tn),
        in_specs=[
            pl.BlockSpec((tm, ga), lambda i, j: (i, 0)),
            pl.BlockSpec((tm, gb), lambda i, j: (i, 0)),
            pl.BlockSpec((1, ga, tn), lambda i, j: (layer, 0, j)),
            pl.BlockSpec((1, gb, tn), lambda i, j: (layer, ga // gb, j)),
            pl.BlockSpec((tm, tn), lambda i, j: (i, j)),
            pl.BlockSpec((1, N_MOD, tn), lambda i, j: (i // tps, 0, j)),
            pl.BlockSpec((1, N_MOD, tn), lambda i, j: (bsz, 0, j)),
        ],
        out_specs=pl.BlockSpec((tm, tn), lambda i, j: (i, j)),
        out_shape=jax.ShapeDtypeStruct((t, d), _F32),
        compiler_params=_cp(("arbitrary", "arbitrary")), name="out_proj",
    )(ya.reshape(t, ga), yb.reshape(t, gb), w_out, w_out, xx.reshape(t, d), mod, mod)
    return out.reshape(bsz, lt, d)


def _route_kernel(lg_ref, br_ref, su_ref, oi_ref, ow_ref, cnt_ref, carry_ref, *, ne):
    epg = ne // N_EXPERT_GROUPS
    tq = lg_ref.shape[1]

    @pl.when(pl.program_id(0) == 0)
    def _():
        carry_ref[...] = jnp.zeros_like(carry_ref)

    sc = _sigmoid(lg_ref[...])
    sel = sc + br_ref[...]
    s = [sel[e:e + 1, :] for e in range(ne)]

    def top2(vals):
        a, b, c_, d_ = vals
        hi1, lo1 = jnp.maximum(a, b), jnp.minimum(a, b)
        hi2, lo2 = jnp.maximum(c_, d_), jnp.minimum(c_, d_)
        return jnp.maximum(hi1, hi2) + jnp.maximum(jnp.maximum(lo1, lo2), jnp.minimum(hi1, hi2))

    gs = [top2(s[g * epg:(g + 1) * epg]) for g in range(N_EXPERT_GROUPS)]
    best = gs[0]
    gsel = jnp.zeros_like(best, dtype=jnp.int32)
    for g in range(1, N_EXPERT_GROUPS):
        better = gs[g] > best
        gsel = jnp.where(better, g, gsel)
        best = jnp.maximum(best, gs[g])
    ing = []
    for j in range(epg):
        v = s[j]
        for g in range(1, N_EXPERT_GROUPS):
            v = jnp.where(gsel == g, s[g * epg + j], v)
        ing.append(v)
    b1 = ing[0]
    i1 = jnp.zeros_like(gsel)
    for j in range(1, epg):
        better = ing[j] > b1
        i1 = jnp.where(better, j, i1)
        b1 = jnp.maximum(b1, ing[j])
    b2 = jnp.full_like(b1, -jnp.inf)
    i2 = jnp.zeros_like(gsel)
    for j in range(epg):
        better = (ing[j] > b2) & (i1 != j)
        i2 = jnp.where(better, j, i2)
        b2 = jnp.where(better, ing[j], b2)
    e1 = gsel * epg + i1
    e2 = gsel * epg + i2

    eio = lax.broadcasted_iota(jnp.int32, (ne, tq), 0)
    oh1 = eio == e1
    oh2 = eio == e2
    w1 = jnp.sum(jnp.where(oh1, sc, 0.0), axis=0, keepdims=True)
    w2 = jnp.sum(jnp.where(oh2, sc, 0.0), axis=0, keepdims=True)
    wsum = w1 + w2
    oh = jnp.where(oh1, 1.0, 0.0) + jnp.where(oh2, 1.0, 0.0)
    rank = _dot(oh.astype(_BF16), su_ref[...]) + carry_ref[:, 0:1]
    r1 = jnp.sum(jnp.where(oh1, rank, 0.0), axis=0, keepdims=True)
    r2 = jnp.sum(jnp.where(oh2, rank, 0.0), axis=0, keepdims=True)
    carry = carry_ref[...] + jnp.sum(oh, axis=1, keepdims=True)
    carry_ref[...] = carry
    cnt_ref[...] = carry

    sub = lax.broadcasted_iota(jnp.int32, (8, tq), 0)
    oi = jnp.where(sub == 0, e1, jnp.where(sub == 1, e2,
         jnp.where(sub == 2, r1.astype(jnp.int32), jnp.where(sub == 3, r2.astype(jnp.int32), 0))))
    oi_ref[...] = oi
    ow_ref[...] = jnp.where(sub == 0, w1 / wsum, jnp.where(sub == 1, w2 / wsum, 0.0))


def _route(logits_t, b_router):
    ne, t = logits_t.shape
    tq = ROW_TILE
    su = np.triu(np.ones((tq, tq), np.float32), 1)
    return pl.pallas_call(
        functools.partial(_route_kernel, ne=ne),
        grid=(t // tq,),
        in_specs=[pl.BlockSpec((ne, tq), lambda i: (0, i)),
                  pl.BlockSpec((ne, 1), lambda i: (0, 0)),
                  pl.BlockSpec((tq, tq), lambda i: (0, 0))],
        out_specs=[pl.BlockSpec((8, tq), lambda i: (0, i)),
                   pl.BlockSpec((8, tq), lambda i: (0, i)),
                   pl.BlockSpec((ne, 128), lambda i: (0, 0))],
        out_shape=[jax.ShapeDtypeStruct((8, t), jnp.int32),
                   jax.ShapeDtypeStruct((8, t), _F32),
                   jax.ShapeDtypeStruct((ne, 128), _F32)],
        scratch_shapes=[pltpu.VMEM((ne, 128), _F32)],
        compiler_params=_cp(("arbitrary",)), name="router",
    )(logits_t, b_router.reshape(ne, 1).astype(_F32), jnp.asarray(su, _BF16))


def _new_weight_block(te_ref, i):
    return (i == 0) | (te_ref[i] != te_ref[jnp.maximum(i - 1, 0)])


def _moe_up_kernel(te_ref, x_ref, wg_ref, wu_ref, h_ref, wg16_ref, wu16_ref):
    i = pl.program_id(1)
    n_valid = te_ref[te_ref.shape[0] - 1]

    @pl.when(_new_weight_block(te_ref, i))
    def _():
        wg16_ref[...] = wg_ref[0, 0].astype(_BF16)
        wu16_ref[...] = wu_ref[0, 0].astype(_BF16)

    @pl.when(i < n_valid)
    def _():
        x = x_ref[...]
        gt = _dot(x, wg16_ref[...])
        up = _dot(x, wu16_ref[...])
        h_ref[...] = (_silu(gt) * up).astype(h_ref.dtype)

    @pl.when(i >= n_valid)
    def _():
        h_ref[...] = jnp.zeros_like(h_ref)


def _moe_down_kernel(te_ref, *refs, n_parts, part_tiles):
    h_refs = refs[:n_parts]
    wd_ref, o_ref, wd16_ref = refs[n_parts:]
    i = pl.program_id(1)
    n_valid = te_ref[te_ref.shape[0] - 1]

    @pl.when(_new_weight_block(te_ref, i))
    def _():
        wd16_ref[...] = wd_ref[0, 0].astype(_BF16)

    for part in range(n_parts):
        @pl.when((i >= part * part_tiles) & (i < jnp.minimum(n_valid, (part + 1) * part_tiles)))
        def _(h_ref=h_refs[part]):
            o_ref[...] = _dot(h_ref[...], wd16_ref[...]).astype(o_ref.dtype)

    @pl.when(i >= n_valid)
    def _():
        o_ref[...] = jnp.zeros_like(o_ref)


def _moe_experts(parts, tile_e, w_gate, w_up, w_down, layer):
    d = parts[0][0].shape[1]
    f = w_gate.shape[-1]
    tm = MOE_TILE
    tf = _pick(f, (MOE_F_TILE, 256, 128))
    td = _pick(d, (MOE_D_TILE, 1024, 512, 256, 128))
    hids = []
    for xs, te_half in parts:
        nth = xs.shape[0] // tm
        hids.append(pl.pallas_call(
            _moe_up_kernel,
            grid_spec=pltpu.PrefetchScalarGridSpec(
                num_scalar_prefetch=1, grid=(f // tf, nth),
                in_specs=[pl.BlockSpec((tm, d), lambda j, i, te: (i, 0)),
                          pl.BlockSpec((1, 1, d, tf), lambda j, i, te: (layer, te[i], 0, j)),
                          pl.BlockSpec((1, 1, d, tf), lambda j, i, te: (layer, te[i], 0, j))],
                out_specs=pl.BlockSpec((tm, tf), lambda j, i, te: (i, j)),
                scratch_shapes=[pltpu.VMEM((d, tf), _BF16), pltpu.VMEM((d, tf), _BF16)]),
            out_shape=jax.ShapeDtypeStruct((nth * tm, f), _BF16),
            compiler_params=_cp(("arbitrary", "arbitrary")), name="moe_up",
        )(te_half, xs, w_gate, w_up))
    n_parts = len(hids)
    part_tiles = hids[0].shape[0] // tm
    nt = n_parts * part_tiles

    def part_spec(part):
        return pl.BlockSpec((tm, f), lambda j, i, te: (jnp.clip(i - part * part_tiles, 0, part_tiles - 1), 0))

    return pl.pallas_call(
        functools.partial(_moe_down_kernel, n_parts=n_parts, part_tiles=part_tiles),
        grid_spec=pltpu.PrefetchScalarGridSpec(
            num_scalar_prefetch=1, grid=(d // td, nt),
            in_specs=[part_spec(part) for part in range(n_parts)]
                     + [pl.BlockSpec((1, 1, f, td), lambda j, i, te: (layer, te[i], 0, j))],
            out_specs=pl.BlockSpec((tm, td), lambda j, i, te: (i, j)),
            scratch_shapes=[pltpu.VMEM((f, td), _BF16)]),
        out_shape=jax.ShapeDtypeStruct((nt * tm, d), _BF16),
        compiler_params=_cp(("arbitrary", "arbitrary")), name="moe_down",
    )(tile_e, *hids, w_down)


def _comb_kernel(x_ref, y0_ref, y1_ref, w_ref, mod_ref, o_ref, *, k_gate):
    y = y0_ref[0].astype(_F32) * w_ref[0, :, 0:1] + y1_ref[0].astype(_F32) * w_ref[0, :, 1:2]
    o_ref[0] = x_ref[0] + mod_ref[0, k_gate:k_gate + 1, :] * y


def _comb_norm_kernel(x_ref, y0_ref, y1_ref, w_ref, mod_ref, nw_ref, modn_ref, o_ref, a_ref, *, k_gate):
    y = y0_ref[0].astype(_F32) * w_ref[0, :, 0:1] + y1_ref[0].astype(_F32) * w_ref[0, :, 1:2]
    o = x_ref[0] + mod_ref[0, k_gate:k_gate + 1, :] * y
    o_ref[0] = o
    a_ref[0] = _modnorm(o, nw_ref[...], modn_ref[0], 0, 1).astype(a_ref.dtype)


def _combine(xx, y0, y1, w01, mod, lc, k_gate, latent_only, next_norm=None):
    bsz, lt, d = xx.shape
    tr = ROW_TILE
    skip = lc // tr if latent_only else 0
    nt = lt // tr - skip
    row = lambda b, i: (b, i + skip, 0)
    yrow = lambda b, i: (b, i, 0)
    mrow = _mod_row(bsz, lc // tr)
    mod_spec = pl.BlockSpec((1, N_MOD, d), lambda b, i: mrow(b, i + skip))
    in_specs = [pl.BlockSpec((1, tr, d), row), pl.BlockSpec((1, tr, d), yrow), pl.BlockSpec((1, tr, d), yrow),
                pl.BlockSpec((1, tr, 2), yrow), mod_spec]
    o_spec = pl.BlockSpec((1, tr, d), yrow)
    o_shape = jax.ShapeDtypeStruct((bsz, nt * tr, d), _F32)
    if next_norm is None:
        return pl.pallas_call(
            functools.partial(_comb_kernel, k_gate=k_gate),
            grid=(bsz, nt), in_specs=in_specs, out_specs=o_spec, out_shape=o_shape,
            compiler_params=_cp(("arbitrary", "arbitrary")), name="moe_combine",
        )(xx, y0, y1, w01, mod)
    nw_next, mod_next = next_norm
    return pl.pallas_call(
        functools.partial(_comb_norm_kernel, k_gate=k_gate),
        grid=(bsz, nt),
        in_specs=in_specs + [pl.BlockSpec((1, d), lambda b, i: (0, 0)), mod_spec],
        out_specs=[o_spec, o_spec],
        out_shape=[o_shape, jax.ShapeDtypeStruct((bsz, nt * tr, d), _BF16)],
        compiler_params=_cp(("arbitrary", "arbitrary")), name="moe_combine_norm",
    )(xx, y0, y1, w01, mod, nw_next.reshape(1, d), mod_next)


def _moe(f, logits_t, b_router, w_gate, w_up, w_down, layer):
    t, d = f.shape
    ne = w_gate.shape[1]
    tm = MOE_TILE
    oi, ow, cnt = _route(logits_t, b_router)
    e01 = oi[0:2]
    r01 = oi[2:4]
    counts = cnt[:, 0].astype(jnp.int32)
    tiles = (counts + tm - 1) // tm
    tile_end = jnp.cumsum(tiles)
    off = (tile_end - tiles) * tm
    first = jnp.sum(jnp.where(e01[None] == jnp.arange(ne, dtype=jnp.int32)[:, None, None],
                              off[:, None, None], 0), axis=0)
    pos = first + r01
    nt = (t * TOP_K) // tm + ne
    p = nt * tm
    tok = jnp.broadcast_to(jnp.arange(t, dtype=jnp.int32)[None, :], (TOP_K, t))
    src = (jnp.arange(p, dtype=jnp.int32) % t).at[pos.reshape(-1)].set(tok.reshape(-1))
    n_valid = tile_end[-1]
    tile_idx = jnp.arange(nt, dtype=jnp.int32)
    tile_e = jnp.sum((tile_end[None, :] <= jnp.minimum(tile_idx, n_valid - 1)[:, None]).astype(jnp.int32), axis=1)
    assert nt % MOE_GATHER_PARTS == 0
    part_tiles = nt // MOE_GATHER_PARTS
    parts = []
    for lo in range(0, nt, part_tiles):
        hi = lo + part_tiles
        n_val = jnp.clip(n_valid - lo, 0, part_tiles).astype(jnp.int32)
        parts.append((f.at[src[lo * tm:hi * tm]].get(mode="promise_in_bounds"),
                      jnp.concatenate([tile_e[lo:hi], n_val[None]])))
    tile_e = jnp.concatenate([tile_e, n_valid[None].astype(jnp.int32)])
    out = _moe_experts(parts, tile_e, w_gate, w_up, w_down, layer)
    y0 = out.at[pos[0]].get(mode="promise_in_bounds")
    y1 = out.at[pos[1]].get(mode="promise_in_bounds")
    return y0, y1, jnp.transpose(ow[0:2])


def kernel(x, c, ctx, c_ctx, norm1_w, norm2_w, w_ada, b_ada, w_in, conv_w, a_log, dt_bias, gdn_norm_w,
           qk_norm_w, na_rpb, w_out, w_router, b_router, w_gate, w_up, w_down):
    bsz, s_len, d = x.shape
    lc = ctx.shape[1]
    lt = lc + s_len
    t = bsz * lt
    depth = w_ada.shape[0]
    gw = d // 2
    naw = d - gw
    nh_a = gw // HEAD_DIM
    nh_b = naw // HEAD_DIM
    rows = s_len // GRID_W
    assert s_len % GRID_W == 0 and rows >= NA_WIN_R
    assert lc % ROW_TILE == 0 and s_len % ROW_TILE == 0
    hg_a = min(GDN_HEADS_PER_STEP, nh_a)
    hg_b = min(NA_HEADS_PER_STEP, nh_b)
    nhg_a = nh_a // hg_a

    xx = jnp.concatenate([ctx, x], axis=1)
    mp = -(-(bsz + 1) // 8) * 8
    cc = jnp.zeros((mp, d), _F32).at[:bsz].set(c).at[bsz].set(c_ctx)
    mod_all = _ada(cc, w_ada, b_ada).reshape(depth, mp, N_MOD, d)
    w_router_t = jnp.transpose(w_router).astype(_F32)

    col_b = 4 * gw + 4 * nh_a
    w_in16 = jnp.swapaxes(w_in, 1, 2).astype(_BF16)
    w_b = w_in16[:, col_b:, :]
    w_ab = jnp.zeros((depth, LANES, d), _BF16).at[:, :4 * nh_a, :].set(w_in16[:, 4 * gw:col_b, :])
    w_out16 = w_out.astype(_BF16)

    for layer in range(depth):
        last = layer == depth - 1
        mod = mod_all[layer]

        if layer == 0:
            a = _norm(xx, norm1_w[layer], mod, lc, 0, 1)
        a2 = a.reshape(t, d)
        proj_a = _matmul(a2, w_in16, layer, 4 * gw, _BF16).reshape(bsz, lt, 4 * gw)
        proj_b = _matmul(a2, w_b, layer, 3 * naw, _BF16).reshape(bsz, lt, 3 * naw)
        ab = _matmul(a2, w_ab, layer, LANES, _F32)[:, :4 * nh_a].reshape(bsz, lt, 2, 2, nhg_a, hg_a)

        q, k, v = _gprep(proj_a, conv_w[layer], lc, nh_a)
        lane_pad = GATE_LANES - 2 * hg_a
        ab_r = jnp.transpose(ab, (3, 4, 0, 1, 2, 5)).reshape(2, nhg_a, bsz, lt, 2 * hg_a)
        ab_r = jnp.pad(ab_r, ((0, 0),) * 4 + ((0, lane_pad),))
        zpad = jnp.zeros((2, nhg_a, 1, hg_a), _F32)
        zlane = jnp.zeros((2, nhg_a, 1, lane_pad), _F32)
        alog_r = jnp.concatenate([zpad, a_log[layer].astype(_F32).reshape(2, nhg_a, 1, hg_a), zlane], axis=-1)
        dtb_r = jnp.concatenate([zpad, dt_bias[layer].astype(_F32).reshape(2, nhg_a, 1, hg_a), zlane], axis=-1)
        o_fwd, o_bwd = _gdn(q, k, v, ab_r, alog_r, dtb_r, lc, hg_a)
        ya = _gdnout(o_fwd, o_bwd, proj_a, 3, gdn_norm_w[layer])

        tab = _na_bias_table(na_rpb[layer], rows)
        nb = naw // (hg_b * HEAD_DIM)
        yb = _na(proj_b, 0, nb, 2 * nb, tab, qk_norm_w[layer].astype(_F32), lc, nh_b, hg_b)

        xx = _outproj(ya, yb, w_out16, layer, xx, mod, lc, 2)

        f, logits_t = _norm(xx, norm2_w[layer], mod, lc, 3, 4, w_router_t, latent_only=last)
        rows_moe = f.shape[1]
        y0, y1, w01 = _moe(f.reshape(bsz * rows_moe, d), logits_t, b_router, w_gate, w_up, w_down, layer)
        ys = (y0.reshape(bsz, rows_moe, d), y1.reshape(bsz, rows_moe, d), w01.reshape(bsz, rows_moe, 2))
        if last:
            xx = _combine(xx, *ys, mod, lc, 5, True)
        else:
            xx, a = _combine(xx, *ys, mod, lc, 5, False, next_norm=(norm1_w[layer + 1], mod_all[layer + 1]))
    return xx
```

```python
import functools
import math

import numpy as np
import jax
import jax.numpy as jnp
from jax import lax
from jax.experimental import pallas as pl
from jax.experimental.pallas import tpu as pltpu

HEAD_DIM = 128
GRID_W = 64
GDN_CHUNK = 64
SHORT_CONV = 5
NA_WIN_R = 8
NA_WIN_C = 16
N_EXPERT_GROUPS = 4
TOP_K = 2
N_MOD = 6
NORM_EPS = 1e-6
MASK_VALUE = -1e30

ROW_TILE = 256
MOE_TILE = 256
MOE_F_TILE = 512
MOE_D_TILE = 4096
MOE_GATHER_PARTS = 2
LANES = 128
GATE_LANES = 128
CONV_GAP = 8
GDN_HEADS_PER_STEP = 16
GDN_PACK = 4
NA_HEADS_PER_STEP = 8
NA_SCORE_LOOKAHEAD = 3
VMEM_LIMIT = 56 * 1024 * 1024

_F32 = jnp.float32
_BF16 = jnp.bfloat16
_HI = lax.Precision.HIGHEST


def _cp(sem, vmem=VMEM_LIMIT, flags=None):
    return pltpu.CompilerParams(dimension_semantics=sem, vmem_limit_bytes=vmem, flags=flags)


def _pick(n, cands):
    for c in cands:
        if n % c == 0:
            return c
    raise ValueError(f"no tile for {n} in {cands}")


def _dot(a, b):
    return jnp.dot(a, b, preferred_element_type=_F32)


def _dot_nt(a, b, precision=None):
    return lax.dot_general(a, b, (((1,), (1,)), ((), ())), preferred_element_type=_F32, precision=precision)


def _dot_tn(a, b):
    return lax.dot_general(a, b, (((0,), (0,)), ((), ())), preferred_element_type=_F32)


def _sigmoid(x):
    return 1.0 / (1.0 + jnp.exp(-x))


def _silu(x):
    return x * _sigmoid(x)


def _ada_kernel(cc_ref, w_ref, b_ref, o_ref):
    a = _silu(cc_ref[...]).astype(_BF16)
    o_ref[0] = _dot(a, w_ref[0].astype(_BF16)) + b_ref[0]


def _ada(cc, w_ada, b_ada):
    nl, d, n = w_ada.shape
    mp = cc.shape[0]
    tn = _pick(n, (512, 256, 128))
    return pl.pallas_call(
        _ada_kernel,
        grid=(nl, n // tn),
        in_specs=[
            pl.BlockSpec((mp, d), lambda l, j: (0, 0)),
            pl.BlockSpec((1, d, tn), lambda l, j: (l, 0, j)),
            pl.BlockSpec((1, 1, tn), lambda l, j: (l, 0, j)),
        ],
        out_specs=pl.BlockSpec((1, mp, tn), lambda l, j: (l, 0, j)),
        out_shape=jax.ShapeDtypeStruct((nl, mp, n), _F32),
        compiler_params=_cp(("arbitrary", "arbitrary")),
        name="ada_mod",
    )(cc, w_ada, b_ada.reshape(nl, 1, n))


def _modnorm(x, nw, mod, k_shift, k_scale):
    ms = jnp.mean(x * x, axis=-1, keepdims=True)
    y = x * lax.rsqrt(ms + NORM_EPS) * nw
    return y * (1.0 + mod[k_scale:k_scale + 1, :]) + mod[k_shift:k_shift + 1, :]


def _norm_kernel(x_ref, nw_ref, mod_ref, o_ref, *, k_shift, k_scale):
    y = _modnorm(x_ref[0], nw_ref[...], mod_ref[0], k_shift, k_scale)
    o_ref[0] = y.astype(o_ref.dtype)


def _norm_router_kernel(x_ref, nw_ref, mod_ref, wr_ref, o_ref, lg_ref, *, k_shift, k_scale):
    y = _modnorm(x_ref[0], nw_ref[...], mod_ref[0], k_shift, k_scale)
    o_ref[0] = y.astype(o_ref.dtype)
    lg_ref[...] = _dot_nt(wr_ref[...], y, precision=_HI)


def _mod_row(bsz, n_ctx_tiles):
    return lambda b, i: (jnp.where(i < n_ctx_tiles, bsz, b), 0, 0)


def _norm(xx, nw, mod, lc, k_shift, k_scale, w_router_t=None, latent_only=False):
    bsz, lt, d = xx.shape
    tr = ROW_TILE
    skip = lc // tr if latent_only else 0
    nt = lt // tr - skip
    mrow = _mod_row(bsz, lc // tr)
    in_specs = [
        pl.BlockSpec((1, tr, d), lambda b, i: (b, i + skip, 0)),
        pl.BlockSpec((1, d), lambda b, i: (0, 0)),
        pl.BlockSpec((1, N_MOD, d), lambda b, i: mrow(b, i + skip)),
    ]
    a_spec = pl.BlockSpec((1, tr, d), lambda b, i: (b, i, 0))
    a_shape = jax.ShapeDtypeStruct((bsz, nt * tr, d), _BF16)
    if w_router_t is None:
        return pl.pallas_call(
            functools.partial(_norm_kernel, k_shift=k_shift, k_scale=k_scale),
            grid=(bsz, nt), in_specs=in_specs, out_specs=a_spec, out_shape=a_shape,
            compiler_params=_cp(("arbitrary", "arbitrary")), name="modnorm",
        )(xx, nw.reshape(1, d), mod)
    ne = w_router_t.shape[0]
    return pl.pallas_call(
        functools.partial(_norm_router_kernel, k_shift=k_shift, k_scale=k_scale),
        grid=(bsz, nt),
        in_specs=in_specs + [pl.BlockSpec((ne, d), lambda b, i: (0, 0))],
        out_specs=[a_spec, pl.BlockSpec((ne, tr), lambda b, i: (0, b * nt + i))],
        out_shape=[a_shape, jax.ShapeDtypeStruct((ne, bsz * nt * tr), _F32)],
        compiler_params=_cp(("arbitrary", "arbitrary")), name="modnorm_router",
    )(xx, nw.reshape(1, d), mod, w_router_t)


def _mm_kernel(a_ref, wt_ref, o_ref):
    o_ref[...] = _dot_nt(a_ref[...], wt_ref[0]).astype(o_ref.dtype)


def _matmul(a, wt, layer, n, out_dtype):
    m, k = a.shape
    tm = _pick(m, (1024, 512, 256, 128))
    tn = _pick(n, (1024, 512, 256, 128))
    return pl.pallas_call(
        _mm_kernel,
        grid=(n // tn, m // tm),
        in_specs=[pl.BlockSpec((tm, k), lambda j, i: (i, 0)),
                  pl.BlockSpec((1, tn, k), lambda j, i: (layer, j, 0))],
        out_specs=pl.BlockSpec((tm, tn), lambda j, i: (i, j)),
        out_shape=jax.ShapeDtypeStruct((m, n), out_dtype),
        compiler_params=_cp(("arbitrary", "arbitrary")), name="matmul",
    )(a, wt)


def _conv_silu(pad_ref, x, w, lc):
    lt = x.shape[0]
    g = CONV_GAP
    pad = SHORT_CONV // 2
    zeros = jnp.zeros((g, x.shape[1]), _F32)
    pad_ref[0:g] = zeros
    pad_ref[g:g + lc] = x[0:lc]
    pad_ref[g + lc:2 * g + lc] = zeros
    pad_ref[2 * g + lc:2 * g + lt] = x[lc:lt]
    pad_ref[2 * g + lt:3 * g + lt] = zeros
    acc = None
    for j in range(SHORT_CONV):
        term = pad_ref[g + j - pad:2 * g + lt + j - pad] * w[j:j + 1, :]
        acc = term if acc is None else acc + term
    y = _silu(acc)
    return jnp.concatenate([y[0:lc], y[lc + g:lt + g]], axis=0)


def _l2n(t):
    return t * lax.rsqrt(jnp.sum(t * t, axis=-1, keepdims=True) + NORM_EPS)


def _gprep_kernel(q_ref, k_ref, v_ref, wq_ref, wk_ref, wv_ref, qo_ref, ko_ref, vo_ref, pad_ref, *, lc):
    q = _conv_silu(pad_ref, q_ref[0].astype(_F32), wq_ref[...], lc)
    qo_ref[0] = (_l2n(q) * (HEAD_DIM ** -0.5)).astype(qo_ref.dtype)
    k = _conv_silu(pad_ref, k_ref[0].astype(_F32), wk_ref[...], lc)
    ko_ref[0] = _l2n(k).astype(ko_ref.dtype)
    v = _conv_silu(pad_ref, v_ref[0].astype(_F32), wv_ref[...], lc)
    vo_ref[0] = v.astype(vo_ref.dtype)


def _gprep(proj, conv_w, lc, nh):
    bsz, lt, _ = proj.shape
    hd = HEAD_DIM

    def xs(off):
        return pl.BlockSpec((1, lt, hd), lambda b, h: (b, 0, off + h))

    def ws(off):
        return pl.BlockSpec((SHORT_CONV, hd), lambda b, h: (0, off + h))

    o_spec = pl.BlockSpec((1, lt, hd), lambda b, h: (b, 0, h))
    o_shape = jax.ShapeDtypeStruct((bsz, lt, nh * hd), _BF16)
    return pl.pallas_call(
        functools.partial(_gprep_kernel, lc=lc),
        grid=(bsz, nh),
        in_specs=[xs(0), xs(nh), xs(2 * nh), ws(0), ws(nh), ws(2 * nh)],
        out_specs=[o_spec, o_spec, o_spec],
        out_shape=[o_shape, o_shape, o_shape],
        scratch_shapes=[pltpu.VMEM((lt + 3 * CONV_GAP, hd), _F32)],
        compiler_params=_cp(("arbitrary", "arbitrary")), name="gdn_prep",
    )(proj, proj, proj, conv_w, conv_w, conv_w)


def _gdn_masks():
    c = GDN_CHUNK
    i = np.arange(c)[:, None]
    j = np.arange(c)[None, :]
    tri = np.stack([(i >= j), (i <= j)]).astype(np.float32)
    eye = np.eye(c, dtype=np.float32)
    lv = [(i // 2 == j // 2)]
    s = 2
    while s < c:
        lv.append((i // (2 * s) == j // (2 * s)) & (i // s != j // s))
        s *= 2
    return tri, eye, np.stack(lv).astype(np.float32)


def _gdn_kernel(qf_ref, kf_ref, vf_ref, qb_ref, kb_ref, vb_ref, abf_ref, abb_ref, al_ref, dtb_ref, tri_ref, eye_ref,
                lv_ref, of_ref, ob_ref, s_ref, *, hg, pk):
    c = GDN_CHUNK
    hd = HEAD_DIM
    t = pl.program_id(2)

    @pl.when(t == 0)
    def _():
        s_ref[...] = jnp.zeros_like(s_ref)

    nd = 2
    wl = pk * c
    q_refs, k_refs, v_refs, o_refs = (qf_ref, qb_ref), (kf_ref, kb_ref), (vf_ref, vb_ref), (of_ref, ob_ref)
    eye = eye_ref[...]
    tri_d = [tri_ref[dr] for dr in range(nd)]
    strict_d = [tri_d[dr] - eye for dr in range(nd)]
    beta_d, gc_d, gtot_d = [], [], []
    for dr, ab_ref in enumerate((abf_ref, abb_ref)):
        ab = ab_ref[0, 0, 0]
        z = ab + dtb_ref[dr, 0]
        softplus = jnp.maximum(z, 0.0) + jnp.log(1.0 + jnp.exp(-jnp.abs(z)))
        g = -jnp.exp(al_ref[dr, 0]) * softplus
        beta_d.append(_sigmoid(ab))
        gc_d.append(jnp.dot(tri_d[dr][:, :c], g, preferred_element_type=_F32, precision=_HI))
        gtot_d.append(jnp.sum(g, axis=0, keepdims=True))
    n_lv = lv_ref.shape[0]
    gpd = hg // pk
    heads = range(nd * hg)
    groups = range(nd * gpd)
    members = range(pk)
    sls = [slice(j * hd, (j + 1) * hd) for j in range(hg)]
    gsl = [slice(gi * pk * hd, (gi + 1) * pk * hd) for gi in range(gpd)]
    tri = [tri_d[gi // gpd] for gi in groups]
    strict = [strict_d[gi // gpd] for gi in groups]
    lane_blk = lax.broadcasted_iota(jnp.int32, (c, wl), 1) // c
    head_blk = lax.broadcasted_iota(jnp.int32, (c, pk * hd), 1) // hd

    def block_diag(x):
        return jnp.concatenate([jnp.where(lane_blk == a, x, 0.0) for a in members], axis=0).astype(_BF16)

    def per_member_cols(vals, width):
        return jnp.concatenate([jnp.broadcast_to(v, (c, width)) for v in vals], axis=1)

    b = [beta_d[j // hg][:, j % hg:j % hg + 1] for j in heads]
    gj = [gc_d[j // hg][:, hg + j % hg:hg + j % hg + 1] for j in heads]
    gt = [gtot_d[j // hg][:, hg + j % hg:hg + j % hg + 1] for j in heads]
    eg = [jnp.exp(gj[j]) for j in heads]
    k16 = [k_refs[gi // gpd][0, :, gsl[gi % gpd]] for gi in groups]
    q16 = [q_refs[gi // gpd][0, :, gsl[gi % gpd]] for gi in groups]
    kf = [k16[gi].astype(_F32) for gi in groups]
    kbeta = [kf[gi] * per_member_cols([b[gi * pk + a] for a in members], hd) for gi in groups]
    decay = []
    for gi in groups:
        gcol = jnp.broadcast_to(gj[gi * pk], (c, wl))
        for a in members[1:]:
            gcol = jnp.where(lane_blk == a, gj[gi * pk + a], gcol)
        grow = jnp.sum(gcol * eye, axis=0, keepdims=True)
        decay.append(jnp.exp(jnp.where(tri[gi] > 0.0, gcol - grow, -jnp.inf)))
    kq = []
    for gi in groups:
        k_bd = jnp.concatenate([jnp.where(head_blk == a, kf[gi], 0.0) for a in members], axis=0).astype(_BF16)
        kq.append(_dot_nt(jnp.concatenate([kbeta[gi].astype(_BF16), q16[gi]], axis=0), k_bd))
    lmat = [kq[gi][:c] * decay[gi] * strict[gi] for gi in groups]
    qk = [kq[gi][c:] * decay[gi] for gi in groups]
    dinv = [eye - lmat[gi] * lv_ref[0] for gi in groups]
    for lv in range(1, n_lv):
        wmat = [_dot((lmat[gi] * lv_ref[lv]).astype(_BF16), block_diag(dinv[gi])) for gi in groups]
        dinv = [dinv[gi] - _dot(dinv[gi].astype(_BF16), block_diag(wmat[gi])) for gi in groups]
    uw = []
    for gi in groups:
        vf = v_refs[gi // gpd][0, :, gsl[gi % gpd]].astype(_F32)
        rhs = jnp.concatenate(
            [jnp.concatenate([vf[:, a * hd:(a + 1) * hd] * b[gi * pk + a],
                              kbeta[gi][:, a * hd:(a + 1) * hd] * eg[gi * pk + a]], axis=1) for a in members],
            axis=0).astype(_BF16)
        uw.append(_dot(block_diag(dinv[gi]), rhs))
    s = [s_ref[j] for j in heads]
    s16 = [s[j].astype(_BF16) for j in heads]
    q_dec = [(q16[j // pk][:, (j % pk) * hd:(j % pk + 1) * hd].astype(_F32) * eg[j]).astype(_BF16) for j in heads]
    k_dec = [(kf[j // pk][:, (j % pk) * hd:(j % pk + 1) * hd] * jnp.exp(gt[j] - gj[j])).astype(_BF16) for j in heads]
    uw_h = [uw[j // pk][(j % pk) * c:(j % pk + 1) * c] for j in heads]
    ws = [_dot(jnp.concatenate([uw_h[j][:, hd:].astype(_BF16), q_dec[j]], axis=0), s16[j]) for j in heads]
    v16 = [(uw_h[j][:, :hd] - ws[j][:c]).astype(_BF16) for j in heads]
    intra = [_dot(block_diag(qk[gi]), jnp.concatenate([v16[gi * pk + a] for a in members], axis=0))
             for gi in groups]
    outs = [ws[j][c:] + intra[j // pk][(j % pk) * c:(j % pk + 1) * c] for j in heads]
    states = [s[j] * jnp.exp(gt[j]) + _dot_tn(k_dec[j], v16[j]) for j in heads]

    for j in heads:
        o_refs[j // hg][0, :, sls[j % hg]] = outs[j].astype(of_ref.dtype)
    for j in heads:
        s_ref[j] = states[j]


def _gdn(q, k, v, ab_r, alog_r, dtb_r, lc, hg):
    bsz, lt, gw = q.shape
    nh = gw // HEAD_DIM
    nhg = nh // hg
    c = GDN_CHUNK
    nc_c = lc // c
    nc = lt // c
    pk = min(GDN_PACK, hg)
    assert hg % pk == 0 and (pk * c) % LANES == 0
    tri, eye, lvm = (np.tile(m, pk) for m in _gdn_masks())

    def bwd(t):
        return jnp.where(t < nc_c, nc_c - 1 - t, nc - 1 + nc_c - t)

    bw = hg * HEAD_DIM
    xf_spec = pl.BlockSpec((1, c, bw), lambda b, g, t: (b, t, g))
    xb_spec = pl.BlockSpec((1, c, bw), lambda b, g, t: (b, bwd(t), g))
    gate_spec = pl.BlockSpec((2, 1, 1, GATE_LANES), lambda b, g, t: (0, g, 0, 0))
    o_shape = jax.ShapeDtypeStruct((bsz, lt, gw), _BF16)
    return pl.pallas_call(
        functools.partial(_gdn_kernel, hg=hg, pk=pk),
        grid=(bsz, nhg, nc),
        in_specs=[
            xf_spec, xf_spec, xf_spec, xb_spec, xb_spec, xb_spec,
            pl.BlockSpec((1, 1, 1, c, GATE_LANES), lambda b, g, t: (0, g, b, t, 0)),
            pl.BlockSpec((1, 1, 1, c, GATE_LANES), lambda b, g, t: (1, g, b, bwd(t), 0)),
            gate_spec, gate_spec,
            pl.BlockSpec((2, c, pk * c), lambda b, g, t: (0, 0, 0)),
            pl.BlockSpec((c, pk * c), lambda b, g, t: (0, 0)),
            pl.BlockSpec(lvm.shape, lambda b, g, t: (0, 0, 0)),
        ],
        out_specs=[xf_spec, xb_spec],
        out_shape=[o_shape, o_shape],
        scratch_shapes=[pltpu.VMEM((2 * hg, HEAD_DIM, HEAD_DIM), _F32)],
        compiler_params=_cp(("arbitrary", "arbitrary", "arbitrary")), name="gdn_scan",
    )(q, k, v, q, k, v, ab_r, ab_r, alog_r, dtb_r, jnp.asarray(tri), jnp.asarray(eye), jnp.asarray(lvm))


def _gdnout_kernel(of_ref, ob_ref, z_ref, nw_ref, y_ref, *, nh):
    hd = HEAD_DIM
    for h in range(nh):
        sl = slice(h * hd, (h + 1) * hd)
        o = of_ref[0, :, sl].astype(_F32) + ob_ref[0, :, sl].astype(_F32)
        ms = jnp.mean(o * o, axis=-1, keepdims=True)
        y = o * lax.rsqrt(ms + NORM_EPS) * nw_ref[...]
        y_ref[0, :, sl] = (y * _silu(z_ref[0, :, sl].astype(_F32))).astype(y_ref.dtype)


def _gdnout(o_fwd, o_bwd, proj, z_off_blocks, norm_w):
    bsz, lt, gw = o_fwd.shape
    nh = gw // HEAD_DIM
    tr = ROW_TILE
    return pl.pallas_call(
        functools.partial(_gdnout_kernel, nh=nh),
        grid=(bsz, lt // tr),
        in_specs=[
            pl.BlockSpec((1, tr, gw), lambda b, i: (b, i, 0)),
            pl.BlockSpec((1, tr, gw), lambda b, i: (b, i, 0)),
            pl.BlockSpec((1, tr, gw), lambda b, i: (b, i, z_off_blocks)),
            pl.BlockSpec((1, HEAD_DIM), lambda b, i: (0, 0)),
        ],
        out_specs=pl.BlockSpec((1, tr, gw), lambda b, i: (b, i, 0)),
        out_shape=jax.ShapeDtypeStruct((bsz, lt, gw), _BF16),
        compiler_params=_cp(("arbitrary", "arbitrary")), name="gdn_out",
    )(o_fwd, o_bwd, proj, norm_w.reshape(1, HEAD_DIM))


def _na_bias_table(rpb, rows):
    wc = NA_WIN_C
    qc = np.arange(GRID_W)[:, None]
    kc = np.arange(GRID_W)[None, :]
    win_start = np.clip(qc - wc // 2, 0, GRID_W - wc)
    valid = (kc >= win_start) & (kc < win_start + wc)
    off = np.clip(kc - qc + wc - 1, 0, 2 * wc - 2)
    ro = np.arange(NA_WIN_R)[:, None] + np.arange(NA_WIN_R)[None, :]
    tab = rpb.astype(_F32)[:, ro][:, :, :, off]
    tab = jnp.where(valid[None, None, None], tab, MASK_VALUE)
    tab = jnp.transpose(tab, (0, 1, 3, 2, 4))
    return tab.reshape(rpb.shape[0], NA_WIN_R, GRID_W, NA_WIN_R * GRID_W)


def _rms_head(t, w):
    return t * lax.rsqrt(jnp.mean(t * t, axis=-1, keepdims=True) + NORM_EPS) * w


def _na_kernel(q_ref, k_ref, v_ref, tab_ref, nw_ref, o_ref, kn_ref, *, hg, lc, rows):
    hd = HEAD_DIM
    w = GRID_W
    nloc = NA_WIN_R * w
    ncc = lc // w
    r = pl.program_id(2)

    @pl.when(r == 0)
    def _():
        for j in range(hg):
            sl = slice(j * hd, (j + 1) * hd)
            kn_ref[:, sl] = _rms_head(k_ref[0, :, sl].astype(_F32), nw_ref[1:2, :]).astype(kn_ref.dtype)

    rr = r - ncc
    sr = jnp.clip(rr - NA_WIN_R // 2, 0, rows - NA_WIN_R)
    start = pl.multiple_of(lc + sr * w, w)
    ro = jnp.clip(sr - rr + NA_WIN_R - 1, 0, NA_WIN_R - 1)

    heads = range(hg)
    sls = [slice(j * hd, (j + 1) * hd) for j in heads]

    def normed_q():
        return [(_rms_head(q_ref[0, :, sls[j]].astype(_F32), nw_ref[0:1, :]) * (hd ** -0.5)).astype(_BF16)
                for j in heads]

    def store(outs):
        for j in heads:
            o_ref[0, :, sls[j]] = outs[j].astype(o_ref.dtype)

    def lane_tile_reduce(op, *arrays):
        tiles = [a[:, i:i + LANES] for a in arrays for i in range(0, a.shape[1], LANES)]
        acc = tiles[0]
        for tl in tiles[1:]:
            acc = op(acc, tl)
        return acc

    @pl.when(r < ncc)
    def _():
        qn = normed_q()
        s_ctx = [_dot_nt(qn[j], kn_ref[0:lc, sls[j]]) for j in heads]
        p = [jnp.exp(s_ctx[j] - jnp.max(s_ctx[j], axis=-1, keepdims=True)) for j in heads]
        den = [jnp.sum(p[j], axis=-1, keepdims=True) for j in heads]
        store([_dot(p[j].astype(_BF16), v_ref[0, 0:lc, sls[j]]) / den[j] for j in heads])

    @pl.when(r >= ncc)
    def _():
        qn = normed_q()

        def scores(j):
            s_loc = _dot_nt(qn[j], kn_ref[pl.ds(start, nloc), sls[j]]) + tab_ref[j, ro]
            return s_loc, _dot_nt(qn[j], kn_ref[0:lc, sls[j]])

        def finish(j, s_loc, s_ctx):
            m = jnp.max(lane_tile_reduce(jnp.maximum, s_loc, s_ctx), axis=-1, keepdims=True)
            p_loc = jnp.exp(s_loc - m)
            p_ctx = jnp.exp(s_ctx - m)
            den = jnp.sum(lane_tile_reduce(jnp.add, p_loc, p_ctx), axis=-1, keepdims=True)
            o = (_dot(p_loc.astype(_BF16), v_ref[0, pl.ds(start, nloc), sls[j]])
                 + _dot(p_ctx.astype(_BF16), v_ref[0, 0:lc, sls[j]]))
            return o / den

        ahead = min(NA_SCORE_LOOKAHEAD, hg)
        pending = [scores(j) for j in range(ahead)]
        outs = []
        for j in heads:
            if j + ahead < hg:
                pending.append(scores(j + ahead))
            outs.append(finish(j, *pending.pop(0)))
        store(outs)


def _na(proj, q_off, k_off, v_off, tab, qk_norm_w, lc, nh, hg):
    bsz, lt, _ = proj.shape
    rows = (lt - lc) // GRID_W
    nhg = nh // hg
    bw = hg * HEAD_DIM
    return pl.pallas_call(
        functools.partial(_na_kernel, hg=hg, lc=lc, rows=rows),
        grid=(bsz, nhg, lt // GRID_W),
        in_specs=[
            pl.BlockSpec((1, GRID_W, bw), lambda b, g, r: (b, r, q_off + g)),
            pl.BlockSpec((1, lt, bw), lambda b, g, r: (b, 0, k_off + g)),
            pl.BlockSpec((1, lt, bw), lambda b, g, r: (b, 0, v_off + g)),
            pl.BlockSpec((hg,) + tab.shape[1:], lambda b, g, r: (g, 0, 0, 0)),
            pl.BlockSpec((2, HEAD_DIM), lambda b, g, r: (0, 0)),
        ],
        out_specs=pl.BlockSpec((1, GRID_W, bw), lambda b, g, r: (b, r, g)),
        out_shape=jax.ShapeDtypeStruct((bsz, lt, nh * HEAD_DIM), _BF16),
        scratch_shapes=[pltpu.VMEM((lt, bw), _BF16)],
        compiler_params=_cp(("arbitrary", "arbitrary", "arbitrary")), name="natten",
    )(proj, proj, proj, tab, qk_norm_w)


def _outproj_kernel(ya_ref, yb_ref, wa_ref, wb_ref, x_ref, ml_ref, mc_ref, o_ref, *, lc, tiles_per_sample, k_gate):
    tm = x_ref.shape[0]
    acc = _dot(ya_ref[...], wa_ref[0]) + _dot(yb_ref[...], wb_ref[0])
    row = (pl.program_id(0) % tiles_per_sample) * tm + lax.broadcasted_iota(jnp.int32, acc.shape, 0)
    gate = jnp.where(row < lc, mc_ref[0, k_gate:k_gate + 1, :], ml_ref[0, k_gate:k_gate + 1, :])
    o_ref[...] = x_ref[...] + gate * acc


def _outproj(ya, yb, w_out, layer, xx, mod, lc, k_gate):
    bsz, lt, d = xx.shape
    t = bsz * lt
    ga = ya.shape[-1]
    gb = yb.shape[-1]
    tm = _pick(lt, (1152, 768, 576, 384, 256, 128))
    tn = _pick(d, (512, 256, 128))
    tps = lt // tm
    out = pl.pallas_call(
        functools.partial(_outproj_kernel, lc=lc, tiles_per_sample=tps, k_gate=k_gate),
        grid=(t // tm, d // tn),
        in_specs=[
            pl.BlockSpec((tm, ga), lambda i, j: (i, 0)),
            pl.BlockSpec((tm, gb), lambda i, j: (i, 0)),
            pl.BlockSpec((1, ga, tn), lambda i, j: (layer, 0, j)),
            pl.BlockSpec((1, gb, tn), lambda i, j: (layer, ga // gb, j)),
            pl.BlockSpec((tm, tn), lambda i, j: (i, j)),
            pl.BlockSpec((1, N_MOD, tn), lambda i, j: (i // tps, 0, j)),
            pl.BlockSpec((1, N_MOD, tn), lambda i, j: (bsz, 0, j)),
        ],
        out_specs=pl.BlockSpec((tm, tn), lambda i, j: (i, j)),
        out_shape=jax.ShapeDtypeStruct((t, d), _F32),
        compiler_params=_cp(("arbitrary", "arbitrary")), name="out_proj",
    )(ya.reshape(t, ga), yb.reshape(t, gb), w_out, w_out, xx.reshape(t, d), mod, mod)
    return out.reshape(bsz, lt, d)


def _route_kernel(lg_ref, br_ref, su_ref, oi_ref, ow_ref, cnt_ref, carry_ref, *, ne):
    epg = ne // N_EXPERT_GROUPS
    tq = lg_ref.shape[1]

    @pl.when(pl.program_id(0) == 0)
    def _():
        carry_ref[...] = jnp.zeros_like(carry_ref)

    sc = _sigmoid(lg_ref[...])
    sel = sc + br_ref[...]
    s = [sel[e:e + 1, :] for e in range(ne)]

    def top2(vals):
        a, b, c_, d_ = vals
        hi1, lo1 = jnp.maximum(a, b), jnp.minimum(a, b)
        hi2, lo2 = jnp.maximum(c_, d_), jnp.minimum(c_, d_)
        return jnp.maximum(hi1, hi2) + jnp.maximum(jnp.maximum(lo1, lo2), jnp.minimum(hi1, hi2))

    gs = [top2(s[g * epg:(g + 1) * epg]) for g in range(N_EXPERT_GROUPS)]
    best = gs[0]
    gsel = jnp.zeros_like(best, dtype=jnp.int32)
    for g in range(1, N_EXPERT_GROUPS):
        better = gs[g] > best
        gsel = jnp.where(better, g, gsel)
        best = jnp.maximum(best, gs[g])
    ing = []
    for j in range(epg):
        v = s[j]
        for g in range(1, N_EXPERT_GROUPS):
            v = jnp.where(gsel == g, s[g * epg + j], v)
        ing.append(v)
    b1 = ing[0]
    i1 = jnp.zeros_like(gsel)
    for j in range(1, epg):
        better = ing[j] > b1
        i1 = jnp.where(better, j, i1)
        b1 = jnp.maximum(b1, ing[j])
    b2 = jnp.full_like(b1, -jnp.inf)
    i2 = jnp.zeros_like(gsel)
    for j in range(epg):
        better = (ing[j] > b2) & (i1 != j)
        i2 = jnp.where(better, j, i2)
        b2 = jnp.where(better, ing[j], b2)
    e1 = gsel * epg + i1
    e2 = gsel * epg + i2

    eio = lax.broadcasted_iota(jnp.int32, (ne, tq), 0)
    oh1 = eio == e1
    oh2 = eio == e2
    w1 = jnp.sum(jnp.where(oh1, sc, 0.0), axis=0, keepdims=True)
    w2 = jnp.sum(jnp.where(oh2, sc, 0.0), axis=0, keepdims=True)
    wsum = w1 + w2
    oh = jnp.where(oh1, 1.0, 0.0) + jnp.where(oh2, 1.0, 0.0)
    rank = _dot(oh.astype(_BF16), su_ref[...]) + carry_ref[:, 0:1]
    r1 = jnp.sum(jnp.where(oh1, rank, 0.0), axis=0, keepdims=True)
    r2 = jnp.sum(jnp.where(oh2, rank, 0.0), axis=0, keepdims=True)
    carry = carry_ref[...] + jnp.sum(oh, axis=1, keepdims=True)
    carry_ref[...] = carry
    cnt_ref[...] = carry

    sub = lax.broadcasted_iota(jnp.int32, (8, tq), 0)
    oi = jnp.where(sub == 0, e1, jnp.where(sub == 1, e2,
         jnp.where(sub == 2, r1.astype(jnp.int32), jnp.where(sub == 3, r2.astype(jnp.int32), 0))))
    oi_ref[...] = oi
    ow_ref[...] = jnp.where(sub == 0, w1 / wsum, jnp.where(sub == 1, w2 / wsum, 0.0))


def _route(logits_t, b_router):
    ne, t = logits_t.shape
    tq = ROW_TILE
    su = np.triu(np.ones((tq, tq), np.float32), 1)
    return pl.pallas_call(
        functools.partial(_route_kernel, ne=ne),
        grid=(t // tq,),
        in_specs=[pl.BlockSpec((ne, tq), lambda i: (0, i)),
                  pl.BlockSpec((ne, 1), lambda i: (0, 0)),
                  pl.BlockSpec((tq, tq), lambda i: (0, 0))],
        out_specs=[pl.BlockSpec((8, tq), lambda i: (0, i)),
                   pl.BlockSpec((8, tq), lambda i: (0, i)),
                   pl.BlockSpec((ne, 128), lambda i: (0, 0))],
        out_shape=[jax.ShapeDtypeStruct((8, t), jnp.int32),
                   jax.ShapeDtypeStruct((8, t), _F32),
                   jax.ShapeDtypeStruct((ne, 128), _F32)],
        scratch_shapes=[pltpu.VMEM((ne, 128), _F32)],
        compiler_params=_cp(("arbitrary",)), name="router",
    )(logits_t, b_router.reshape(ne, 1).astype(_F32), jnp.asarray(su, _BF16))


def _new_weight_block(te_ref, i):
    return (i == 0) | (te_ref[i] != te_ref[jnp.maximum(i - 1, 0)])


def _moe_up_kernel(te_ref, x_ref, wg_ref, wu_ref, h_ref, wg16_ref, wu16_ref):
    i = pl.program_id(1)
    n_valid = te_ref[te_ref.shape[0] - 1]

    @pl.when(_new_weight_block(te_ref, i))
    def _():
        wg16_ref[...] = wg_ref[0, 0].astype(_BF16)
        wu16_ref[...] = wu_ref[0, 0].astype(_BF16)

    @pl.when(i < n_valid)
    def _():
        x = x_ref[...]
        gt = _dot(x, wg16_ref[...])
        up = _dot(x, wu16_ref[...])
        h_ref[...] = (_silu(gt) * up).astype(h_ref.dtype)

    @pl.when(i >= n_valid)
    def _():
        h_ref[...] = jnp.zeros_like(h_ref)


def _moe_down_kernel(te_ref, *refs, n_parts, part_tiles):
    h_refs = refs[:n_parts]
    wd_ref, o_ref, wd16_ref = refs[n_parts:]
    i = pl.program_id(1)
    n_valid = te_ref[te_ref.shape[0] - 1]

    @pl.when(_new_weight_block(te_ref, i))
    def _():
        wd16_ref[...] = wd_ref[0, 0].astype(_BF16)

    for part in range(n_parts):
        @pl.when((i >= part * part_tiles) & (i < jnp.minimum(n_valid, (part + 1) * part_tiles)))
        def _(h_ref=h_refs[part]):
            o_ref[...] = _dot(h_ref[...], wd16_ref[...]).astype(o_ref.dtype)

    @pl.when(i >= n_valid)
    def _():
        o_ref[...] = jnp.zeros_like(o_ref)


def _moe_experts(parts, tile_e, w_gate, w_up, w_down, layer):
    d = parts[0][0].shape[1]
    f = w_gate.shape[-1]
    tm = MOE_TILE
    tf = _pick(f, (MOE_F_TILE, 256, 128))
    td = _pick(d, (MOE_D_TILE, 1024, 512, 256, 128))
    hids = []
    for xs, te_half in parts:
        nth = xs.shape[0] // tm
        hids.append(pl.pallas_call(
            _moe_up_kernel,
            grid_spec=pltpu.PrefetchScalarGridSpec(
                num_scalar_prefetch=1, grid=(f // tf, nth),
                in_specs=[pl.BlockSpec((tm, d), lambda j, i, te: (i, 0)),
                          pl.BlockSpec((1, 1, d, tf), lambda j, i, te: (layer, te[i], 0, j)),
                          pl.BlockSpec((1, 1, d, tf), lambda j, i, te: (layer, te[i], 0, j))],
                out_specs=pl.BlockSpec((tm, tf), lambda j, i, te: (i, j)),
                scratch_shapes=[pltpu.VMEM((d, tf), _BF16), pltpu.VMEM((d, tf), _BF16)]),
            out_shape=jax.ShapeDtypeStruct((nth * tm, f), _BF16),
            compiler_params=_cp(("arbitrary", "arbitrary")), name="moe_up",
        )(te_half, xs, w_gate, w_up))
    n_parts = len(hids)
    part_tiles = hids[0].shape[0] // tm
    nt = n_parts * part_tiles

    def part_spec(part):
        return pl.BlockSpec((tm, f), lambda j, i, te: (jnp.clip(i - part * part_tiles, 0, part_tiles - 1), 0))

    return pl.pallas_call(
        functools.partial(_moe_down_kernel, n_parts=n_parts, part_tiles=part_tiles),
        grid_spec=pltpu.PrefetchScalarGridSpec(
            num_scalar_prefetch=1, grid=(d // td, nt),
            in_specs=[part_spec(part) for part in range(n_parts)]
                     + [pl.BlockSpec((1, 1, f, td), lambda j, i, te: (layer, te[i], 0, j))],
            out_specs=pl.BlockSpec((tm, td), lambda j, i, te: (i, j)),
            scratch_shapes=[pltpu.VMEM((f, td), _BF16)]),
        out_shape=jax.ShapeDtypeStruct((nt * tm, d), _BF16),
        compiler_params=_cp(("arbitrary", "arbitrary")), name="moe_down",
    )(tile_e, *hids, w_down)


def _comb_kernel(x_ref, y0_ref, y1_ref, w_ref, mod_ref, o_ref, *, k_gate):
    y = y0_ref[0].astype(_F32) * w_ref[0, :, 0:1] + y1_ref[0].astype(_F32) * w_ref[0, :, 1:2]
    o_ref[0] = x_ref[0] + mod_ref[0, k_gate:k_gate + 1, :] * y


def _comb_norm_kernel(x_ref, y0_ref, y1_ref, w_ref, mod_ref, nw_ref, modn_ref, o_ref, a_ref, *, k_gate):
    y = y0_ref[0].astype(_F32) * w_ref[0, :, 0:1] + y1_ref[0].astype(_F32) * w_ref[0, :, 1:2]
    o = x_ref[0] + mod_ref[0, k_gate:k_gate + 1, :] * y
    o_ref[0] = o
    a_ref[0] = _modnorm(o, nw_ref[...], modn_ref[0], 0, 1).astype(a_ref.dtype)


def _combine(xx, y0, y1, w01, mod, lc, k_gate, latent_only, next_norm=None):
    bsz, lt, d = xx.shape
    tr = ROW_TILE
    skip = lc // tr if latent_only else 0
    nt = lt // tr - skip
    row = lambda b, i: (b, i + skip, 0)
    yrow = lambda b, i: (b, i, 0)
    mrow = _mod_row(bsz, lc // tr)
    mod_spec = pl.BlockSpec((1, N_MOD, d), lambda b, i: mrow(b, i + skip))
    in_specs = [pl.BlockSpec((1, tr, d), row), pl.BlockSpec((1, tr, d), yrow), pl.BlockSpec((1, tr, d), yrow),
                pl.BlockSpec((1, tr, 2), yrow), mod_spec]
    o_spec = pl.BlockSpec((1, tr, d), yrow)
    o_shape = jax.ShapeDtypeStruct((bsz, nt * tr, d), _F32)
    if next_norm is None:
        return pl.pallas_call(
            functools.partial(_comb_kernel, k_gate=k_gate),
            grid=(bsz, nt), in_specs=in_specs, out_specs=o_spec, out_shape=o_shape,
            compiler_params=_cp(("arbitrary", "arbitrary")), name="moe_combine",
        )(xx, y0, y1, w01, mod)
    nw_next, mod_next = next_norm
    return pl.pallas_call(
        functools.partial(_comb_norm_kernel, k_gate=k_gate),
        grid=(bsz, nt),
        in_specs=in_specs + [pl.BlockSpec((1, d), lambda b, i: (0, 0)), mod_spec],
        out_specs=[o_spec, o_spec],
        out_shape=[o_shape, jax.ShapeDtypeStruct((bsz, nt * tr, d), _BF16)],
        compiler_params=_cp(("arbitrary", "arbitrary")), name="moe_combine_norm",
    )(xx, y0, y1, w01, mod, nw_next.reshape(1, d), mod_next)


def _moe(f, logits_t, b_router, w_gate, w_up, w_down, layer):
    t, d = f.shape
    ne = w_gate.shape[1]
    tm = MOE_TILE
    oi, ow, cnt = _route(logits_t, b_router)
    e01 = oi[0:2]
    r01 = oi[2:4]
    counts = cnt[:, 0].astype(jnp.int32)
    tiles = (counts + tm - 1) // tm
    tile_end = jnp.cumsum(tiles)
    off = (tile_end - tiles) * tm
    first = jnp.sum(jnp.where(e01[None] == jnp.arange(ne, dtype=jnp.int32)[:, None, None],
                              off[:, None, None], 0), axis=0)
    pos = first + r01
    nt = (t * TOP_K) // tm + ne
    p = nt * tm
    tok = jnp.broadcast_to(jnp.arange(t, dtype=jnp.int32)[None, :], (TOP_K, t))
    src = (jnp.arange(p, dtype=jnp.int32) % t).at[pos.reshape(-1)].set(tok.reshape(-1))
    n_valid = tile_end[-1]
    tile_idx = jnp.arange(nt, dtype=jnp.int32)
    tile_e = jnp.sum((tile_end[None, :] <= jnp.minimum(tile_idx, n_valid - 1)[:, None]).astype(jnp.int32), axis=1)
    assert nt % MOE_GATHER_PARTS == 0
    part_tiles = nt // MOE_GATHER_PARTS
    parts = []
    for lo in range(0, nt, part_tiles):
        hi = lo + part_tiles
        n_val = jnp.clip(n_valid - lo, 0, part_tiles).astype(jnp.int32)
        parts.append((f.at[src[lo * tm:hi * tm]].get(mode="promise_in_bounds"),
                      jnp.concatenate([tile_e[lo:hi], n_val[None]])))
    tile_e = jnp.concatenate([tile_e, n_valid[None].astype(jnp.int32)])
    out = _moe_experts(parts, tile_e, w_gate, w_up, w_down, layer)
    y0 = out.at[pos[0]].get(mode="promise_in_bounds")
    y1 = out.at[pos[1]].get(mode="promise_in_bounds")
    return y0, y1, jnp.transpose(ow[0:2])


def kernel(x, c, ctx, c_ctx, norm1_w, norm2_w, w_ada, b_ada, w_in, conv_w, a_log, dt_bias, gdn_norm_w,
           qk_norm_w, na_rpb, w_out, w_router, b_router, w_gate, w_up, w_down):
    bsz, s_len, d = x.shape
    lc = ctx.shape[1]
    lt = lc + s_len
    t = bsz * lt
    depth = w_ada.shape[0]
    gw = d // 2
    naw = d - gw
    nh_a = gw // HEAD_DIM
    nh_b = naw // HEAD_DIM
    rows = s_len // GRID_W
    assert s_len % GRID_W == 0 and rows >= NA_WIN_R
    assert lc % ROW_TILE == 0 and s_len % ROW_TILE == 0
    hg_a = min(GDN_HEADS_PER_STEP, nh_a)
    hg_b = min(NA_HEADS_PER_STEP, nh_b)
    nhg_a = nh_a // hg_a

    xx = jnp.concatenate([ctx, x], axis=1)
    mp = -(-(bsz + 1) // 8) * 8
    cc = jnp.zeros((mp, d), _F32).at[:bsz].set(c).at[bsz].set(c_ctx)
    mod_all = _ada(cc, w_ada, b_ada).reshape(depth, mp, N_MOD, d)
    w_router_t = jnp.transpose(w_router).astype(_F32)

    col_b = 4 * gw + 4 * nh_a
    w_in16 = jnp.swapaxes(w_in, 1, 2).astype(_BF16)
    w_b = w_in16[:, col_b:, :]
    w_ab = jnp.zeros((depth, LANES, d), _BF16).at[:, :4 * nh_a, :].set(w_in16[:, 4 * gw:col_b, :])
    w_out16 = w_out.astype(_BF16)

    for layer in range(depth):
        last = layer == depth - 1
        mod = mod_all[layer]

        if layer == 0:
            a = _norm(xx, norm1_w[layer], mod, lc, 0, 1)
        a2 = a.reshape(t, d)
        proj_a = _matmul(a2, w_in16, layer, 4 * gw, _BF16).reshape(bsz, lt, 4 * gw)
        proj_b = _matmul(a2, w_b, layer, 3 * naw, _BF16).reshape(bsz, lt, 3 * naw)
        ab = _matmul(a2, w_ab, layer, LANES, _F32)[:, :4 * nh_a].reshape(bsz, lt, 2, 2, nhg_a, hg_a)

        q, k, v = _gprep(proj_a, conv_w[layer], lc, nh_a)
        lane_pad = GATE_LANES - 2 * hg_a
        ab_r = jnp.transpose(ab, (3, 4, 0, 1, 2, 5)).reshape(2, nhg_a, bsz, lt, 2 * hg_a)
        ab_r = jnp.pad(ab_r, ((0, 0),) * 4 + ((0, lane_pad),))
        zpad = jnp.zeros((2, nhg_a, 1, hg_a), _F32)
        zlane = jnp.zeros((2, nhg_a, 1, lane_pad), _F32)
        alog_r = jnp.concatenate([zpad, a_log[layer].astype(_F32).reshape(2, nhg_a, 1, hg_a), zlane], axis=-1)
        dtb_r = jnp.concatenate([zpad, dt_bias[layer].astype(_F32).reshape(2, nhg_a, 1, hg_a), zlane], axis=-1)
        o_fwd, o_bwd = _gdn(q, k, v, ab_r, alog_r, dtb_r, lc, hg_a)
        ya = _gdnout(o_fwd, o_bwd, proj_a, 3, gdn_norm_w[layer])

        tab = _na_bias_table(na_rpb[layer], rows)
        nb = naw // (hg_b * HEAD_DIM)
        yb = _na(proj_b, 0, nb, 2 * nb, tab, qk_norm_w[layer].astype(_F32), lc, nh_b, hg_b)

        xx = _outproj(ya, yb, w_out16, layer, xx, mod, lc, 2)

        f, logits_t = _norm(xx, norm2_w[layer], mod, lc, 3, 4, w_router_t, latent_only=last)
        rows_moe = f.shape[1]
        y0, y1, w01 = _moe(f.reshape(bsz * rows_moe, d), logits_t, b_router, w_gate, w_up, w_down, layer)
        ys = (y0.reshape(bsz, rows_moe, d), y1.reshape(bsz, rows_moe, d), w01.reshape(bsz, rows_moe, 2))
        if last:
            xx = _combine(xx, *ys, mod, lc, 5, True)
        else:
            xx, a = _combine(xx, *ys, mod, lc, 5, False, next_norm=(norm1_w[layer + 1], mod_all[layer + 1]))
    return xx
```
